```python
import math
import jax, jax.numpy as jnp
from jax import lax
import numpy as np

D_MODEL = 1024
BATCH = 32
SEQ = 2048
DEPTH = 2

N_A_LAYERS = DEPTH // 2
N_B_LAYERS = DEPTH - N_A_LAYERS
D_FF = 2816
RMS_EPS = 1e-6

S5_GROUP_CH = 16
S5_GROUPS = D_MODEL // S5_GROUP_CH
S5_STATE = 64
DT_MIN = 0.001
DT_MAX = 0.1

N_HEADS = 16
HEAD_DIM = D_MODEL // N_HEADS
N_KV_HEADS = 4
Q_PER_KV = N_HEADS // N_KV_HEADS
CMP_LEN = 32
CMP_STRIDE = 16
CMP_HIDDEN = 2 * HEAD_DIM
SEL_LEN = 64
SEL_TOPN = 8
WIN = 512
WIN_QB = 128
SEL_QB = 64
N_BRANCH = 3
ATTN_SCALE = HEAD_DIM ** -0.5

NUM_BUCKETS = 32
MAX_DISTANCE = 128

NEG_INF = -1e30
BIG = 1e9

kernel_name = 'yoco_s5_nsa_macaron_trunk'


def _rmsnorm(x, g):
    xf = x.astype(jnp.float32)
    y = xf * lax.rsqrt(jnp.mean(xf * xf, axis=-1, keepdims=True) + RMS_EPS)
    return (y * g.astype(jnp.float32)).astype(x.dtype)


def _swiglu(x, w_in, w_out):
    a, b = jnp.split(x @ w_in, 2, axis=-1)
    return (jax.nn.silu(a) * b) @ w_out


def _rel_bucket(dist):
    n = jnp.maximum(dist, 0)
    max_exact = NUM_BUCKETS // 2
    logv = jnp.log(jnp.maximum(n, 1).astype(jnp.float32) / max_exact) / math.log(MAX_DISTANCE / max_exact)
    large = jnp.minimum(max_exact + (logv * (NUM_BUCKETS - max_exact)).astype(jnp.int32), NUM_BUCKETS - 1)
    return jnp.where(n < max_exact, n, large)


def _masked_softmax(logits, mask):
    return jax.nn.softmax(jnp.where(mask, logits, NEG_INF), axis=-1)


def _s5_mixer(u, a_re, a_im, log_dt, b_re, b_im, c_re, c_im, d_skip, w_glu):
    bsz, seq, _ = u.shape
    ug = u.reshape(bsz, seq, S5_GROUPS, S5_GROUP_CH)
    dt = jnp.exp(log_dt)[:, None]
    mag = jnp.exp(a_re * dt)
    ab_re = mag * jnp.cos(a_im * dt)
    ab_im = mag * jnp.sin(a_im * dt)
    den = a_re * a_re + a_im * a_im
    z_re = ((ab_re - 1.0) * a_re + ab_im * a_im) / den
    z_im = (ab_im * a_re - (ab_re - 1.0) * a_im) / den
    bb_re = z_re[..., None] * b_re - z_im[..., None] * b_im
    bb_im = z_re[..., None] * b_im + z_im[..., None] * b_re
    bu_re = jnp.einsum('bsgh,gph->bsgp', ug, bb_re)
    bu_im = jnp.einsum('bsgh,gph->bsgp', ug, bb_im)
    shape = (1, seq, S5_GROUPS, S5_STATE)
    la_re = jnp.broadcast_to(ab_re, shape)
    la_im = jnp.broadcast_to(ab_im, shape)

    def combine(e1, e2):
        ar1, ai1, br1, bi1 = e1
        ar2, ai2, br2, bi2 = e2
        return (ar2 * ar1 - ai2 * ai1,
                ar2 * ai1 + ai2 * ar1,
                ar2 * br1 - ai2 * bi1 + br2,
                ar2 * bi1 + ai2 * br1 + bi2)

    _, _, x_re, x_im = lax.associative_scan(combine, (la_re, la_im, bu_re, bu_im), axis=1)
    y = (jnp.einsum('bsgp,ghp->bsgh', x_re, c_re) - jnp.einsum('bsgp,ghp->bsgh', x_im, c_im)
         + d_skip * ug)
    y = jax.nn.gelu(y.reshape(bsz, seq, D_MODEL))
    return y * jax.nn.sigmoid(y @ w_glu)


def _shared_kv(h, kv_norm, w_kv, k_norm_cmp, k_norm_slc, k_norm_win,
               cmp_pos_k, cmp_pos_v, cmp_k_w1, cmp_k_w2, cmp_v_w1, cmp_v_w2):
    bsz, seq, _ = h.shape
    kv = (_rmsnorm(h, kv_norm) @ w_kv).reshape(bsz, seq, 2 * N_BRANCH, N_KV_HEADS, HEAD_DIM)
    n_cmp = (seq - CMP_LEN) // CMP_STRIDE + 1
    blk = jnp.arange(n_cmp)[:, None] * CMP_STRIDE + jnp.arange(CMP_LEN)[None, :]

    def compress(src, pos, w1, w2):
        blocks = src[:, blk] + pos[None, None, :, None, :]
        hid = jax.nn.gelu(jnp.einsum('bclgd,ldh->bcgh', blocks, w1))
        return jnp.einsum('bcgh,hd->bcgd', hid, w2)

    k_cmp = _rmsnorm(compress(kv[:, :, 0], cmp_pos_k, cmp_k_w1, cmp_k_w2), k_norm_cmp)
    v_cmp = compress(kv[:, :, 1], cmp_pos_v, cmp_v_w1, cmp_v_w2)
    k_slc = _rmsnorm(kv[:, :, 2], k_norm_slc)
    v_slc = kv[:, :, 3]
    k_win = _rmsnorm(kv[:, :, 4], k_norm_win)
    v_win = kv[:, :, 5]
    return (k_cmp, v_cmp, k_slc, v_slc, k_win, v_win)


def _compressed_branch(q, k_cmp, v_cmp, rel_bias):
    seq = q.shape[1]
    n_cmp = k_cmp.shape[1]
    n_sel = seq // SEL_LEN
    t = jnp.arange(seq)
    c_start = jnp.arange(n_cmp) * CMP_STRIDE
    dist = t[:, None] - (c_start + CMP_LEN - 1)[None, :]
    valid = dist >= 0
    bias = rel_bias[_rel_bucket(dist)].transpose(2, 0, 1).reshape(N_KV_HEADS, Q_PER_KV, seq, n_cmp)
    logits = jnp.einsum('bsgrd,bcgd->bgrsc', q, k_cmp).astype(jnp.float32) * ATTN_SCALE + bias
    p = _masked_softmax(logits, valid) * jnp.any(valid, axis=-1)[:, None].astype(jnp.float32)
    o = jnp.einsum('bgrsc,bcgd->bsgrd', p.astype(v_cmp.dtype), v_cmp)
    j_start = jnp.arange(n_sel) * SEL_LEN
    ov = jnp.clip(jnp.minimum(c_start[:, None] + CMP_LEN, j_start[None, :] + SEL_LEN)
                  - jnp.maximum(c_start[:, None], j_start[None, :]), 0, None)
    overlap = ov.astype(jnp.float32) / CMP_LEN
    p_slc = jnp.einsum('bgrsc,cj->bgsj', p, overlap)
    return o, p_slc


def _select_blocks(p_slc):
    seq, n_sel = p_slc.shape[2], p_slc.shape[3]
    k = min(SEL_TOPN, n_sel)
    t = jnp.arange(seq)[:, None]
    j = jnp.arange(n_sel)[None, :]
    cur = t // SEL_LEN
    forced = (j == 0) | (j == cur) | (j == cur - 1)
    causal = j * SEL_LEN <= t
    score = jnp.where(forced, BIG, jnp.where(causal, p_slc, -BIG))
    _, idx = lax.top_k(score, k)
    return idx


def _selected_branch(q, k_slc, v_slc, idx, rel_bias):
    bsz, seq = q.shape[0], q.shape[1]
    n_top = idx.shape[-1]
    kl = n_top * SEL_LEN
    n_ch = seq // SEL_QB
    kst = k_slc.transpose(0, 2, 1, 3)
    vst = v_slc.transpose(0, 2, 1, 3)
    table = rel_bias.reshape(NUM_BUCKETS, N_KV_HEADS, Q_PER_KV).transpose(1, 0, 2)
    gather = jax.vmap(jax.vmap(lambda a, i: a[i]))
    qc = q.reshape(bsz, n_ch, SEL_QB, N_KV_HEADS, Q_PER_KV, HEAD_DIM).transpose(1, 0, 2, 3, 4, 5)
    ic = idx.reshape(bsz, N_KV_HEADS, n_ch, SEL_QB, n_top).transpose(2, 0, 1, 3, 4)

    def step(args):
        ci, qb, ib = args
        tok = (ib[..., None] * SEL_LEN + jnp.arange(SEL_LEN)).reshape(bsz, N_KV_HEADS, SEL_QB * kl)
        kg = gather(kst, tok).reshape(bsz, N_KV_HEADS, SEL_QB, kl, HEAD_DIM)
        vg = gather(vst, tok).reshape(bsz, N_KV_HEADS, SEL_QB, kl, HEAD_DIM)
        tok = tok.reshape(bsz, N_KV_HEADS, SEL_QB, kl)
        qpos = ci * SEL_QB + jnp.arange(SEL_QB)
        dist = qpos[None, None, :, None] - tok
        bias = table[jnp.arange(N_KV_HEADS)[None, :, None, None], _rel_bucket(dist)]
        logits = (jnp.einsum('bqgrd,bgqkd->bgrqk', qb, kg).astype(jnp.float32) * ATTN_SCALE
                  + bias.transpose(0, 1, 4, 2, 3))
        p = _masked_softmax(logits, (dist >= 0)[:, :, None])
        return jnp.einsum('bgrqk,bgqkd->bqgrd', p.astype(vg.dtype), vg)

    out = lax.map(step, (jnp.arange(n_ch), qc, ic))
    return out.transpose(1, 0, 2, 3, 4, 5).reshape(bsz, seq, N_KV_HEADS, Q_PER_KV, HEAD_DIM)


def _window_branch(q, k_win, v_win, rel_bias):
    bsz, seq = q.shape[0], q.shape[1]
    n_blk = seq // WIN_QB
    span = WIN + WIN_QB
    kp = jnp.pad(k_win, ((0, 0), (WIN, 0), (0, 0), (0, 0)))
    vp = jnp.pad(v_win, ((0, 0), (WIN, 0), (0, 0), (0, 0)))
    dist = WIN + jnp.arange(WIN_QB)[:, None] - jnp.arange(span)[None, :]
    band = (dist >= 0) & (dist < WIN)
    bias = rel_bias[_rel_bucket(dist)].transpose(2, 0, 1).reshape(N_KV_HEADS, Q_PER_KV, WIN_QB, span)
    qb_all = q.reshape(bsz, n_blk, WIN_QB, N_KV_HEADS, Q_PER_KV, HEAD_DIM).transpose(1, 0, 2, 3, 4, 5)

    def step(args):
        bi, qb = args
        start = bi * WIN_QB
        kb = lax.dynamic_slice_in_dim(kp, start, span, axis=1)
        vb = lax.dynamic_slice_in_dim(vp, start, span, axis=1)
        kpos = start - WIN + jnp.arange(span)
        mask = band & (kpos >= 0)[None, :]
        logits = jnp.einsum('bqgrd,bkgd->bgrqk', qb, kb).astype(jnp.float32) * ATTN_SCALE + bias
        p = _masked_softmax(logits, mask)
        return jnp.einsum('bgrqk,bkgd->bqgrd', p.astype(vb.dtype), vb)

    out = lax.map(step, (jnp.arange(n_blk), qb_all))
    return out.transpose(1, 0, 2, 3, 4, 5).reshape(bsz, seq, N_KV_HEADS, Q_PER_KV, HEAD_DIM)


def _nsa_mixer(u, kv, w_qg, q_norm, w_o, rel_bias):
    k_cmp, v_cmp, k_slc, v_slc, k_win, v_win = kv
    bsz, seq, _ = u.shape
    qg = u @ w_qg
    q = _rmsnorm(qg[..., :N_HEADS * HEAD_DIM].reshape(bsz, seq, N_KV_HEADS, Q_PER_KV, HEAD_DIM), q_norm)
    gates = jax.nn.sigmoid(qg[..., N_HEADS * HEAD_DIM:].reshape(bsz, seq, N_KV_HEADS, Q_PER_KV, N_BRANCH))
    o_cmp, p_slc = _compressed_branch(q, k_cmp, v_cmp, rel_bias)
    idx = _select_blocks(p_slc)
    o_slc = _selected_branch(q, k_slc, v_slc, idx, rel_bias)
    o_win = _window_branch(q, k_win, v_win, rel_bias)
    o = gates[..., 0:1] * o_cmp + gates[..., 1:2] * o_slc + gates[..., 2:3] * o_win
    return o.reshape(bsz, seq, D_MODEL) @ w_o


def setup_inputs(seed: int = 0) -> dict:
    key = jax.random.key(seed)
    ks = iter(jax.random.split(key, 40))

    def nrm(shape, scale):
        return jax.random.normal(next(ks), shape, jnp.float32) * scale

    def gain(shape):
        return 1.0 + nrm(shape, 0.02)

    na, nb = N_A_LAYERS, N_B_LAYERS
    g, p, ch = S5_GROUPS, S5_STATE, S5_GROUP_CH
    return {
        'x': nrm((BATCH, SEQ, D_MODEL), 1.0),
        'rel_bias': nrm((NUM_BUCKETS, N_HEADS), 0.5),
        'ffn1_norm': gain((DEPTH, D_MODEL)),
        'ffn1_w_in': nrm((DEPTH, D_MODEL, 2 * D_FF), D_MODEL ** -0.5),
        'ffn1_w_out': nrm((DEPTH, D_FF, D_MODEL), D_FF ** -0.5),
        'mix_norm': gain((DEPTH, D_MODEL)),
        'ffn2_norm': gain((DEPTH, D_MODEL)),
        'ffn2_w_in': nrm((DEPTH, D_MODEL, 2 * D_FF), D_MODEL ** -0.5),
        'ffn2_w_out': nrm((DEPTH, D_FF, D_MODEL), D_FF ** -0.5),
        's5_a_re': -0.5 + nrm((na, g, p), 0.01),
        's5_a_im': jnp.pi * jnp.arange(p, dtype=jnp.float32)[None, None, :] + nrm((na, g, p), 0.01),
        's5_log_dt': jax.random.uniform(next(ks), (na, g), jnp.float32, math.log(DT_MIN), math.log(DT_MAX)),
        's5_b_re': nrm((na, g, p, ch), (2.0 * ch) ** -0.5),
        's5_b_im': nrm((na, g, p, ch), (2.0 * ch) ** -0.5),
        's5_c_re': nrm((na, g, ch, p), (2.0 * p) ** -0.5),
        's5_c_im': nrm((na, g, ch, p), (2.0 * p) ** -0.5),
        's5_d': nrm((na, g, ch), 1.0),
        's5_w_glu': nrm((na, D_MODEL, D_MODEL), D_MODEL ** -0.5),
        'kv_norm': gain((D_MODEL,)),
        'w_kv': nrm((D_MODEL, 2 * N_BRANCH * N_KV_HEADS * HEAD_DIM), D_MODEL ** -0.5),
        'k_norm_cmp': gain((HEAD_DIM,)),
        'k_norm_slc': gain((HEAD_DIM,)),
        'k_norm_win': gain((HEAD_DIM,)),
        'cmp_pos_k': nrm((CMP_LEN, HEAD_DIM), 0.1),
        'cmp_pos_v': nrm((CMP_LEN, HEAD_DIM), 0.1),
        'cmp_k_w1': nrm((CMP_LEN, HEAD_DIM, CMP_HIDDEN), (CMP_LEN * HEAD_DIM) ** -0.5),
        'cmp_k_w2': nrm((CMP_HIDDEN, HEAD_DIM), CMP_HIDDEN ** -0.5),
        'cmp_v_w1': nrm((CMP_LEN, HEAD_DIM, CMP_HIDDEN), (CMP_LEN * HEAD_DIM) ** -0.5),
        'cmp_v_w2': nrm((CMP_HIDDEN, HEAD_DIM), CMP_HIDDEN ** -0.5),
        'w_qg': nrm((nb, D_MODEL, N_HEADS * HEAD_DIM + N_BRANCH * N_HEADS), D_MODEL ** -0.5),
        'q_norm': gain((nb, HEAD_DIM)),
        'w_o': nrm((nb, D_MODEL, D_MODEL), D_MODEL ** -0.5),
    }


def reference(x, rel_bias, ffn1_norm, ffn1_w_in, ffn1_w_out, mix_norm, ffn2_norm, ffn2_w_in, ffn2_w_out,
              s5_a_re, s5_a_im, s5_log_dt, s5_b_re, s5_b_im, s5_c_re, s5_c_im, s5_d, s5_w_glu,
              kv_norm, w_kv, k_norm_cmp, k_norm_slc, k_norm_win, cmp_pos_k, cmp_pos_v,
              cmp_k_w1, cmp_k_w2, cmp_v_w1, cmp_v_w2, w_qg, q_norm, w_o):
    h = x
    kv = None
    for layer in range(DEPTH):
        h = h + 0.5 * _swiglu(_rmsnorm(h, ffn1_norm[layer]), ffn1_w_in[layer], ffn1_w_out[layer])
        u = _rmsnorm(h, mix_norm[layer])
        if layer < N_A_LAYERS:
            a = layer
            h = h + _s5_mixer(u, s5_a_re[a], s5_a_im[a], s5_log_dt[a], s5_b_re[a], s5_b_im[a],
                              s5_c_re[a], s5_c_im[a], s5_d[a], s5_w_glu[a])
        else:
            b = layer - N_A_LAYERS
            h = h + _nsa_mixer(u, kv, w_qg[b], q_norm[b], w_o[b], rel_bias)
        h = h + 0.5 * _swiglu(_rmsnorm(h, ffn2_norm[layer]), ffn2_w_in[layer], ffn2_w_out[layer])
        if layer == N_A_LAYERS - 1:
            kv = _shared_kv(h, kv_norm, w_kv, k_norm_cmp, k_norm_slc, k_norm_win,
                            cmp_pos_k, cmp_pos_v, cmp_k_w1, cmp_k_w2, cmp_v_w1, cmp_v_w2)
    return h
```

```python
import functools
import math

import jax
import jax.numpy as jnp
from jax import lax
from jax.experimental import pallas as pl
from jax.experimental.pallas import tpu as pltpu

F32 = jnp.float32
BF16 = jnp.bfloat16

D_MODEL = 1024
D_FF = 2816
RMS_EPS = 1e-6
S5_GROUP_CH = 16
S5_GROUPS = D_MODEL // S5_GROUP_CH
S5_STATE = 64
N_HEADS = 16
HEAD_DIM = 64
N_KV_HEADS = 4
Q_PER_KV = N_HEADS // N_KV_HEADS
CMP_LEN = 32
CMP_STRIDE = 16
CMP_HIDDEN = 2 * HEAD_DIM
SEL_LEN = 64
SEL_TOPN = 8
WIN = 512
N_BRANCH = 3
ATTN_SCALE = HEAD_DIM ** -0.5
NUM_BUCKETS = 32
MAX_DISTANCE = 128
NEG_INF = -1e30
BIG = 1e9

LANES = 128
MXU_DIM = 256
VMEM_LIMIT = 56 * 1024 * 1024

FFN_ROWS = 512
FF_CHUNK = MXU_DIM
S5_TC = 16
S5_SLAB = MXU_DIM
S5_LANES = 256
KV_ROWS = 512
QT = 128
SEL_LANE0 = HEAD_DIM


def _rms(x, n):
    ss = jnp.sum(x * x, axis=-1, keepdims=True)
    return x * lax.rsqrt(ss * (1.0 / n) + RMS_EPS)


def _cparams(sem):
    return pltpu.CompilerParams(dimension_semantics=sem, vmem_limit_bytes=VMEM_LIMIT)


def _const_spec(shape):
    nd = len(shape)
    return pl.BlockSpec(shape, lambda *_: (0,) * nd, pipeline_mode=pl.Buffered(1))


def _ffn_body(x_ref, g_ref, win_ref, wout_ref, o_ref, act_ref):
    x = x_ref[...]
    xn = (_rms(x, D_MODEL) * g_ref[...]).astype(BF16)
    for c in range(D_FF // FF_CHUNK):
        lo = c * FF_CHUNK
        a = jnp.dot(xn, win_ref[:, lo:lo + FF_CHUNK], preferred_element_type=F32)
        b = jnp.dot(xn, win_ref[:, D_FF + lo:D_FF + lo + FF_CHUNK], preferred_element_type=F32)
        act_ref[:, lo:lo + FF_CHUNK] = (a * jax.nn.sigmoid(a) * b).astype(BF16)
    y = jnp.dot(act_ref[...], wout_ref[...], preferred_element_type=F32)
    o_ref[...] = x + 0.5 * y


def _token_spec(rows, bsz, seq, time_major):
    nt = seq // rows
    if time_major:
        return pl.BlockSpec((rows, D_MODEL), lambda b, t: (t, b))
    return pl.BlockSpec((rows, D_MODEL), lambda b, t: (b * nt + t, 0))


def _ffn(h, gain, w_in, w_out, bsz, seq, in_tm, out_tm):
    rows = min(FFN_ROWS, seq)
    out_shape = (seq, bsz * D_MODEL) if out_tm else (bsz * seq, D_MODEL)
    return pl.pallas_call(
        _ffn_body,
        grid=(bsz, seq // rows),
        in_specs=[_token_spec(rows, bsz, seq, in_tm),
                  _const_spec((1, D_MODEL)),
                  _const_spec((D_MODEL, 2 * D_FF)),
                  _const_spec((D_FF, D_MODEL))],
        out_specs=_token_spec(rows, bsz, seq, out_tm),
        out_shape=jax.ShapeDtypeStruct(out_shape, F32),
        scratch_shapes=[pltpu.VMEM((rows, D_FF), BF16)],
        compiler_params=_cparams(("parallel", "parallel")),
        name="ffn",
    )(h, gain.reshape(1, D_MODEL), w_in.astype(BF16), w_out.astype(BF16))


def _s5_body(h_ref, g_ref, bmat_ref, cmat_ref, are_ref, aim_ref, d_ref, wglu_ref,
             o_ref, bu_ref, st_ref, y_ref, *, tc, nb):
    n_slab = D_MODEL // S5_SLAB
    half = S5_SLAB // S5_GROUP_CH * S5_STATE
    rows = tc * nb

    @pl.when(pl.program_id(0) == 0)
    def _():
        st_ref[...] = jnp.zeros_like(st_ref)

    h = h_ref[...].reshape(rows, D_MODEL)
    u = _rms(h, D_MODEL) * g_ref[...]
    ub = u.astype(BF16)
    for sl in range(n_slab):
        bu_ref[...] = jnp.dot(ub[:, sl * S5_SLAB:(sl + 1) * S5_SLAB], bmat_ref[sl],
                              preferred_element_type=F32)
        for wb in range(half // S5_LANES):
            lo = wb * S5_LANES
            ar = jnp.broadcast_to(are_ref[sl, :, lo:lo + S5_LANES], (nb, S5_LANES))
            ai = jnp.broadcast_to(aim_ref[sl, :, lo:lo + S5_LANES], (nb, S5_LANES))

            def step(t, carry, lo=lo, ar=ar, ai=ai):
                xr, xi = carry
                r0 = pl.multiple_of(t * nb, nb)
                br = bu_ref[pl.ds(r0, nb), lo:lo + S5_LANES]
                bi = bu_ref[pl.ds(r0, nb), half + lo:half + lo + S5_LANES]
                nr = ar * xr - ai * xi + br
                ni = ar * xi + ai * xr + bi
                bu_ref[pl.ds(r0, nb), lo:lo + S5_LANES] = nr
                bu_ref[pl.ds(r0, nb), half + lo:half + lo + S5_LANES] = ni
                return nr, ni

            xr, xi = lax.fori_loop(
                0, tc, step,
                (st_ref[sl, :, lo:lo + S5_LANES], st_ref[sl, :, half + lo:half + lo + S5_LANES]))
            st_ref[sl, :, lo:lo + S5_LANES] = xr
            st_ref[sl, :, half + lo:half + lo + S5_LANES] = xi
        y_ref[:, sl * S5_SLAB:(sl + 1) * S5_SLAB] = jnp.dot(
            bu_ref[...].astype(BF16), cmat_ref[sl], preferred_element_type=F32)
    y = jax.nn.gelu(y_ref[...] + d_ref[...] * u)
    gate = jnp.dot(y.astype(BF16), wglu_ref[...], preferred_element_type=F32)
    o_ref[...] = (h + y * jax.nn.sigmoid(gate)).reshape(tc, nb, D_MODEL)


def _s5_params(a_re, a_im, log_dt, b_re, b_im, c_re, c_im):
    dt = jnp.exp(log_dt)[:, None]
    mag = jnp.exp(a_re * dt)
    ab_re = mag * jnp.cos(a_im * dt)
    ab_im = mag * jnp.sin(a_im * dt)
    den = a_re * a_re + a_im * a_im
    z_re = ((ab_re - 1.0) * a_re + ab_im * a_im) / den
    z_im = (ab_im * a_re - (ab_re - 1.0) * a_im) / den
    bb_re = z_re[..., None] * b_re - z_im[..., None] * b_im
    bb_im = z_re[..., None] * b_im + z_im[..., None] * b_re
    n_slab = D_MODEL // S5_SLAB
    gps = S5_GROUPS // n_slab
    eye = jnp.eye(gps, dtype=F32)

    def in_mat(bb):
        bb = bb.reshape(n_slab, gps, S5_STATE, S5_GROUP_CH)
        m = jnp.einsum('sgph,gk->sghkp', bb, eye)
        return m.reshape(n_slab, gps * S5_GROUP_CH, gps * S5_STATE)

    def out_mat(cc):
        cc = cc.reshape(n_slab, gps, S5_GROUP_CH, S5_STATE)
        m = jnp.einsum('sghp,gk->sgpkh', cc, eye)
        return m.reshape(n_slab, gps * S5_STATE, gps * S5_GROUP_CH)

    bmat = jnp.concatenate([in_mat(bb_re), in_mat(bb_im)], axis=2).astype(BF16)
    cmat = jnp.concatenate([out_mat(c_re), out_mat(-c_im)], axis=1).astype(BF16)
    are = ab_re.reshape(n_slab, 1, gps * S5_STATE)
    aim = ab_im.reshape(n_slab, 1, gps * S5_STATE)
    return bmat, cmat, are, aim


def _s5(h_tm, gain, a_re, a_im, log_dt, b_re, b_im, c_re, c_im, d_skip, w_glu, bsz, seq):
    tc = min(S5_TC, seq)
    n_slab = D_MODEL // S5_SLAB
    half = S5_SLAB // S5_GROUP_CH * S5_STATE
    bmat, cmat, are, aim = _s5_params(a_re, a_im, log_dt, b_re, b_im, c_re, c_im)
    blk = pl.BlockSpec((tc, bsz, D_MODEL), lambda i: (i, 0, 0))
    out = pl.pallas_call(
        functools.partial(_s5_body, tc=tc, nb=bsz),
        grid=(seq // tc,),
        in_specs=[blk,
                  _const_spec((1, D_MODEL)),
                  _const_spec((n_slab, S5_SLAB, 2 * half)),
                  _const_spec((n_slab, 2 * half, S5_SLAB)),
                  _const_spec((n_slab, 1, half)),
                  _const_spec((n_slab, 1, half)),
                  _const_spec((1, D_MODEL)),
                  _const_spec((D_MODEL, D_MODEL))],
        out_specs=blk,
        out_shape=jax.ShapeDtypeStruct((seq, bsz, D_MODEL), F32),
        scratch_shapes=[pltpu.VMEM((tc * bsz, 2 * half), F32),
                        pltpu.VMEM((n_slab, bsz, 2 * half), F32),
                        pltpu.VMEM((tc * bsz, D_MODEL), F32)],
        compiler_params=_cparams(("arbitrary",)),
        name="s5",
    )(h_tm.reshape(seq, bsz, D_MODEL), gain.reshape(1, D_MODEL), bmat, cmat, are, aim,
      d_skip.reshape(1, D_MODEL), w_glu.astype(BF16))
    return out.reshape(seq, bsz * D_MODEL)


def _kv_body(h_ref, g_ref, w_ref, gk_ref, kc_ref, vc_ref, ks_ref, vs_ref, kw_ref, vw_ref):
    gw = N_KV_HEADS * HEAD_DIM
    kw = N_KV_HEADS * LANES
    hn = (_rms(h_ref[...], D_MODEL) * g_ref[...]).astype(BF16)
    kv = jnp.dot(hn, w_ref[...], preferred_element_type=F32)
    kc_ref[...] = kv[:, 0:gw].astype(BF16)
    vc_ref[...] = kv[:, gw:2 * gw].astype(BF16)
    vs_ref[...] = kv[:, 2 * gw:3 * gw].astype(BF16)
    vw_ref[...] = kv[:, 3 * gw:4 * gw].astype(BF16)
    for br, ref in ((0, ks_ref), (1, kw_ref)):
        for g in range(N_KV_HEADS):
            lo = 4 * gw + br * kw + g * LANES
            k = _rms(kv[:, lo:lo + LANES], HEAD_DIM) * gk_ref[br]
            ref[:, g * LANES:(g + 1) * LANES] = k.astype(BF16)


def _pad_heads(w, n_heads):
    k = w.shape[0]
    w = w.reshape(k, n_heads, HEAD_DIM)
    return jnp.pad(w, ((0, 0), (0, 0), (0, LANES - HEAD_DIM))).reshape(k, n_heads * LANES)


def _kv_proj(h, kv_norm, w_kv, k_norm_slc, k_norm_win, bsz, seq):
    gw = N_KV_HEADS * HEAD_DIM
    rows = min(KV_ROWS, seq)
    wk = w_kv.reshape(D_MODEL, 2 * N_BRANCH, gw)
    w = jnp.concatenate([wk[:, 0], wk[:, 1], wk[:, 3], wk[:, 5],
                         _pad_heads(wk[:, 2], N_KV_HEADS), _pad_heads(wk[:, 4], N_KV_HEADS)],
                        axis=1).astype(BF16)
    gk = jnp.pad(jnp.stack([k_norm_slc, k_norm_win]), ((0, 0), (0, LANES - HEAD_DIM)))
    gk = gk.reshape(2, 1, LANES)
    n = bsz * seq
    tok = lambda c: pl.BlockSpec((rows, c), lambda i: (i, 0))
    dense = jax.ShapeDtypeStruct((n, gw), BF16)
    padded = jax.ShapeDtypeStruct((n, N_KV_HEADS * LANES), BF16)
    return pl.pallas_call(
        _kv_body,
        grid=(n // rows,),
        in_specs=[tok(D_MODEL), _const_spec((1, D_MODEL)), _const_spec(w.shape),
                  _const_spec((2, 1, LANES))],
        out_specs=[tok(gw), tok(gw), tok(N_KV_HEADS * LANES), tok(gw),
                   tok(N_KV_HEADS * LANES), tok(gw)],
        out_shape=[dense, dense, padded, dense, padded, dense],
        compiler_params=_cparams(("parallel",)),
        name="kv_proj",
    )(h, kv_norm.reshape(1, D_MODEL), w, gk)


def _cmp_body(xk_ref, xv_ref, w1k_ref, w1v_ref, bk_ref, bv_ref, w2k_ref, w2v_ref, gk_ref,
              kc_ref, vc_ref, *, nh):
    for g in range(N_KV_HEADS):
        outs = []
        for x_ref, w1_ref, b_ref, w2_ref in ((xk_ref, w1k_ref, bk_ref, w2k_ref),
                                             (xv_ref, w1v_ref, bv_ref, w2v_ref)):
            ab = jnp.dot(x_ref[0, g], w1_ref[...], preferred_element_type=F32)
            hid = ab[:, :CMP_HIDDEN] + pltpu.roll(ab[:, CMP_HIDDEN:], nh - 1, 0) + b_ref[...]
            hid = jax.nn.gelu(hid).astype(BF16)
            outs.append(jnp.dot(hid, w2_ref[...], preferred_element_type=F32))
        k = _rms(outs[0], HEAD_DIM) * gk_ref[...]
        kc_ref[0, g] = k.T.astype(BF16)
        lane = lax.broadcasted_iota(jnp.int32, outs[1].shape, 1)
        vc_ref[0, g] = jnp.where(lane < HEAD_DIM, outs[1], 1.0).astype(BF16)


def _compress(kc_raw, vc_raw, k_norm_cmp, pos_k, pos_v, k_w1, k_w2, v_w1, v_w2, bsz, seq):
    nh = seq // CMP_STRIDE
    hb = CMP_STRIDE * HEAD_DIM

    def halfblocks(x):
        x = x.reshape(bsz, nh, CMP_STRIDE, N_KV_HEADS, HEAD_DIM)
        return x.transpose(0, 3, 1, 2, 4).reshape(bsz, N_KV_HEADS, nh, hb)

    def w1cat(w1):
        w = w1.reshape(2, hb, CMP_HIDDEN)
        return jnp.concatenate([w[0], w[1]], axis=1).astype(BF16)

    def w2pad(w2):
        return jnp.pad(w2, ((0, 0), (0, LANES - HEAD_DIM))).astype(BF16)

    bias = lambda pos, w1: jnp.einsum('ld,ldh->h', pos, w1,
                                      precision=lax.Precision.HIGHEST).reshape(1, CMP_HIDDEN)
    gk = jnp.pad(k_norm_cmp, (0, LANES - HEAD_DIM)).reshape(1, LANES)
    xspec = pl.BlockSpec((1, N_KV_HEADS, nh, hb), lambda b: (b, 0, 0, 0))
    ospec = pl.BlockSpec((1, N_KV_HEADS, LANES, nh), lambda b: (b, 0, 0, 0))
    vspec = pl.BlockSpec((1, N_KV_HEADS, nh, LANES), lambda b: (b, 0, 0, 0))
    return pl.pallas_call(
        functools.partial(_cmp_body, nh=nh),
        grid=(bsz,),
        in_specs=[xspec, xspec, _const_spec((hb, 2 * CMP_HIDDEN)), _const_spec((hb, 2 * CMP_HIDDEN)),
                  _const_spec((1, CMP_HIDDEN)), _const_spec((1, CMP_HIDDEN)),
                  _const_spec((CMP_HIDDEN, LANES)), _const_spec((CMP_HIDDEN, LANES)),
                  _const_spec((1, LANES))],
        out_specs=[ospec, vspec],
        out_shape=[jax.ShapeDtypeStruct((bsz, N_KV_HEADS, LANES, nh), BF16),
                   jax.ShapeDtypeStruct((bsz, N_KV_HEADS, nh, LANES), BF16)],
        compiler_params=_cparams(("parallel",)),
        name="kv_compress",
    )(halfblocks(kc_raw), halfblocks(vc_raw), w1cat(k_w1), w1cat(v_w1),
      bias(pos_k, k_w1), bias(pos_v, v_w1), w2pad(k_w2), w2pad(v_w2), gk)


def _split3_dot(x, w):
    x1 = x.astype(BF16)
    r1 = x - x1.astype(F32)
    x2 = r1.astype(BF16)
    x3 = (r1 - x2.astype(F32)).astype(BF16)
    d = lambda a: jnp.dot(a, w, preferred_element_type=F32)
    return d(x1) + d(x2) + d(x3)


def _nsa_body(h_ref, gmix_ref, wqg_ref, qsc_ref, wo_ref, kc_ref, vc_ref, ks_ref, vs_ref,
              kw_ref, vw_ref, ctab_ref, wtab_ref, cfar_ref, ovl_ref,
              o_ref, s_scr, m_scr, o_scr, *, n_sel, n_top):
    i = pl.program_id(1)
    rq = Q_PER_KV * QT
    h = h_ref[...]
    u = (_rms(h, D_MODEL) * gmix_ref[...]).astype(BF16)
    qg = jnp.dot(u, wqg_ref[...], preferred_element_type=F32)
    gates = jax.nn.sigmoid(qg[:, N_HEADS * LANES:])
    lane = lax.broadcasted_iota(jnp.int32, (QT, LANES), 1)
    qpos = i * QT + lax.broadcasted_iota(jnp.int32, (QT, LANES), 0)
    low = lane < HEAD_DIM
    jj = lane - SEL_LANE0
    sel_lane = (jj >= 0) & (jj < n_sel)
    jjf = jj.astype(F32)
    cur = lax.shift_right_arithmetic(qpos, int(math.log2(SEL_LEN)))
    forced = (jj == 0) | (jj == cur) | (jj == cur - 1)
    causal = jj * SEL_LEN <= qpos
    any_valid = jnp.concatenate([(qpos[:, :1] >= CMP_LEN - 1).astype(F32)] * Q_PER_KV, axis=0)

    for g in range(N_KV_HEADS):
        heads = [Q_PER_KV * g + r for r in range(Q_PER_KV)]
        q_h = [_rms(qg[:, hh * LANES:(hh + 1) * LANES], HEAD_DIM) * qsc_ref[:, hh * LANES:(hh + 1) * LANES]
               for hh in heads]
        q = jnp.concatenate(q_h, axis=0)
        qb = q.astype(BF16)
        tab = lambda lo: wtab_ref[Q_PER_KV * g:Q_PER_KV * (g + 1), :, lo:lo + QT].reshape(rq, QT)
        cfar = cfar_ref[g]

        sc = jnp.dot(qb, kc_ref[0, g], preferred_element_type=F32)
        sc = sc + ctab_ref[Q_PER_KV * g:Q_PER_KV * (g + 1)].reshape(rq, sc.shape[-1])
        e = jnp.exp(sc - jnp.max(sc, axis=-1, keepdims=True))
        p = e / jnp.sum(e, axis=-1, keepdims=True) * any_valid
        pv_c = jnp.dot(p.astype(BF16), vc_ref[0, g], preferred_element_type=F32)
        psum = p[0:QT] + p[QT:2 * QT] + p[2 * QT:3 * QT] + p[3 * QT:4 * QT]
        pslc = _split3_dot(psum, ovl_ref[...])

        score = jnp.where(forced, BIG, jnp.where(causal, pslc, -BIG))
        score = jnp.where(sel_lane, score, -jnp.inf)
        picked = jnp.zeros((QT, LANES), F32)
        for _ in range(n_top):
            mx = jnp.max(score, axis=-1, keepdims=True)
            first = jnp.min(jnp.where(score == mx, jjf, float(LANES)), axis=-1, keepdims=True)
            hit = jjf == first
            picked = jnp.where(hit, 1.0, picked)
            score = jnp.where(hit, -jnp.inf, score)
        selb = jnp.where(sel_lane & (picked == 0.0), NEG_INF, 0.0)
        qsb = (q + jnp.concatenate([selb] * Q_PER_KV, axis=0)).astype(BF16)

        pen0 = jnp.where(i == 0, NEG_INF, 0.0).astype(F32)
        s_prev = jnp.dot(qsb, ks_ref[0, g, i], preferred_element_type=F32) + tab(QT) + pen0
        s_cur = jnp.dot(qsb, ks_ref[0, g, i + 1], preferred_element_type=F32) + tab(2 * QT)
        m_scr[...] = jnp.maximum(s_prev, s_cur)
        n_far = jnp.maximum(i - 1, 0)

        def far_logits(t, c, g=g, qsb=qsb, cfar=cfar):
            s = jnp.dot(qsb, ks_ref[0, g, t + 1], preferred_element_type=F32) + cfar
            s_scr[t] = s
            m_scr[...] = jnp.maximum(m_scr[...], s)
            return c

        lax.fori_loop(0, n_far, far_logits, 0)
        m = jnp.max(m_scr[...], axis=-1, keepdims=True)
        pv_s = jnp.dot(jnp.exp(s_prev - m).astype(BF16), vs_ref[0, g, i], preferred_element_type=F32)
        pv_s = pv_s + jnp.dot(jnp.exp(s_cur - m).astype(BF16), vs_ref[0, g, i + 1],
                              preferred_element_type=F32)

        def far_pv(t, acc, g=g, m=m):
            pt = jnp.exp(s_scr[t] - m).astype(BF16)
            return acc + jnp.dot(pt, vs_ref[0, g, t + 1], preferred_element_type=F32)

        pv_s = lax.fori_loop(0, n_far, far_pv, pv_s)

        s_w = []
        for j in range(WIN // QT + 1):
            s = jnp.dot(qb, kw_ref[0, g, i + j], preferred_element_type=F32)
            if j == 0:
                s = s + tab(0)
            elif j == WIN // QT - 1:
                s = s + tab(QT)
            elif j == WIN // QT:
                s = s + tab(2 * QT)
            else:
                s = s + cfar
            if j < WIN // QT:
                s = s + jnp.where(i + j < WIN // QT, NEG_INF, 0.0).astype(F32)
            s_w.append(s)
        mw = s_w[0]
        for s in s_w[1:]:
            mw = jnp.maximum(mw, s)
        mw = jnp.max(mw, axis=-1, keepdims=True)
        pv_w = None
        for j, s in enumerate(s_w):
            d = jnp.dot(jnp.exp(s - mw).astype(BF16), vw_ref[0, g, i + j], preferred_element_type=F32)
            pv_w = d if pv_w is None else pv_w + d

        o_s = pv_s * (1.0 / pltpu.roll(pv_s, HEAD_DIM, 1))
        o_w = pv_w * (1.0 / pltpu.roll(pv_w, HEAD_DIM, 1))
        outs = []
        for r in range(Q_PER_KV):
            c0 = (Q_PER_KV * g + r) * N_BRANCH
            gate = lambda b: jnp.broadcast_to(gates[:, c0 + b:c0 + b + 1], (QT, LANES))
            rs = slice(r * QT, (r + 1) * QT)
            o = gate(0) * pv_c[rs] + gate(1) * o_s[rs] + gate(2) * o_w[rs]
            outs.append(jnp.where(low, o, 0.0))
        for pr in range(Q_PER_KV // 2):
            col = (Q_PER_KV * g // 2 + pr) * LANES
            o_scr[:, col:col + LANES] = outs[2 * pr] + pltpu.roll(outs[2 * pr + 1], HEAD_DIM, 1)

    y = jnp.dot(o_scr[...].astype(BF16), wo_ref[...], preferred_element_type=F32)
    o_ref[...] = h + y


def _rel_bucket(dist):
    n = jnp.maximum(dist, 0)
    max_exact = NUM_BUCKETS // 2
    logv = jnp.log(jnp.maximum(n, 1).astype(F32) / max_exact) / math.log(MAX_DISTANCE / max_exact)
    large = jnp.minimum(max_exact + (logv * (NUM_BUCKETS - max_exact)).astype(jnp.int32), NUM_BUCKETS - 1)
    return jnp.where(n < max_exact, n, large)


def _bias_tables(rel_bias, seq):
    nc = seq // CMP_STRIDE
    n_cmp = (seq - CMP_LEN) // CMP_STRIDE + 1
    f = lambda dist: rel_bias[_rel_bucket(dist)].transpose(2, 0, 1)
    dc = jnp.arange(seq)[:, None] - (jnp.arange(nc) * CMP_STRIDE + CMP_LEN - 1)[None, :]
    ok = (dc >= 0) & (jnp.arange(nc) < n_cmp)[None, :]
    ctab = jnp.where(ok[None], f(dc), NEG_INF)
    dq = jnp.arange(QT)[:, None] - jnp.arange(QT)[None, :]
    d0, d3, d4 = WIN + dq, QT + dq, dq
    wtab = jnp.concatenate([jnp.where((d0 < WIN)[None], f(d0), NEG_INF), f(d3),
                            jnp.where((d4 >= 0)[None], f(d4), NEG_INF)], axis=2)
    far = rel_bias[NUM_BUCKETS - 1]
    cfar = jnp.broadcast_to(far.reshape(N_KV_HEADS, Q_PER_KV, 1, 1),
                            (N_KV_HEADS, Q_PER_KV, QT, QT)).reshape(N_KV_HEADS, Q_PER_KV * QT, QT)
    return ctab, wtab, cfar


def _overlap_matrix(seq):
    nc = seq // CMP_STRIDE
    n_sel = seq // SEL_LEN
    c_start = jnp.arange(nc) * CMP_STRIDE
    j_start = jnp.arange(n_sel) * SEL_LEN
    ov = jnp.clip(jnp.minimum(c_start[:, None] + CMP_LEN, j_start[None, :] + SEL_LEN)
                  - jnp.maximum(c_start[:, None], j_start[None, :]), 0, None)
    ov = ov.astype(F32) / CMP_LEN
    return jnp.pad(ov, ((0, 0), (SEL_LANE0, LANES - SEL_LANE0 - n_sel))).astype(BF16)


def _key_tiles(k, bsz, seq, pad_front, with_sel):
    nt = seq // QT
    k = k.reshape(bsz, nt, QT, N_KV_HEADS, LANES).transpose(0, 3, 1, 4, 2)
    if with_sel:
        key = jnp.arange(seq).reshape(nt, 1, QT)
        row = jnp.arange(LANES).reshape(1, LANES, 1)
        k = k + (row - SEL_LANE0 == key // SEL_LEN).astype(BF16)[None, None]
    return jnp.pad(k, ((0, 0), (0, 0), (pad_front, 0), (0, 0), (0, 0)))


def _value_tiles(v, bsz, seq, pad_front):
    nt = seq // QT
    v = v.reshape(bsz, nt, QT, N_KV_HEADS, HEAD_DIM).transpose(0, 3, 1, 2, 4)
    v = jnp.concatenate([v, jnp.ones_like(v)], axis=-1)
    return jnp.pad(v, ((0, 0), (0, 0), (pad_front, 0), (0, 0), (0, 0)))


def _nsa(h, kv, mix_gain, w_qg, q_norm, w_o, rel_bias, bsz, seq):
    kc, vc, ks, vs, kw, vw = kv
    nt = seq // QT
    nc = seq // CMP_STRIDE
    n_sel = seq // SEL_LEN
    n_top = min(SEL_TOPN, n_sel)
    nq = N_HEADS * HEAD_DIM
    wq = jnp.concatenate([_pad_heads(w_qg[:, :nq], N_HEADS),
                          jnp.pad(w_qg[:, nq:], ((0, 0), (0, LANES - N_BRANCH * N_HEADS)))],
                         axis=1).astype(BF16)
    qsc = jnp.tile(jnp.pad(q_norm * ATTN_SCALE, (0, LANES - HEAD_DIM)), N_HEADS).reshape(1, N_HEADS * LANES)
    ctab, wtab, cfar = _bias_tables(rel_bias, seq)
    ovl = _overlap_matrix(seq)
    wwin = WIN // QT
    bspec = lambda shape: pl.BlockSpec((1,) + shape, lambda b, i: (b,) + (0,) * len(shape))
    rq = Q_PER_KV * QT
    return pl.pallas_call(
        functools.partial(_nsa_body, n_sel=n_sel, n_top=n_top),
        grid=(bsz, nt),
        in_specs=[pl.BlockSpec((QT, D_MODEL), lambda b, i: (b * nt + i, 0)),
                  _const_spec((1, D_MODEL)),
                  _const_spec(wq.shape),
                  _const_spec((1, N_HEADS * LANES)),
                  _const_spec((D_MODEL, D_MODEL)),
                  bspec((N_KV_HEADS, LANES, nc)),
                  bspec((N_KV_HEADS, nc, LANES)),
                  bspec((N_KV_HEADS, nt + 1, LANES, QT)),
                  bspec((N_KV_HEADS, nt + 1, QT, LANES)),
                  bspec((N_KV_HEADS, nt + wwin, LANES, QT)),
                  bspec((N_KV_HEADS, nt + wwin, QT, LANES)),
                  pl.BlockSpec((N_HEADS, QT, nc), lambda b, i: (0, i, 0)),
                  _const_spec((N_HEADS, QT, 3 * QT)),
                  _const_spec((N_KV_HEADS, rq, QT)),
                  _const_spec((nc, LANES))],
        out_specs=pl.BlockSpec((QT, D_MODEL), lambda b, i: (b * nt + i, 0)),
        out_shape=jax.ShapeDtypeStruct((bsz * seq, D_MODEL), F32),
        scratch_shapes=[pltpu.VMEM((nt, rq, QT), F32),
                        pltpu.VMEM((rq, QT), F32),
                        pltpu.VMEM((QT, D_MODEL), F32)],
        compiler_params=_cparams(("parallel", "arbitrary")),
        name="nsa",
    )(h, mix_gain.reshape(1, D_MODEL), wq, qsc, w_o.astype(BF16),
      kc, vc,
      _key_tiles(ks, bsz, seq, 1, True), _value_tiles(vs, bsz, seq, 1),
      _key_tiles(kw, bsz, seq, wwin, False), _value_tiles(vw, bsz, seq, wwin),
      ctab, wtab, cfar, ovl)


def kernel(x, rel_bias, ffn1_norm, ffn1_w_in, ffn1_w_out, mix_norm, ffn2_norm, ffn2_w_in, ffn2_w_out,
           s5_a_re, s5_a_im, s5_log_dt, s5_b_re, s5_b_im, s5_c_re, s5_c_im, s5_d, s5_w_glu,
           kv_norm, w_kv, k_norm_cmp, k_norm_slc, k_norm_win, cmp_pos_k, cmp_pos_v,
           cmp_k_w1, cmp_k_w2, cmp_v_w1, cmp_v_w2, w_qg, q_norm, w_o):
    bsz, seq, _ = x.shape
    depth = ffn1_norm.shape[0]
    n_a = s5_a_re.shape[0]
    h = x.reshape(bsz * seq, D_MODEL)
    kv = None
    for layer in range(depth):
        s5_layer = layer < n_a
        h = _ffn(h, ffn1_norm[layer], ffn1_w_in[layer], ffn1_w_out[layer], bsz, seq,
                 in_tm=False, out_tm=s5_layer)
        if s5_layer:
            a = layer
            h = _s5(h, mix_norm[layer], s5_a_re[a], s5_a_im[a], s5_log_dt[a], s5_b_re[a], s5_b_im[a],
                    s5_c_re[a], s5_c_im[a], s5_d[a], s5_w_glu[a], bsz, seq)
        else:
            b = layer - n_a
            h = _nsa(h, kv, mix_norm[layer], w_qg[b], q_norm[b], w_o[b], rel_bias, bsz, seq)
        h = _ffn(h, ffn2_norm[layer], ffn2_w_in[layer], ffn2_w_out[layer], bsz, seq,
                 in_tm=s5_layer, out_tm=False)
        if layer == n_a - 1:
            kc_raw, vc_raw, ks, vs, kw, vw = _kv_proj(h, kv_norm, w_kv, k_norm_slc, k_norm_win, bsz, seq)
            kc, vc = _compress(kc_raw, vc_raw, k_norm_cmp, cmp_pos_k, cmp_pos_v,
                               cmp_k_w1, cmp_k_w2, cmp_v_w1, cmp_v_w2, bsz, seq)
            kv = (kc, vc, ks, vs, kw, vw)
    return h.reshape(bsz, seq, D_MODEL)
```

```python
import functools
import math

import jax
import jax.numpy as jnp
from jax import lax
from jax.experimental import pallas as pl
from jax.experimental.pallas import tpu as pltpu

F32 = jnp.float32
BF16 = jnp.bfloat16

D_MODEL = 1024
D_FF = 2816
RMS_EPS = 1e-6
S5_GROUP_CH = 16
S5_GROUPS = D_MODEL // S5_GROUP_CH
S5_STATE = 64
N_HEADS = 16
HEAD_DIM = 64
N_KV_HEADS = 4
Q_PER_KV = N_HEADS // N_KV_HEADS
CMP_LEN = 32
CMP_STRIDE = 16
CMP_HIDDEN = 2 * HEAD_DIM
SEL_LEN = 64
SEL_TOPN = 8
WIN = 512
N_BRANCH = 3
ATTN_SCALE = HEAD_DIM ** -0.5
NUM_BUCKETS = 32
MAX_DISTANCE = 128
NEG_INF = -1e30
BIG = 1e9

LANES = 128
MXU_DIM = 256
VMEM_LIMIT = 56 * 1024 * 1024

FFN_ROWS = 512
FF_CHUNK = MXU_DIM
S5_TC = 16
S5_SLAB = MXU_DIM
S5_LANES = 256
KV_ROWS = 512
QP_ROWS = 512
QT = 128
FAR_TILES = 4

SEL_LANE0 = HEAD_DIM
FARB_LANE0 = 96
PAD_LANE = 99


def _rms(x, n):
    ss = jnp.sum(x * x, axis=-1, keepdims=True)
    return x * lax.rsqrt(ss * (1.0 / n) + RMS_EPS)


def _cparams(sem):
    return pltpu.CompilerParams(dimension_semantics=sem, vmem_limit_bytes=VMEM_LIMIT)


def _const_spec(shape):
    nd = len(shape)
    return pl.BlockSpec(shape, lambda *_: (0,) * nd, pipeline_mode=pl.Buffered(1))


def _split3(x):
    x1 = x.astype(BF16)
    r1 = x - x1.astype(F32)
    x2 = r1.astype(BF16)
    x3 = (r1 - x2.astype(F32)).astype(BF16)
    return x1, x2, x3


def _ffn_body(*refs, pre):
    if pre:
        x_ref, po_ref, pw_ref, g_ref, win_ref, wout_ref, o_ref, act_ref = refs
        x = x_ref[...] + jnp.dot(po_ref[...], pw_ref[...], preferred_element_type=F32)
    else:
        x_ref, g_ref, win_ref, wout_ref, o_ref, act_ref = refs
        x = x_ref[...]
    xn = (_rms(x, D_MODEL) * g_ref[...]).astype(BF16)
    for c in range(D_FF // FF_CHUNK):
        lo = c * FF_CHUNK
        a = jnp.dot(xn, win_ref[:, lo:lo + FF_CHUNK], preferred_element_type=F32)
        b = jnp.dot(xn, win_ref[:, D_FF + lo:D_FF + lo + FF_CHUNK], preferred_element_type=F32)
        act_ref[:, lo:lo + FF_CHUNK] = (a * jax.nn.sigmoid(a) * b).astype(BF16)
    y = jnp.dot(act_ref[...], wout_ref[...], preferred_element_type=F32)
    o_ref[...] = x + 0.5 * y


def _token_spec(rows, seq, time_major):
    nt = seq // rows
    if time_major:
        return pl.BlockSpec((rows, D_MODEL), lambda b, t: (t, b))
    return pl.BlockSpec((rows, D_MODEL), lambda b, t: (b * nt + t, 0))


def _ffn(h, gain, w_in, w_out, bsz, seq, in_tm, out_tm, pre=None):
    rows = min(FFN_ROWS, seq)
    out_shape = (seq, bsz * D_MODEL) if out_tm else (bsz * seq, D_MODEL)
    args, specs = [h], [_token_spec(rows, seq, in_tm)]
    if pre is not None:
        args += [pre[0], pre[1].astype(BF16)]
        specs += [_token_spec(rows, seq, False), _const_spec((D_MODEL, D_MODEL))]
    args += [gain.reshape(1, D_MODEL), w_in.astype(BF16), w_out.astype(BF16)]
    specs += [_const_spec((1, D_MODEL)), _const_spec((D_MODEL, 2 * D_FF)), _const_spec((D_FF, D_MODEL))]
    return pl.pallas_call(
        functools.partial(_ffn_body, pre=pre is not None),
        grid=(bsz, seq // rows),
        in_specs=specs,
        out_specs=_token_spec(rows, seq, out_tm),
        out_shape=jax.ShapeDtypeStruct(out_shape, F32),
        scratch_shapes=[pltpu.VMEM((rows, D_FF), BF16)],
        compiler_params=_cparams(("parallel", "parallel")),
        name="ffn",
    )(*args)


def _s5_body(h_ref, g_ref, bmat_ref, cmat_ref, are_ref, aim_ref, d_ref, wglu_ref,
             o_ref, bu_ref, st_ref, y_ref, *, tc, nb):
    n_slab = D_MODEL // S5_SLAB
    half = S5_SLAB // S5_GROUP_CH * S5_STATE
    rows = tc * nb

    @pl.when(pl.program_id(0) == 0)
    def _():
        st_ref[...] = jnp.zeros_like(st_ref)

    h = h_ref[...].reshape(rows, D_MODEL)
    u = _rms(h, D_MODEL) * g_ref[...]
    ub = u.astype(BF16)
    for sl in range(n_slab):
        bu_ref[...] = jnp.dot(ub[:, sl * S5_SLAB:(sl + 1) * S5_SLAB], bmat_ref[sl],
                              preferred_element_type=F32)
        for wb in range(half // S5_LANES):
            lo = wb * S5_LANES
            ar = jnp.broadcast_to(are_ref[sl, :, lo:lo + S5_LANES], (nb, S5_LANES))
            ai = jnp.broadcast_to(aim_ref[sl, :, lo:lo + S5_LANES], (nb, S5_LANES))

            def step(t, carry, lo=lo, ar=ar, ai=ai):
                xr, xi = carry
                r0 = pl.multiple_of(t * nb, nb)
                br = bu_ref[pl.ds(r0, nb), lo:lo + S5_LANES]
                bi = bu_ref[pl.ds(r0, nb), half + lo:half + lo + S5_LANES]
                nr = ar * xr - ai * xi + br
                ni = ar * xi + ai * xr + bi
                bu_ref[pl.ds(r0, nb), lo:lo + S5_LANES] = nr
                bu_ref[pl.ds(r0, nb), half + lo:half + lo + S5_LANES] = ni
                return nr, ni

            xr, xi = lax.fori_loop(
                0, tc, step,
                (st_ref[sl, :, lo:lo + S5_LANES], st_ref[sl, :, half + lo:half + lo + S5_LANES]))
            st_ref[sl, :, lo:lo + S5_LANES] = xr
            st_ref[sl, :, half + lo:half + lo + S5_LANES] = xi
        y_ref[:, sl * S5_SLAB:(sl + 1) * S5_SLAB] = jnp.dot(
            bu_ref[...].astype(BF16), cmat_ref[sl], preferred_element_type=F32)
    y = jax.nn.gelu(y_ref[...] + d_ref[...] * u)
    gate = jnp.dot(y.astype(BF16), wglu_ref[...], preferred_element_type=F32)
    o_ref[...] = (h + y * jax.nn.sigmoid(gate)).reshape(tc, nb, D_MODEL)


def _s5_params(a_re, a_im, log_dt, b_re, b_im, c_re, c_im):
    dt = jnp.exp(log_dt)[:, None]
    mag = jnp.exp(a_re * dt)
    ab_re = mag * jnp.cos(a_im * dt)
    ab_im = mag * jnp.sin(a_im * dt)
    den = a_re * a_re + a_im * a_im
    z_re = ((ab_re - 1.0) * a_re + ab_im * a_im) / den
    z_im = (ab_im * a_re - (ab_re - 1.0) * a_im) / den
    bb_re = z_re[..., None] * b_re - z_im[..., None] * b_im
    bb_im = z_re[..., None] * b_im + z_im[..., None] * b_re
    n_slab = D_MODEL // S5_SLAB
    gps = S5_GROUPS // n_slab
    eye = jnp.eye(gps, dtype=F32)

    def in_mat(bb):
        bb = bb.reshape(n_slab, gps, S5_STATE, S5_GROUP_CH)
        m = jnp.einsum('sgph,gk->sghkp', bb, eye)
        return m.reshape(n_slab, gps * S5_GROUP_CH, gps * S5_STATE)

    def out_mat(cc):
        cc = cc.reshape(n_slab, gps, S5_GROUP_CH, S5_STATE)
        m = jnp.einsum('sghp,gk->sgpkh', cc, eye)
        return m.reshape(n_slab, gps * S5_STATE, gps * S5_GROUP_CH)

    bmat = jnp.concatenate([in_mat(bb_re), in_mat(bb_im)], axis=2).astype(BF16)
    cmat = jnp.concatenate([out_mat(c_re), out_mat(-c_im)], axis=1).astype(BF16)
    are = ab_re.reshape(n_slab, 1, gps * S5_STATE)
    aim = ab_im.reshape(n_slab, 1, gps * S5_STATE)
    return bmat, cmat, are, aim


def _s5(h_tm, gain, a_re, a_im, log_dt, b_re, b_im, c_re, c_im, d_skip, w_glu, bsz, seq):
    tc = min(S5_TC, seq)
    n_slab = D_MODEL // S5_SLAB
    half = S5_SLAB // S5_GROUP_CH * S5_STATE
    bmat, cmat, are, aim = _s5_params(a_re, a_im, log_dt, b_re, b_im, c_re, c_im)
    blk = pl.BlockSpec((tc, bsz, D_MODEL), lambda i: (i, 0, 0))
    out = pl.pallas_call(
        functools.partial(_s5_body, tc=tc, nb=bsz),
        grid=(seq // tc,),
        in_specs=[blk,
                  _const_spec((1, D_MODEL)),
                  _const_spec((n_slab, S5_SLAB, 2 * half)),
                  _const_spec((n_slab, 2 * half, S5_SLAB)),
                  _const_spec((n_slab, 1, half)),
                  _const_spec((n_slab, 1, half)),
                  _const_spec((1, D_MODEL)),
                  _const_spec((D_MODEL, D_MODEL))],
        out_specs=blk,
        out_shape=jax.ShapeDtypeStruct((seq, bsz, D_MODEL), F32),
        scratch_shapes=[pltpu.VMEM((tc * bsz, 2 * half), F32),
                        pltpu.VMEM((n_slab, bsz, 2 * half), F32),
                        pltpu.VMEM((tc * bsz, D_MODEL), F32)],
        compiler_params=_cparams(("arbitrary",)),
        name="s5",
    )(h_tm.reshape(seq, bsz, D_MODEL), gain.reshape(1, D_MODEL), bmat, cmat, are, aim,
      d_skip.reshape(1, D_MODEL), w_glu.astype(BF16))
    return out.reshape(seq, bsz * D_MODEL)


def _kv_body(h_ref, g_ref, w_ref, gk_ref, kc_ref, vc_ref, ks_ref, vs_ref, kw_ref, vw_ref):
    gw = N_KV_HEADS * HEAD_DIM
    kw = N_KV_HEADS * LANES
    hn = (_rms(h_ref[...], D_MODEL) * g_ref[...]).astype(BF16)
    kv = jnp.dot(hn, w_ref[...], preferred_element_type=F32)
    kc_ref[...] = kv[:, 0:gw].astype(BF16)
    vc_ref[...] = kv[:, gw:2 * gw].astype(BF16)
    vs_ref[...] = kv[:, 2 * gw:3 * gw].astype(BF16)
    vw_ref[...] = kv[:, 3 * gw:4 * gw].astype(BF16)
    for br, ref in ((0, ks_ref), (1, kw_ref)):
        for g in range(N_KV_HEADS):
            lo = 4 * gw + br * kw + g * LANES
            k = _rms(kv[:, lo:lo + LANES], HEAD_DIM) * gk_ref[br]
            ref[:, g * LANES:(g + 1) * LANES] = k.astype(BF16)


def _pad_heads(w, n_heads):
    k = w.shape[0]
    w = w.reshape(k, n_heads, HEAD_DIM)
    return jnp.pad(w, ((0, 0), (0, 0), (0, LANES - HEAD_DIM))).reshape(k, n_heads * LANES)


def _kv_proj(h, kv_norm, w_kv, k_norm_slc, k_norm_win, bsz, seq):
    gw = N_KV_HEADS * HEAD_DIM
    rows = min(KV_ROWS, seq)
    wk = w_kv.reshape(D_MODEL, 2 * N_BRANCH, gw)
    w = jnp.concatenate([wk[:, 0], wk[:, 1], wk[:, 3], wk[:, 5],
                         _pad_heads(wk[:, 2], N_KV_HEADS), _pad_heads(wk[:, 4], N_KV_HEADS)],
                        axis=1).astype(BF16)
    gk = jnp.pad(jnp.stack([k_norm_slc, k_norm_win]), ((0, 0), (0, LANES - HEAD_DIM)))
    gk = gk.reshape(2, 1, LANES)
    n = bsz * seq
    tok = lambda c: pl.BlockSpec((rows, c), lambda i: (i, 0))
    dense = jax.ShapeDtypeStruct((n, gw), BF16)
    padded = jax.ShapeDtypeStruct((n, N_KV_HEADS * LANES), BF16)
    return pl.pallas_call(
        _kv_body,
        grid=(n // rows,),
        in_specs=[tok(D_MODEL), _const_spec((1, D_MODEL)), _const_spec(w.shape),
                  _const_spec((2, 1, LANES))],
        out_specs=[tok(gw), tok(gw), tok(N_KV_HEADS * LANES), tok(gw),
                   tok(N_KV_HEADS * LANES), tok(gw)],
        out_shape=[dense, dense, padded, dense, padded, dense],
        compiler_params=_cparams(("parallel",)),
        name="kv_proj",
    )(h, kv_norm.reshape(1, D_MODEL), w, gk)


def _cmp_body(xk_ref, xv_ref, w1k_ref, w1v_ref, bk_ref, bv_ref, w2k_ref, w2v_ref, gk_ref,
              kc_ref, vc_ref, *, nh):
    for g in range(N_KV_HEADS):
        outs = []
        for x_ref, w1_ref, b_ref, w2_ref in ((xk_ref, w1k_ref, bk_ref, w2k_ref),
                                             (xv_ref, w1v_ref, bv_ref, w2v_ref)):
            ab = jnp.dot(x_ref[0, g], w1_ref[...], preferred_element_type=F32)
            hid = ab[:, :CMP_HIDDEN] + pltpu.roll(ab[:, CMP_HIDDEN:], nh - 1, 0) + b_ref[...]
            hid = jax.nn.gelu(hid).astype(BF16)
            outs.append(jnp.dot(hid, w2_ref[...], preferred_element_type=F32))
        k = _rms(outs[0], HEAD_DIM) * gk_ref[...]
        kc_ref[0, g] = k.T.astype(BF16)
        vc_ref[0, g] = outs[1].astype(BF16)


def _compress(kc_raw, vc_raw, k_norm_cmp, pos_k, pos_v, k_w1, k_w2, v_w1, v_w2, bsz, seq):
    nh = seq // CMP_STRIDE
    hb = CMP_STRIDE * HEAD_DIM

    def halfblocks(x):
        x = x.reshape(bsz, nh, CMP_STRIDE, N_KV_HEADS, HEAD_DIM)
        return x.transpose(0, 3, 1, 2, 4).reshape(bsz, N_KV_HEADS, nh, hb)

    def w1cat(w1):
        w = w1.reshape(2, hb, CMP_HIDDEN)
        return jnp.concatenate([w[0], w[1]], axis=1).astype(BF16)

    def w2pad(w2):
        return jnp.pad(w2, ((0, 0), (0, LANES - HEAD_DIM))).astype(BF16)

    bias = lambda pos, w1: jnp.einsum('ld,ldh->h', pos, w1,
                                      precision=lax.Precision.HIGHEST).reshape(1, CMP_HIDDEN)
    gk = jnp.pad(k_norm_cmp, (0, LANES - HEAD_DIM)).reshape(1, LANES)
    xspec = pl.BlockSpec((1, N_KV_HEADS, nh, hb), lambda b: (b, 0, 0, 0))
    ospec = pl.BlockSpec((1, N_KV_HEADS, LANES, nh), lambda b: (b, 0, 0, 0))
    vspec = pl.BlockSpec((1, N_KV_HEADS, nh, LANES), lambda b: (b, 0, 0, 0))
    return pl.pallas_call(
        functools.partial(_cmp_body, nh=nh),
        grid=(bsz,),
        in_specs=[xspec, xspec, _const_spec((hb, 2 * CMP_HIDDEN)), _const_spec((hb, 2 * CMP_HIDDEN)),
                  _const_spec((1, CMP_HIDDEN)), _const_spec((1, CMP_HIDDEN)),
                  _const_spec((CMP_HIDDEN, LANES)), _const_spec((CMP_HIDDEN, LANES)),
                  _const_spec((1, LANES))],
        out_specs=[ospec, vspec],
        out_shape=[jax.ShapeDtypeStruct((bsz, N_KV_HEADS, LANES, nh), BF16),
                   jax.ShapeDtypeStruct((bsz, N_KV_HEADS, nh, LANES), BF16)],
        compiler_params=_cparams(("parallel",)),
        name="kv_compress",
    )(halfblocks(kc_raw), halfblocks(vc_raw), w1cat(k_w1), w1cat(v_w1),
      bias(pos_k, k_w1), bias(pos_v, v_w1), w2pad(k_w2), w2pad(v_w2), gk)


def _qproj_body(h_ref, g_ref, w_ref, qsc_ref, qc_ref, q_ref, gate_ref):
    u = (_rms(h_ref[...], D_MODEL) * g_ref[...]).astype(BF16)
    qg = jnp.dot(u, w_ref[...], preferred_element_type=F32)
    for hh in range(N_HEADS):
        sl = slice(hh * LANES, (hh + 1) * LANES)
        q_ref[:, sl] = (_rms(qg[:, sl], HEAD_DIM) * qsc_ref[:, sl] + qc_ref[:, sl]).astype(BF16)
    gate_ref[...] = jax.nn.sigmoid(qg[:, N_HEADS * LANES:])


def _qproj(h, mix_gain, w_qg, q_norm, rel_bias, bsz, seq):
    rows = min(QP_ROWS, seq)
    n = bsz * seq
    nq = N_HEADS * HEAD_DIM
    wq = jnp.concatenate([_pad_heads(w_qg[:, :nq], N_HEADS),
                          jnp.pad(w_qg[:, nq:], ((0, 0), (0, LANES - N_BRANCH * N_HEADS)))],
                         axis=1).astype(BF16)
    qsc = jnp.tile(jnp.pad(q_norm * ATTN_SCALE, (0, LANES - HEAD_DIM)), N_HEADS).reshape(1, N_HEADS * LANES)
    far = _split3(rel_bias[NUM_BUCKETS - 1])
    qc = jnp.zeros((N_HEADS, LANES), F32)
    for t, term in enumerate(far):
        qc = qc.at[:, FARB_LANE0 + t].set(term.astype(F32))
    qc = qc.at[:, PAD_LANE].set(NEG_INF).reshape(1, N_HEADS * LANES)
    tok = lambda c: pl.BlockSpec((rows, c), lambda i: (i, 0))
    return pl.pallas_call(
        _qproj_body,
        grid=(n // rows,),
        in_specs=[tok(D_MODEL), _const_spec((1, D_MODEL)), _const_spec(wq.shape),
                  _const_spec((1, N_HEADS * LANES)), _const_spec((1, N_HEADS * LANES))],
        out_specs=[tok(N_HEADS * LANES), tok(LANES)],
        out_shape=[jax.ShapeDtypeStruct((n, N_HEADS * LANES), BF16),
                   jax.ShapeDtypeStruct((n, LANES), F32)],
        compiler_params=_cparams(("parallel",)),
        name="q_proj",
    )(h, mix_gain.reshape(1, D_MODEL), wq, qsc, qc)


def _attn_body(q_ref, gate_ref, kc_ref, vc_ref, ks_ref, vs_ref, kw_ref, vw_ref,
               ctab_ref, wtab_ref, ovlt_ref,
               o_ref, s_scr, sn_scr, m_scr, qf_scr, acc_scr, out_scr, *, n_sel, n_top):
    i = pl.program_id(1)
    rq = Q_PER_KV * QT
    n_win = WIN // QT + 1
    gates = gate_ref[...]
    lane = lax.broadcasted_iota(jnp.int32, (QT, LANES), 1)
    low = lane < HEAD_DIM
    sel_lane = (lane >= SEL_LANE0) & (lane < SEL_LANE0 + n_sel)
    far_cut = sel_lane & (lane - SEL_LANE0 >= 2 * (i - 1))
    sel_lane4 = jnp.concatenate([sel_lane] * Q_PER_KV, axis=0)
    row_pos = i * QT + lax.broadcasted_iota(jnp.int32, (QT, 1), 0)
    any_valid = jnp.concatenate([(row_pos >= CMP_LEN - 1).astype(F32)] * Q_PER_KV, axis=0)
    blk = lax.broadcasted_iota(jnp.int32, (n_sel, QT), 0)
    blkf = blk.astype(F32)
    qpos = i * QT + lax.broadcasted_iota(jnp.int32, (n_sel, QT), 1)
    cur = lax.shift_right_arithmetic(qpos, int(math.log2(SEL_LEN)))
    forced = (blk == 0) | (blk == cur) | (blk == cur - 1)
    causal = blk * SEL_LEN <= qpos

    def gate_col(g, r, b):
        c = (Q_PER_KV * g + r) * N_BRANCH + b
        return jnp.broadcast_to(gates[:, c:c + 1], (QT, LANES))

    def gate_rows(g, b):
        return jnp.concatenate([gate_col(g, r, b) for r in range(Q_PER_KV)], axis=0)

    def normalise(pv):
        return pv * (1.0 / pltpu.roll(pv, HEAD_DIM, 1))

    def tab(g, lo, width):
        return wtab_ref[Q_PER_KV * g:Q_PER_KV * (g + 1), :, lo:lo + width].reshape(rq, width)

    for g in range(N_KV_HEADS):
        q = jnp.concatenate([q_ref[:, (Q_PER_KV * g + r) * LANES:(Q_PER_KV * g + r + 1) * LANES]
                             for r in range(Q_PER_KV)], axis=0)

        sc = jnp.dot(q, kc_ref[0, g], preferred_element_type=F32)
        sc = sc + ctab_ref[Q_PER_KV * g:Q_PER_KV * (g + 1)].reshape(rq, sc.shape[-1])
        e = jnp.exp(sc - jnp.max(sc, axis=-1, keepdims=True))
        p = e * (any_valid / jnp.sum(e, axis=-1, keepdims=True))
        pv_c = jnp.dot(p.astype(BF16), vc_ref[0, g], preferred_element_type=F32)
        psum = p[0:QT] + p[QT:2 * QT] + p[2 * QT:3 * QT] + p[3 * QT:4 * QT]
        ps_t = sum(jnp.dot(ovlt_ref[...], t, preferred_element_type=F32) for t in _split3(psum.T))

        score = jnp.where(forced, BIG, jnp.where(causal, ps_t[SEL_LANE0:SEL_LANE0 + n_sel], -BIG))
        picked = jnp.zeros((n_sel, QT), F32)
        for _ in range(n_top):
            mx = jnp.max(score, axis=0, keepdims=True)
            first = jnp.min(jnp.where(score == mx, blkf, float(LANES)), axis=0, keepdims=True)
            hit = blkf == first
            picked = jnp.where(hit, 1.0, picked)
            score = jnp.where(hit, -jnp.inf, score)
        selb_t = jnp.where(picked == 0.0, NEG_INF, 0.0)
        selb = jnp.concatenate([jnp.zeros((SEL_LANE0, QT), F32), selb_t,
                                jnp.zeros((LANES - SEL_LANE0 - n_sel, QT), F32)], axis=0).T
        selb_far = jnp.where(far_cut, NEG_INF, selb)
        q_near = jnp.where(sel_lane4, jnp.concatenate([selb.astype(BF16)] * Q_PER_KV, axis=0), q)
        qf_scr[g] = jnp.where(sel_lane4, jnp.concatenate([selb_far.astype(BF16)] * Q_PER_KV, axis=0), q)

        k_w = jnp.concatenate([kw_ref[0, g, i + j] for j in range(n_win)], axis=1)
        s_w = jnp.dot(q, k_w, preferred_element_type=F32)
        parts = [s_w[:, :QT] + tab(g, 0, QT), s_w[:, QT:(n_win - 2) * QT],
                 s_w[:, (n_win - 2) * QT:] + tab(g, QT, 2 * QT)]
        mw = jnp.max(jnp.concatenate(parts, axis=1), axis=-1, keepdims=True)
        p_w = jnp.concatenate([jnp.exp(t - mw).astype(BF16) for t in parts], axis=1)
        v_w = jnp.concatenate([vw_ref[0, g, i + j] for j in range(n_win)], axis=0)
        o_w = normalise(jnp.dot(p_w, v_w, preferred_element_type=F32))
        out_scr[g] = gate_rows(g, 0) * pv_c + gate_rows(g, 2) * o_w

        k_n = jnp.concatenate([ks_ref[0, g, i], ks_ref[0, g, i + 1]], axis=1)
        s_n = jnp.dot(q_near, k_n, preferred_element_type=F32) + tab(g, QT, 2 * QT)
        sn_scr[g] = s_n
        m_scr[g] = jnp.maximum(s_n[:, :QT], s_n[:, QT:])

    n_chunks = (jnp.maximum(i - 1, 0) + FAR_TILES - 1) // FAR_TILES

    def far_logits(c, carry):
        for g in range(N_KV_HEADS):
            k_f = jnp.concatenate([ks_ref[0, g, FAR_TILES * c + 1 + t] for t in range(FAR_TILES)], axis=1)
            s = jnp.dot(qf_scr[g], k_f, preferred_element_type=F32)
            s_scr[g, c] = s
            m = m_scr[g]
            for t in range(FAR_TILES):
                m = jnp.maximum(m, s[:, t * QT:(t + 1) * QT])
            m_scr[g] = m
        return carry

    lax.fori_loop(0, n_chunks, far_logits, 0)

    for g in range(N_KV_HEADS):
        m = jnp.max(m_scr[g], axis=-1, keepdims=True)
        m_scr[g] = jnp.broadcast_to(m, (rq, QT))
        p_n = jnp.exp(sn_scr[g] - m).astype(BF16)
        v_n = jnp.concatenate([vs_ref[0, g, i], vs_ref[0, g, i + 1]], axis=0)
        acc_scr[g] = jnp.dot(p_n, v_n, preferred_element_type=F32)

    def far_pv(c, carry):
        for g in range(N_KV_HEADS):
            m = jnp.concatenate([m_scr[g]] * FAR_TILES, axis=1)
            p = jnp.exp(s_scr[g, c] - m).astype(BF16)
            v_f = jnp.concatenate([vs_ref[0, g, FAR_TILES * c + 1 + t] for t in range(FAR_TILES)], axis=0)
            acc_scr[g] += jnp.dot(p, v_f, preferred_element_type=F32)
        return carry

    lax.fori_loop(0, n_chunks, far_pv, 0)

    for g in range(N_KV_HEADS):
        o = out_scr[g] + gate_rows(g, 1) * normalise(acc_scr[g])
        heads = [jnp.where(low, o[r * QT:(r + 1) * QT], 0.0) for r in range(Q_PER_KV)]
        for pr in range(Q_PER_KV // 2):
            col = (Q_PER_KV * g // 2 + pr) * LANES
            pair = heads[2 * pr] + pltpu.roll(heads[2 * pr + 1], HEAD_DIM, 1)
            o_ref[:, col:col + LANES] = pair.astype(BF16)


def _rel_bucket(dist):
    n = jnp.maximum(dist, 0)
    max_exact = NUM_BUCKETS // 2
    logv = jnp.log(jnp.maximum(n, 1).astype(F32) / max_exact) / math.log(MAX_DISTANCE / max_exact)
    large = jnp.minimum(max_exact + (logv * (NUM_BUCKETS - max_exact)).astype(jnp.int32), NUM_BUCKETS - 1)
    return jnp.where(n < max_exact, n, large)


def _bias_tables(rel_bias, seq):
    nt = seq // QT
    nc = seq // CMP_STRIDE
    cpt = QT // CMP_STRIDE
    f = lambda dist: rel_bias[_rel_bucket(dist)].transpose(2, 0, 1)
    c_rel = jnp.arange(-cpt * (nt - 1), nc)
    dc = jnp.arange(QT)[:, None] - (c_rel * CMP_STRIDE + CMP_LEN - 1)[None, :]
    t0 = jnp.where((dc >= 0)[None], f(dc), NEG_INF)
    ctab = jnp.stack([t0[:, :, cpt * (nt - 1 - i):cpt * (nt - 1 - i) + nc] for i in range(nt)], axis=1)
    ctab = ctab.reshape(N_HEADS, seq, nc)
    far = rel_bias[NUM_BUCKETS - 1].reshape(N_HEADS, 1, 1)
    dq = jnp.arange(QT)[:, None] - jnp.arange(QT)[None, :]
    d0, d3, d4 = WIN + dq, QT + dq, dq
    wtab = jnp.concatenate([jnp.where((d0 < WIN)[None], f(d0) - far, NEG_INF), f(d3) - far,
                            jnp.where((d4 >= 0)[None], f(d4) - far, NEG_INF)], axis=2)
    return ctab, wtab


def _overlap_matrix_t(seq):
    nc = seq // CMP_STRIDE
    n_sel = seq // SEL_LEN
    c_start = jnp.arange(nc) * CMP_STRIDE
    j_start = jnp.arange(n_sel) * SEL_LEN
    ov = jnp.clip(jnp.minimum(c_start[None, :] + CMP_LEN, j_start[:, None] + SEL_LEN)
                  - jnp.maximum(c_start[None, :], j_start[:, None]), 0, None)
    ov = ov.astype(F32) / CMP_LEN
    return jnp.pad(ov, ((SEL_LANE0, LANES - SEL_LANE0 - n_sel), (0, 0))).astype(BF16)


def _key_tiles(k, bsz, seq, pad_front, with_sel):
    nt = seq // QT
    k = k.reshape(bsz, nt, QT, N_KV_HEADS, LANES).transpose(0, 3, 1, 4, 2)
    row = jnp.arange(LANES).reshape(1, LANES, 1)
    extra = ((row >= FARB_LANE0) & (row < FARB_LANE0 + 3)).astype(BF16)
    if with_sel:
        key = jnp.arange(seq).reshape(nt, 1, QT)
        extra = extra + (row - SEL_LANE0 == key // SEL_LEN).astype(BF16)
    k = k + extra[None, None]
    pad = jnp.broadcast_to((row == PAD_LANE).astype(BF16), (bsz, N_KV_HEADS, pad_front, LANES, QT))
    return jnp.concatenate([pad, k], axis=2)


def _value_tiles(v, bsz, seq, pad_front):
    nt = seq // QT
    v = v.reshape(bsz, nt, QT, N_KV_HEADS, HEAD_DIM).transpose(0, 3, 1, 2, 4)
    v = jnp.concatenate([v, jnp.ones_like(v)], axis=-1)
    return jnp.pad(v, ((0, 0), (0, 0), (pad_front, 0), (0, 0), (0, 0)))


def _attention(q, gates, kv, rel_bias, bsz, seq):
    kc, vc, ks, vs, kw, vw = kv
    nt = seq // QT
    nc = seq // CMP_STRIDE
    n_sel = seq // SEL_LEN
    n_top = min(SEL_TOPN, n_sel)
    n_chunk = max(-(-(nt - 2) // FAR_TILES), 1)
    wwin = WIN // QT
    ctab, wtab = _bias_tables(rel_bias, seq)
    rq = Q_PER_KV * QT
    per_batch = lambda shape: pl.BlockSpec((1,) + shape, lambda b, i: (b,) + (0,) * len(shape),
                                           pipeline_mode=pl.Buffered(1))
    tok = lambda c: pl.BlockSpec((QT, c), lambda b, i: (b * nt + i, 0))
    nts = max(nt, n_chunk * FAR_TILES) + 1
    tail = ((0, 0), (0, 0), (0, nts - nt - 1), (0, 0), (0, 0))
    ks_t = jnp.pad(_key_tiles(ks, bsz, seq, 1, True), tail)
    vs_t = jnp.pad(_value_tiles(vs, bsz, seq, 1), tail)
    return pl.pallas_call(
        functools.partial(_attn_body, n_sel=n_sel, n_top=n_top),
        grid=(bsz, nt),
        in_specs=[tok(N_HEADS * LANES), tok(LANES),
                  per_batch((N_KV_HEADS, LANES, nc)),
                  per_batch((N_KV_HEADS, nc, LANES)),
                  per_batch((N_KV_HEADS, nts, LANES, QT)),
                  per_batch((N_KV_HEADS, nts, QT, LANES)),
                  per_batch((N_KV_HEADS, nt + wwin, LANES, QT)),
                  per_batch((N_KV_HEADS, nt + wwin, QT, LANES)),
                  pl.BlockSpec((N_HEADS, QT, nc), lambda b, i: (0, i, 0)),
                  _const_spec((N_HEADS, QT, 3 * QT)),
                  _const_spec((LANES, nc))],
        out_specs=tok(D_MODEL),
        out_shape=jax.ShapeDtypeStruct((bsz * seq, D_MODEL), BF16),
        scratch_shapes=[pltpu.VMEM((N_KV_HEADS, n_chunk, rq, FAR_TILES * QT), F32),
                        pltpu.VMEM((N_KV_HEADS, rq, 2 * QT), F32),
                        pltpu.VMEM((N_KV_HEADS, rq, QT), F32),
                        pltpu.VMEM((N_KV_HEADS, rq, LANES), BF16),
                        pltpu.VMEM((N_KV_HEADS, rq, LANES), F32),
                        pltpu.VMEM((N_KV_HEADS, rq, LANES), F32)],
        compiler_params=_cparams(("parallel", "arbitrary")),
        name="nsa_attn",
    )(q, gates, kc, vc, ks_t, vs_t,
      _key_tiles(kw, bsz, seq, wwin, False), _value_tiles(vw, bsz, seq, wwin),
      ctab, wtab, _overlap_matrix_t(seq))


def kernel(x, rel_bias, ffn1_norm, ffn1_w_in, ffn1_w_out, mix_norm, ffn2_norm, ffn2_w_in, ffn2_w_out,
           s5_a_re, s5_a_im, s5_log_dt, s5_b_re, s5_b_im, s5_c_re, s5_c_im, s5_d, s5_w_glu,
           kv_norm, w_kv, k_norm_cmp, k_norm_slc, k_norm_win, cmp_pos_k, cmp_pos_v,
           cmp_k_w1, cmp_k_w2, cmp_v_w1, cmp_v_w2, w_qg, q_norm, w_o):
    bsz, seq, _ = x.shape
    depth = ffn1_norm.shape[0]
    n_a = s5_a_re.shape[0]
    h = x.reshape(bsz * seq, D_MODEL)
    kv = None
    for layer in range(depth):
        s5_layer = layer < n_a
        h = _ffn(h, ffn1_norm[layer], ffn1_w_in[layer], ffn1_w_out[layer], bsz, seq,
                 in_tm=False, out_tm=s5_layer)
        pre = None
        if s5_layer:
            a = layer
            h = _s5(h, mix_norm[layer], s5_a_re[a], s5_a_im[a], s5_log_dt[a], s5_b_re[a], s5_b_im[a],
                    s5_c_re[a], s5_c_im[a], s5_d[a], s5_w_glu[a], bsz, seq)
        else:
            b = layer - n_a
            q, gates = _qproj(h, mix_norm[layer], w_qg[b], q_norm[b], rel_bias, bsz, seq)
            pre = (_attention(q, gates, kv, rel_bias, bsz, seq), w_o[b])
        h = _ffn(h, ffn2_norm[layer], ffn2_w_in[layer], ffn2_w_out[layer], bsz, seq,
                 in_tm=s5_layer, out_tm=False, pre=pre)
        if layer == n_a - 1:
            kc_raw, vc_raw, ks, vs, kw, vw = _kv_proj(h, kv_norm, w_kv, k_norm_slc, k_norm_win, bsz, seq)
            kc, vc = _compress(kc_raw, vc_raw, k_norm_cmp, cmp_pos_k, cmp_pos_v,
                               cmp_k_w1, cmp_k_w2, cmp_v_w1, cmp_v_w2, bsz, seq)
            kv = (kc, vc, ks, vs, kw, vw)
    return h.reshape(bsz, seq, D_MODEL)
```

```python
import functools
import math

import jax
import jax.numpy as jnp
from jax import lax
from jax.experimental import pallas as pl
from jax.experimental.pallas import tpu as pltpu

F32 = jnp.float32
BF16 = jnp.bfloat16

D_MODEL = 1024
D_FF = 2816
RMS_EPS = 1e-6
S5_GROUP_CH = 16
S5_GROUPS = D_MODEL // S5_GROUP_CH
S5_STATE = 64
N_HEADS = 16
HEAD_DIM = 64
N_KV_HEADS = 4
Q_PER_KV = N_HEADS // N_KV_HEADS
CMP_LEN = 32
CMP_STRIDE = 16
CMP_HIDDEN = 2 * HEAD_DIM
SEL_LEN = 64
SEL_TOPN = 8
WIN = 512
N_BRANCH = 3
ATTN_SCALE = HEAD_DIM ** -0.5
NUM_BUCKETS = 32
MAX_DISTANCE = 128
NEG_INF = -1e30
BIG = 1e9

LANES = 128
MXU_DIM = 256
VMEM_LIMIT = 56 * 1024 * 1024

FFN_ROWS = 512
FF_CHUNK = MXU_DIM
S5_TC = 16
S5_SLAB = MXU_DIM
S5_LANES = 256
QP_ROWS = 512
QT = 128
FAR_TILES = 4
PAD_TILES = WIN // QT
LOG2E = math.log2(math.e)

SEL_LANE0 = HEAD_DIM
FARB_LANE0 = 96
PAD_LANE = 99


def _rms(x, n):
    ss = jnp.sum(x * x, axis=-1, keepdims=True)
    return x * lax.rsqrt(ss * (1.0 / n) + RMS_EPS)


def _cparams(sem):
    return pltpu.CompilerParams(dimension_semantics=sem, vmem_limit_bytes=VMEM_LIMIT)


def _const_spec(shape):
    nd = len(shape)
    return pl.BlockSpec(shape, lambda *_: (0,) * nd, pipeline_mode=pl.Buffered(1))


def _split3(x):
    x1 = x.astype(BF16)
    r1 = x - x1.astype(F32)
    x2 = r1.astype(BF16)
    x3 = (r1 - x2.astype(F32)).astype(BF16)
    return x1, x2, x3


def _ffn_body(*refs, pre):
    if pre:
        x_ref, po_ref, pw_ref, g_ref, win_ref, wout_ref, o_ref, act_ref = refs
        x = x_ref[...] + jnp.dot(po_ref[...], pw_ref[...], preferred_element_type=F32)
    else:
        x_ref, g_ref, win_ref, wout_ref, o_ref, act_ref = refs
        x = x_ref[...]
    xn = (_rms(x, D_MODEL) * g_ref[...]).astype(BF16)
    for c in range(D_FF // FF_CHUNK):
        lo = c * FF_CHUNK
        a = jnp.dot(xn, win_ref[:, lo:lo + FF_CHUNK], preferred_element_type=F32)
        b = jnp.dot(xn, win_ref[:, D_FF + lo:D_FF + lo + FF_CHUNK], preferred_element_type=F32)
        act_ref[:, lo:lo + FF_CHUNK] = (a * jax.nn.sigmoid(a) * b).astype(BF16)
    y = jnp.dot(act_ref[...], wout_ref[...], preferred_element_type=F32)
    o_ref[...] = x + 0.5 * y


def _token_spec(rows, seq, time_major):
    nt = seq // rows
    if time_major:
        return pl.BlockSpec((rows, D_MODEL), lambda b, t: (t, b))
    return pl.BlockSpec((rows, D_MODEL), lambda b, t: (b * nt + t, 0))


def _ffn(h, gain, w_in, w_out, bsz, seq, in_tm, out_tm, pre=None):
    rows = min(FFN_ROWS, seq)
    out_shape = (seq, bsz * D_MODEL) if out_tm else (bsz * seq, D_MODEL)
    args, specs = [h], [_token_spec(rows, seq, in_tm)]
    if pre is not None:
        args += [pre[0], pre[1].astype(BF16)]
        specs += [_token_spec(rows, seq, False), _const_spec((D_MODEL, D_MODEL))]
    args += [gain.reshape(1, D_MODEL), w_in.astype(BF16), w_out.astype(BF16)]
    specs += [_const_spec((1, D_MODEL)), _const_spec((D_MODEL, 2 * D_FF)), _const_spec((D_FF, D_MODEL))]
    return pl.pallas_call(
        functools.partial(_ffn_body, pre=pre is not None),
        grid=(bsz, seq // rows),
        in_specs=specs,
        out_specs=_token_spec(rows, seq, out_tm),
        out_shape=jax.ShapeDtypeStruct(out_shape, F32),
        scratch_shapes=[pltpu.VMEM((rows, D_FF), BF16)],
        compiler_params=_cparams(("parallel", "parallel")),
        name="ffn",
    )(*args)


def _s5_body(h_ref, g_ref, bmat_ref, cmat_ref, are_ref, aim_ref, d_ref, wglu_ref,
             o_ref, bu_ref, st_ref, y_ref, *, tc, nb):
    n_slab = D_MODEL // S5_SLAB
    half = S5_SLAB // S5_GROUP_CH * S5_STATE
    rows = tc * nb

    @pl.when(pl.program_id(0) == 0)
    def _():
        st_ref[...] = jnp.zeros_like(st_ref)

    h = h_ref[...].reshape(rows, D_MODEL)
    u = _rms(h, D_MODEL) * g_ref[...]
    ub = u.astype(BF16)
    for sl in range(n_slab):
        bu_ref[...] = jnp.dot(ub[:, sl * S5_SLAB:(sl + 1) * S5_SLAB], bmat_ref[sl],
                              preferred_element_type=F32)
        for wb in range(half // S5_LANES):
            lo = wb * S5_LANES
            ar = jnp.broadcast_to(are_ref[sl, :, lo:lo + S5_LANES], (nb, S5_LANES))
            ai = jnp.broadcast_to(aim_ref[sl, :, lo:lo + S5_LANES], (nb, S5_LANES))

            def step(t, carry, lo=lo, ar=ar, ai=ai):
                xr, xi = carry
                r0 = pl.multiple_of(t * nb, nb)
                br = bu_ref[pl.ds(r0, nb), lo:lo + S5_LANES]
                bi = bu_ref[pl.ds(r0, nb), half + lo:half + lo + S5_LANES]
                nr = ar * xr - ai * xi + br
                ni = ar * xi + ai * xr + bi
                bu_ref[pl.ds(r0, nb), lo:lo + S5_LANES] = nr
                bu_ref[pl.ds(r0, nb), half + lo:half + lo + S5_LANES] = ni
                return nr, ni

            xr, xi = lax.fori_loop(
                0, tc, step,
                (st_ref[sl, :, lo:lo + S5_LANES], st_ref[sl, :, half + lo:half + lo + S5_LANES]))
            st_ref[sl, :, lo:lo + S5_LANES] = xr
            st_ref[sl, :, half + lo:half + lo + S5_LANES] = xi
        y_ref[:, sl * S5_SLAB:(sl + 1) * S5_SLAB] = jnp.dot(
            bu_ref[...].astype(BF16), cmat_ref[sl], preferred_element_type=F32)
    y = jax.nn.gelu(y_ref[...] + d_ref[...] * u)
    gate = jnp.dot(y.astype(BF16), wglu_ref[...], preferred_element_type=F32)
    o_ref[...] = (h + y * jax.nn.sigmoid(gate)).reshape(tc, nb, D_MODEL)


def _s5_params(a_re, a_im, log_dt, b_re, b_im, c_re, c_im):
    dt = jnp.exp(log_dt)[:, None]
    mag = jnp.exp(a_re * dt)
    ab_re = mag * jnp.cos(a_im * dt)
    ab_im = mag * jnp.sin(a_im * dt)
    den = a_re * a_re + a_im * a_im
    z_re = ((ab_re - 1.0) * a_re + ab_im * a_im) / den
    z_im = (ab_im * a_re - (ab_re - 1.0) * a_im) / den
    bb_re = z_re[..., None] * b_re - z_im[..., None] * b_im
    bb_im = z_re[..., None] * b_im + z_im[..., None] * b_re
    n_slab = D_MODEL // S5_SLAB
    gps = S5_GROUPS // n_slab
    eye = jnp.eye(gps, dtype=F32)

    def in_mat(bb):
        bb = bb.reshape(n_slab, gps, S5_STATE, S5_GROUP_CH)
        m = jnp.einsum('sgph,gk->sghkp', bb, eye)
        return m.reshape(n_slab, gps * S5_GROUP_CH, gps * S5_STATE)

    def out_mat(cc):
        cc = cc.reshape(n_slab, gps, S5_GROUP_CH, S5_STATE)
        m = jnp.einsum('sghp,gk->sgpkh', cc, eye)
        return m.reshape(n_slab, gps * S5_STATE, gps * S5_GROUP_CH)

    bmat = jnp.concatenate([in_mat(bb_re), in_mat(bb_im)], axis=2).astype(BF16)
    cmat = jnp.concatenate([out_mat(c_re), out_mat(-c_im)], axis=1).astype(BF16)
    are = ab_re.reshape(n_slab, 1, gps * S5_STATE)
    aim = ab_im.reshape(n_slab, 1, gps * S5_STATE)
    return bmat, cmat, are, aim


def _s5(h_tm, gain, a_re, a_im, log_dt, b_re, b_im, c_re, c_im, d_skip, w_glu, bsz, seq):
    tc = min(S5_TC, seq)
    n_slab = D_MODEL // S5_SLAB
    half = S5_SLAB // S5_GROUP_CH * S5_STATE
    bmat, cmat, are, aim = _s5_params(a_re, a_im, log_dt, b_re, b_im, c_re, c_im)
    blk = pl.BlockSpec((tc, bsz, D_MODEL), lambda i: (i, 0, 0))
    out = pl.pallas_call(
        functools.partial(_s5_body, tc=tc, nb=bsz),
        grid=(seq // tc,),
        in_specs=[blk,
                  _const_spec((1, D_MODEL)),
                  _const_spec((n_slab, S5_SLAB, 2 * half)),
                  _const_spec((n_slab, 2 * half, S5_SLAB)),
                  _const_spec((n_slab, 1, half)),
                  _const_spec((n_slab, 1, half)),
                  _const_spec((1, D_MODEL)),
                  _const_spec((D_MODEL, D_MODEL))],
        out_specs=blk,
        out_shape=jax.ShapeDtypeStruct((seq, bsz, D_MODEL), F32),
        scratch_shapes=[pltpu.VMEM((tc * bsz, 2 * half), F32),
                        pltpu.VMEM((n_slab, bsz, 2 * half), F32),
                        pltpu.VMEM((tc * bsz, D_MODEL), F32)],
        compiler_params=_cparams(("arbitrary",)),
        name="s5",
    )(h_tm.reshape(seq, bsz, D_MODEL), gain.reshape(1, D_MODEL), bmat, cmat, are, aim,
      d_skip.reshape(1, D_MODEL), w_glu.astype(BF16))
    return out.reshape(seq, bsz * D_MODEL)


def _kv_body(h_ref, g_ref, w_ref, gk_ref, kc_ref, vc_ref, ks_ref, vs_ref, kw_ref, vw_ref, *, n_real):
    s = pl.program_id(1)
    gw = N_KV_HEADS * HEAD_DIM
    pw = N_KV_HEADS * LANES
    tiles = ks_ref.shape[2]
    row = lax.broadcasted_iota(jnp.int32, (LANES, QT), 0)
    key_hi = (lax.broadcasted_iota(jnp.int32, (LANES, QT), 1) >= SEL_LEN).astype(jnp.int32)
    farb_rows = ((row >= FARB_LANE0) & (row < FARB_LANE0 + 3)).astype(F32)
    ones = jnp.ones((QT, LANES), BF16)

    @pl.when(s < n_real)
    def _():
        hn = (_rms(h_ref[...], D_MODEL) * g_ref[...]).astype(BF16)
        kv = jnp.dot(hn, w_ref[...], preferred_element_type=F32)
        kc_ref[...] = kv[:, 0:gw].astype(BF16)
        vc_ref[...] = kv[:, gw:2 * gw].astype(BF16)
        for br, (k_ref, v_ref) in enumerate(((ks_ref, vs_ref), (kw_ref, vw_ref))):
            for g in range(N_KV_HEADS):
                lo = 2 * gw + br * 2 * pw + g * LANES
                k = _rms(kv[:, lo:lo + LANES], HEAD_DIM) * gk_ref[br]
                v = kv[:, lo + pw:lo + pw + LANES].astype(BF16)
                for t in range(tiles):
                    extra = farb_rows
                    if br == 0:
                        blk = 2 * (tiles * s + t) + key_hi
                        extra = extra + (row - SEL_LANE0 == blk).astype(F32)
                    k_ref[0, g, t] = (k[t * QT:(t + 1) * QT].T + extra).astype(BF16)
                    v_ref[0, g, t] = jnp.concatenate([v[t * QT:(t + 1) * QT], ones], axis=1)

    @pl.when(s == n_real)
    def _():
        pad_k = jnp.broadcast_to((row == PAD_LANE).astype(BF16), (N_KV_HEADS, tiles, LANES, QT))
        pad_v = jnp.zeros((N_KV_HEADS, tiles, QT, 2 * LANES), BF16)
        for k_ref, v_ref in ((ks_ref, vs_ref), (kw_ref, vw_ref)):
            k_ref[0] = pad_k
            v_ref[0] = pad_v


def _pad_heads(w, n_heads):
    k = w.shape[0]
    w = w.reshape(k, n_heads, HEAD_DIM)
    return jnp.pad(w, ((0, 0), (0, 0), (0, LANES - HEAD_DIM))).reshape(k, n_heads * LANES)


def _kv_proj(h, kv_norm, w_kv, k_norm_slc, k_norm_win, bsz, seq):
    gw = N_KV_HEADS * HEAD_DIM
    tiles = PAD_TILES
    rows = tiles * QT
    n_real = seq // rows
    nt = seq // QT
    wk = w_kv.reshape(D_MODEL, 2 * N_BRANCH, gw)
    w = jnp.concatenate([wk[:, 0], wk[:, 1]] + [_pad_heads(wk[:, j], N_KV_HEADS) for j in (2, 3, 4, 5)],
                        axis=1).astype(BF16)
    gk = jnp.pad(jnp.stack([k_norm_slc, k_norm_win]), ((0, 0), (0, LANES - HEAD_DIM)))
    gk = gk.reshape(2, 1, LANES)
    n = bsz * seq
    real = lambda s: jnp.minimum(s, n_real - 1)
    tok = lambda c: pl.BlockSpec((rows, c), lambda b, s: (b * n_real + real(s), 0))
    tile_spec = lambda r, c: pl.BlockSpec((1, N_KV_HEADS, tiles, r, c), lambda b, s: (b, 0, s, 0, 0))
    dense = jax.ShapeDtypeStruct((n, gw), BF16)
    k_tiles = jax.ShapeDtypeStruct((bsz, N_KV_HEADS, nt + tiles, LANES, QT), BF16)
    v_tiles = jax.ShapeDtypeStruct((bsz, N_KV_HEADS, nt + tiles, QT, 2 * LANES), BF16)
    return pl.pallas_call(
        functools.partial(_kv_body, n_real=n_real),
        grid=(bsz, n_real + 1),
        in_specs=[tok(D_MODEL), _const_spec((1, D_MODEL)), _const_spec(w.shape),
                  _const_spec((2, 1, LANES))],
        out_specs=[tok(gw), tok(gw), tile_spec(LANES, QT), tile_spec(QT, 2 * LANES),
                   tile_spec(LANES, QT), tile_spec(QT, 2 * LANES)],
        out_shape=[dense, dense, k_tiles, v_tiles, k_tiles, v_tiles],
        compiler_params=_cparams(("parallel", "arbitrary")),
        name="kv_proj",
    )(h, kv_norm.reshape(1, D_MODEL), w, gk)


def _cmp_body(xk_ref, xv_ref, w1k_ref, w1v_ref, bk_ref, bv_ref, w2k_ref, w2v_ref, gk_ref,
              kc_ref, vc_ref, *, nh):
    for g in range(N_KV_HEADS):
        outs = []
        for x_ref, w1_ref, b_ref, w2_ref in ((xk_ref, w1k_ref, bk_ref, w2k_ref),
                                             (xv_ref, w1v_ref, bv_ref, w2v_ref)):
            ab = jnp.dot(x_ref[0, g], w1_ref[...], preferred_element_type=F32)
            hid = ab[:, :CMP_HIDDEN] + pltpu.roll(ab[:, CMP_HIDDEN:], nh - 1, 0) + b_ref[...]
            hid = jax.nn.gelu(hid).astype(BF16)
            outs.append(jnp.dot(hid, w2_ref[...], preferred_element_type=F32))
        k = _rms(outs[0], HEAD_DIM) * gk_ref[...]
        kc_ref[0, g] = k.T.astype(BF16)
        vc_ref[0, g] = outs[1].astype(BF16)


def _compress(kc_raw, vc_raw, k_norm_cmp, pos_k, pos_v, k_w1, k_w2, v_w1, v_w2, bsz, seq):
    nh = seq // CMP_STRIDE
    hb = CMP_STRIDE * HEAD_DIM

    def halfblocks(x):
        x = x.reshape(bsz, nh, CMP_STRIDE, N_KV_HEADS, HEAD_DIM)
        return x.transpose(0, 3, 1, 2, 4).reshape(bsz, N_KV_HEADS, nh, hb)

    def w1cat(w1):
        w = w1.reshape(2, hb, CMP_HIDDEN)
        return jnp.concatenate([w[0], w[1]], axis=1).astype(BF16)

    def w2pad(w2):
        return jnp.pad(w2, ((0, 0), (0, LANES - HEAD_DIM))).astype(BF16)

    bias = lambda pos, w1: jnp.einsum('ld,ldh->h', pos, w1,
                                      precision=lax.Precision.HIGHEST).reshape(1, CMP_HIDDEN)
    gk = jnp.pad(k_norm_cmp, (0, LANES - HEAD_DIM)).reshape(1, LANES)
    xspec = pl.BlockSpec((1, N_KV_HEADS, nh, hb), lambda b: (b, 0, 0, 0))
    ospec = pl.BlockSpec((1, N_KV_HEADS, LANES, nh), lambda b: (b, 0, 0, 0))
    vspec = pl.BlockSpec((1, N_KV_HEADS, nh, LANES), lambda b: (b, 0, 0, 0))
    return pl.pallas_call(
        functools.partial(_cmp_body, nh=nh),
        grid=(bsz,),
        in_specs=[xspec, xspec, _const_spec((hb, 2 * CMP_HIDDEN)), _const_spec((hb, 2 * CMP_HIDDEN)),
                  _const_spec((1, CMP_HIDDEN)), _const_spec((1, CMP_HIDDEN)),
                  _const_spec((CMP_HIDDEN, LANES)), _const_spec((CMP_HIDDEN, LANES)),
                  _const_spec((1, LANES))],
        out_specs=[ospec, vspec],
        out_shape=[jax.ShapeDtypeStruct((bsz, N_KV_HEADS, LANES, nh), BF16),
                   jax.ShapeDtypeStruct((bsz, N_KV_HEADS, nh, LANES), BF16)],
        compiler_params=_cparams(("parallel",)),
        name="kv_compress",
    )(halfblocks(kc_raw), halfblocks(vc_raw), w1cat(k_w1), w1cat(v_w1),
      bias(pos_k, k_w1), bias(pos_v, v_w1), w2pad(k_w2), w2pad(v_w2), gk)


def _qproj_body(h_ref, g_ref, w_ref, qsc_ref, qc_ref, q_ref, gate_ref):
    u = (_rms(h_ref[...], D_MODEL) * g_ref[...]).astype(BF16)
    qg = jnp.dot(u, w_ref[...], preferred_element_type=F32)
    for hh in range(N_HEADS):
        sl = slice(hh * LANES, (hh + 1) * LANES)
        q_ref[:, sl] = (_rms(qg[:, sl], HEAD_DIM) * qsc_ref[:, sl] + qc_ref[:, sl]).astype(BF16)
    gate_ref[...] = jax.nn.sigmoid(qg[:, N_HEADS * LANES:])


def _qproj(h, mix_gain, w_qg, q_norm, rel_bias, bsz, seq):
    rows = min(QP_ROWS, seq)
    n = bsz * seq
    nq = N_HEADS * HEAD_DIM
    wq = jnp.concatenate([_pad_heads(w_qg[:, :nq], N_HEADS),
                          jnp.pad(w_qg[:, nq:], ((0, 0), (0, LANES - N_BRANCH * N_HEADS)))],
                         axis=1).astype(BF16)
    qsc = jnp.tile(jnp.pad(q_norm * (ATTN_SCALE * LOG2E), (0, LANES - HEAD_DIM)), N_HEADS)
    qsc = qsc.reshape(1, N_HEADS * LANES)
    far = _split3(rel_bias[NUM_BUCKETS - 1] * LOG2E)
    qc = jnp.zeros((N_HEADS, LANES), F32)
    for t, term in enumerate(far):
        qc = qc.at[:, FARB_LANE0 + t].set(term.astype(F32))
    qc = qc.at[:, PAD_LANE].set(NEG_INF).reshape(1, N_HEADS * LANES)
    tok = lambda c: pl.BlockSpec((rows, c), lambda i: (i, 0))
    return pl.pallas_call(
        _qproj_body,
        grid=(n // rows,),
        in_specs=[tok(D_MODEL), _const_spec((1, D_MODEL)), _const_spec(wq.shape),
                  _const_spec((1, N_HEADS * LANES)), _const_spec((1, N_HEADS * LANES))],
        out_specs=[tok(N_HEADS * LANES), tok(LANES)],
        out_shape=[jax.ShapeDtypeStruct((n, N_HEADS * LANES), BF16),
                   jax.ShapeDtypeStruct((n, LANES), F32)],
        compiler_params=_cparams(("parallel",)),
        name="q_proj",
    )(h, mix_gain.reshape(1, D_MODEL), wq, qsc, qc)


def _attn_body(q_ref, gate_ref, kc_ref, vc_ref, ks_ref, vs_ref, kw_ref, vw_ref,
               ctab_ref, wtab_ref, ovlt_ref, gexp_ref,
               o_ref, s_scr, sn_scr, m_scr, qf_scr, acc_scr, out_scr, *, n_sel, n_top, nt):
    i = pl.program_id(1)
    rq = Q_PER_KV * QT
    n_win = WIN // QT + 1
    lane = lax.broadcasted_iota(jnp.int32, (QT, LANES), 1)
    sel_lane = (lane >= SEL_LANE0) & (lane < SEL_LANE0 + n_sel)
    far_cut = sel_lane & (lane - SEL_LANE0 >= 2 * (i - 1))
    sel_lane4 = jnp.concatenate([sel_lane] * Q_PER_KV, axis=0)
    blk = lax.broadcasted_iota(jnp.int32, (n_sel, QT), 0)
    blkf = blk.astype(F32)
    qpos = i * QT + lax.broadcasted_iota(jnp.int32, (n_sel, QT), 1)
    cur = lax.shift_right_arithmetic(qpos, int(math.log2(SEL_LEN)))
    forced = (blk == 0) | (blk == cur) | (blk == cur - 1)
    causal = blk * SEL_LEN <= qpos
    tile = lambda t: jnp.where(t < 0, nt, t)

    def gate_of(b):
        return sum(jnp.dot(t, gexp_ref[b], preferred_element_type=F32) for t in _split3(gate_ref[...])[:2])

    gexp = {0: gate_of(0), 2: gate_of(2)}

    def merged(o, g, pr):
        return o[2 * pr * QT:(2 * pr + 1) * QT] + pltpu.roll(o[(2 * pr + 1) * QT:(2 * pr + 2) * QT], HEAD_DIM, 1)

    def normalise(pv):
        return pv[:, :LANES] * (1.0 / pv[:, LANES:])

    def tab(g, lo, width):
        return wtab_ref[Q_PER_KV * g:Q_PER_KV * (g + 1), :, lo:lo + width].reshape(rq, width)

    for g in range(N_KV_HEADS):
        q = jnp.concatenate([q_ref[:, (Q_PER_KV * g + r) * LANES:(Q_PER_KV * g + r + 1) * LANES]
                             for r in range(Q_PER_KV)], axis=0)

        sc = jnp.dot(q, kc_ref[0, g], preferred_element_type=F32)
        sc = sc + ctab_ref[Q_PER_KV * g:Q_PER_KV * (g + 1)].reshape(rq, sc.shape[-1])
        e = jnp.exp2(sc - jnp.max(sc, axis=-1, keepdims=True))
        row_ok = i * QT + lax.broadcasted_iota(jnp.int32, (QT, sc.shape[-1]), 0) >= CMP_LEN - 1
        p = jnp.where(jnp.concatenate([row_ok] * Q_PER_KV, axis=0),
                      e * (1.0 / jnp.sum(e, axis=-1, keepdims=True)), 0.0)
        pv_c = jnp.dot(p.astype(BF16), vc_ref[0, g], preferred_element_type=F32)
        psum = p[0:QT] + p[QT:2 * QT] + p[2 * QT:3 * QT] + p[3 * QT:4 * QT]
        ps_t = sum(jnp.dot(ovlt_ref[...], t, preferred_element_type=F32) for t in _split3(psum.T))

        score = jnp.where(forced, BIG, jnp.where(causal, ps_t[SEL_LANE0:SEL_LANE0 + n_sel], -BIG))
        picked = jnp.zeros((n_sel, QT), F32)
        for _ in range(n_top):
            mx = jnp.max(score, axis=0, keepdims=True)
            first = jnp.min(jnp.where(score == mx, blkf, float(LANES)), axis=0, keepdims=True)
            hit = blkf == first
            picked = jnp.where(hit, 1.0, picked)
            score = jnp.where(hit, -jnp.inf, score)
        selb_t = jnp.where(picked == 0.0, NEG_INF, 0.0)
        selb = jnp.concatenate([jnp.zeros((SEL_LANE0, QT), F32), selb_t,
                                jnp.zeros((LANES - SEL_LANE0 - n_sel, QT), F32)], axis=0).T
        selb_far = jnp.where(far_cut, NEG_INF, selb)
        q_near = jnp.where(sel_lane4, jnp.concatenate([selb.astype(BF16)] * Q_PER_KV, axis=0), q)
        qf_scr[g] = jnp.where(sel_lane4, jnp.concatenate([selb_far.astype(BF16)] * Q_PER_KV, axis=0), q)

        w_tiles = [tile(i - (n_win - 1) + j) for j in range(n_win)]
        k_w = jnp.concatenate([kw_ref[0, g, t] for t in w_tiles], axis=1)
        s_w = jnp.dot(q, k_w, preferred_element_type=F32)
        parts = [s_w[:, :QT] + tab(g, 0, QT), s_w[:, QT:(n_win - 2) * QT],
                 s_w[:, (n_win - 2) * QT:] + tab(g, QT, 2 * QT)]
        mw = jnp.max(jnp.concatenate(parts, axis=1), axis=-1, keepdims=True)
        p_w = jnp.concatenate([jnp.exp2(t - mw).astype(BF16) for t in parts], axis=1)
        v_w = jnp.concatenate([vw_ref[0, g, t] for t in w_tiles], axis=0)
        o_w = normalise(jnp.dot(p_w, v_w, preferred_element_type=F32))
        for pr in range(Q_PER_KV // 2):
            col = (Q_PER_KV * g // 2 + pr) * LANES
            out_scr[:, col:col + LANES] = (gexp[0][:, col:col + LANES] * merged(pv_c, g, pr)
                                           + gexp[2][:, col:col + LANES] * merged(o_w, g, pr))

        k_n = jnp.concatenate([ks_ref[0, g, tile(i - 1)], ks_ref[0, g, i]], axis=1)
        s_n = jnp.dot(q_near, k_n, preferred_element_type=F32) + tab(g, QT, 2 * QT)
        sn_scr[g] = s_n
        m_scr[g] = jnp.maximum(s_n[:, :QT], s_n[:, QT:])

    n_chunks = (jnp.maximum(i - 1, 0) + FAR_TILES - 1) // FAR_TILES

    def far_logits(c, carry):
        for g in range(N_KV_HEADS):
            k_f = jnp.concatenate([ks_ref[0, g, FAR_TILES * c + t] for t in range(FAR_TILES)], axis=1)
            s = jnp.dot(qf_scr[g], k_f, preferred_element_type=F32)
            s_scr[g, c] = s
            m = m_scr[g]
            for t in range(FAR_TILES):
                m = jnp.maximum(m, s[:, t * QT:(t + 1) * QT])
            m_scr[g] = m
        return carry

    lax.fori_loop(0, n_chunks, far_logits, 0)

    for g in range(N_KV_HEADS):
        m = jnp.max(m_scr[g], axis=-1, keepdims=True)
        m_scr[g] = jnp.broadcast_to(m, (rq, QT))
        p_n = jnp.exp2(sn_scr[g] - m).astype(BF16)
        v_n = jnp.concatenate([vs_ref[0, g, tile(i - 1)], vs_ref[0, g, i]], axis=0)
        acc_scr[g] = jnp.dot(p_n, v_n, preferred_element_type=F32)

    def far_pv(c, carry):
        for g in range(N_KV_HEADS):
            m = jnp.concatenate([m_scr[g]] * FAR_TILES, axis=1)
            p = jnp.exp2(s_scr[g, c] - m).astype(BF16)
            v_f = jnp.concatenate([vs_ref[0, g, FAR_TILES * c + t] for t in range(FAR_TILES)], axis=0)
            acc_scr[g] += jnp.dot(p, v_f, preferred_element_type=F32)
        return carry

    lax.fori_loop(0, n_chunks, far_pv, 0)

    gexp[1] = gate_of(1)
    for g in range(N_KV_HEADS):
        o_s = normalise(acc_scr[g])
        for pr in range(Q_PER_KV // 2):
            col = (Q_PER_KV * g // 2 + pr) * LANES
            o = out_scr[:, col:col + LANES] + gexp[1][:, col:col + LANES] * merged(o_s, g, pr)
            o_ref[:, col:col + LANES] = o.astype(BF16)


def _rel_bucket(dist):
    n = jnp.maximum(dist, 0)
    max_exact = NUM_BUCKETS // 2
    logv = jnp.log(jnp.maximum(n, 1).astype(F32) / max_exact) / math.log(MAX_DISTANCE / max_exact)
    large = jnp.minimum(max_exact + (logv * (NUM_BUCKETS - max_exact)).astype(jnp.int32), NUM_BUCKETS - 1)
    return jnp.where(n < max_exact, n, large)


def _bias_tables(rel_bias, seq):
    nt = seq // QT
    nc = seq // CMP_STRIDE
    cpt = QT // CMP_STRIDE
    rb = rel_bias * LOG2E

    def f(dist):
        onehot = (_rel_bucket(dist)[..., None] == jnp.arange(NUM_BUCKETS)).astype(F32)
        return jnp.einsum('...k,kh->h...', onehot, rb, precision=lax.Precision.HIGHEST)

    c_rel = jnp.arange(-cpt * (nt - 1), nc)
    dc = jnp.arange(QT)[:, None] - (c_rel * CMP_STRIDE + CMP_LEN - 1)[None, :]
    t0 = jnp.where((dc >= 0)[None], f(dc), NEG_INF)
    ctab = jnp.stack([t0[:, :, cpt * (nt - 1 - i):cpt * (nt - 1 - i) + nc] for i in range(nt)], axis=1)
    ctab = ctab.reshape(N_HEADS, seq, nc)
    far = rb[NUM_BUCKETS - 1].reshape(N_HEADS, 1, 1)
    dq = jnp.arange(QT)[:, None] - jnp.arange(QT)[None, :]
    d0, d3, d4 = WIN + dq, QT + dq, dq
    wtab = jnp.concatenate([jnp.where((d0 < WIN)[None], f(d0) - far, NEG_INF), f(d3) - far,
                            jnp.where((d4 >= 0)[None], f(d4) - far, NEG_INF)], axis=2)
    return ctab, wtab


def _overlap_matrix_t(seq):
    nc = seq // CMP_STRIDE
    n_sel = seq // SEL_LEN
    c_start = jnp.arange(nc) * CMP_STRIDE
    j_start = jnp.arange(n_sel) * SEL_LEN
    ov = jnp.clip(jnp.minimum(c_start[None, :] + CMP_LEN, j_start[:, None] + SEL_LEN)
                  - jnp.maximum(c_start[None, :], j_start[:, None]), 0, None)
    ov = ov.astype(F32) / CMP_LEN
    return jnp.pad(ov, ((SEL_LANE0, LANES - SEL_LANE0 - n_sel), (0, 0))).astype(BF16)


def _gate_expansion():
    col = jnp.arange(LANES).reshape(1, LANES, 1)
    head = (jnp.arange(D_MODEL) // HEAD_DIM).reshape(1, 1, D_MODEL)
    br = jnp.arange(N_BRANCH).reshape(N_BRANCH, 1, 1)
    return (col == head * N_BRANCH + br).astype(BF16)


def _attention(q, gates, kv, rel_bias, bsz, seq):
    kc, vc, ks, vs, kw, vw = kv
    nt = seq // QT
    nc = seq // CMP_STRIDE
    n_sel = seq // SEL_LEN
    n_top = min(SEL_TOPN, n_sel)
    n_chunk = max(-(-(nt - 2) // FAR_TILES), 1)
    assert n_chunk * FAR_TILES <= nt
    nts = nt + PAD_TILES
    ctab, wtab = _bias_tables(rel_bias, seq)
    rq = Q_PER_KV * QT
    per_batch = lambda shape: pl.BlockSpec((1,) + shape, lambda b, i: (b,) + (0,) * len(shape),
                                           pipeline_mode=pl.Buffered(1))
    tok = lambda c: pl.BlockSpec((QT, c), lambda b, i: (b * nt + i, 0))
    return pl.pallas_call(
        functools.partial(_attn_body, n_sel=n_sel, n_top=n_top, nt=nt),
        grid=(bsz, nt),
        in_specs=[tok(N_HEADS * LANES), tok(LANES),
                  per_batch((N_KV_HEADS, LANES, nc)),
                  per_batch((N_KV_HEADS, nc, LANES)),
                  per_batch((N_KV_HEADS, nts, LANES, QT)),
                  per_batch((N_KV_HEADS, nts, QT, 2 * LANES)),
                  per_batch((N_KV_HEADS, nts, LANES, QT)),
                  per_batch((N_KV_HEADS, nts, QT, 2 * LANES)),
                  pl.BlockSpec((N_HEADS, QT, nc), lambda b, i: (0, i, 0)),
                  _const_spec((N_HEADS, QT, 3 * QT)),
                  _const_spec((LANES, nc)),
                  _const_spec((N_BRANCH, LANES, D_MODEL))],
        out_specs=tok(D_MODEL),
        out_shape=jax.ShapeDtypeStruct((bsz * seq, D_MODEL), BF16),
        scratch_shapes=[pltpu.VMEM((N_KV_HEADS, n_chunk, rq, FAR_TILES * QT), F32),
                        pltpu.VMEM((N_KV_HEADS, rq, 2 * QT), F32),
                        pltpu.VMEM((N_KV_HEADS, rq, QT), F32),
                        pltpu.VMEM((N_KV_HEADS, rq, LANES), BF16),
                        pltpu.VMEM((N_KV_HEADS, rq, 2 * LANES), F32),
                        pltpu.VMEM((QT, D_MODEL), F32)],
        compiler_params=_cparams(("parallel", "arbitrary")),
        name="nsa_attn",
    )(q, gates, kc, vc, ks, vs, kw, vw, ctab, wtab, _overlap_matrix_t(seq), _gate_expansion())


def kernel(x, rel_bias, ffn1_norm, ffn1_w_in, ffn1_w_out, mix_norm, ffn2_norm, ffn2_w_in, ffn2_w_out,
           s5_a_re, s5_a_im, s5_log_dt, s5_b_re, s5_b_im, s5_c_re, s5_c_im, s5_d, s5_w_glu,
           kv_norm, w_kv, k_norm_cmp, k_norm_slc, k_norm_win, cmp_pos_k, cmp_pos_v,
           cmp_k_w1, cmp_k_w2, cmp_v_w1, cmp_v_w2, w_qg, q_norm, w_o):
    bsz, seq, _ = x.shape
    depth = ffn1_norm.shape[0]
    n_a = s5_a_re.shape[0]
    h = x.reshape(bsz * seq, D_MODEL)
    kv = None
    for layer in range(depth):
        s5_layer = layer < n_a
        h = _ffn(h, ffn1_norm[layer], ffn1_w_in[layer], ffn1_w_out[layer], bsz, seq,
                 in_tm=False, out_tm=s5_layer)
        pre = None
        if s5_layer:
            a = layer
            h = _s5(h, mix_norm[layer], s5_a_re[a], s5_a_im[a], s5_log_dt[a], s5_b_re[a], s5_b_im[a],
                    s5_c_re[a], s5_c_im[a], s5_d[a], s5_w_glu[a], bsz, seq)
        else:
            b = layer - n_a
            q, gates = _qproj(h, mix_norm[layer], w_qg[b], q_norm[b], rel_bias, bsz, seq)
            pre = (_attention(q, gates, kv, rel_bias, bsz, seq), w_o[b])
        h = _ffn(h, ffn2_norm[layer], ffn2_w_in[layer], ffn2_w_out[layer], bsz, seq,
                 in_tm=s5_layer, out_tm=False, pre=pre)
        if layer == n_a - 1:
            kc_raw, vc_raw, ks, vs, kw, vw = _kv_proj(h, kv_norm, w_kv, k_norm_slc, k_norm_win, bsz, seq)
            kc, vc = _compress(kc_raw, vc_raw, k_norm_cmp, cmp_pos_k, cmp_pos_v,
                               cmp_k_w1, cmp_k_w2, cmp_v_w1, cmp_v_w2, bsz, seq)
            kv = (kc, vc, ks, vs, kw, vw)
    return h.reshape(bsz, seq, D_MODEL)
```

```python
import functools
import math

import jax
import jax.numpy as jnp
from jax import lax
from jax.experimental import pallas as pl
from jax.experimental.pallas import tpu as pltpu

F32 = jnp.float32
BF16 = jnp.bfloat16

D_MODEL = 1024
D_FF = 2816
RMS_EPS = 1e-6
S5_GROUP_CH = 16
S5_GROUPS = D_MODEL // S5_GROUP_CH
S5_STATE = 64
N_HEADS = 16
HEAD_DIM = 64
N_KV_HEADS = 4
Q_PER_KV = N_HEADS // N_KV_HEADS
CMP_LEN = 32
CMP_STRIDE = 16
CMP_HIDDEN = 2 * HEAD_DIM
SEL_LEN = 64
SEL_TOPN = 8
WIN = 512
N_BRANCH = 3
ATTN_SCALE = HEAD_DIM ** -0.5
NUM_BUCKETS = 32
MAX_DISTANCE = 128
NEG_INF = -1e30
BIG = 1e9

LANES = 128
MXU_DIM = 256
VMEM_LIMIT = 56 * 1024 * 1024

FFN_ROWS = 512
FF_CHUNK = MXU_DIM
S5_TC = 16
S5_SLAB = MXU_DIM
S5_LANES = 256
QP_ROWS = 512
QT = 128
FAR_TILES = 4
PAD_TILES = WIN // QT
LOG2E = math.log2(math.e)

SEL_LANE0 = HEAD_DIM
FARB_LANE0 = 96
PAD_LANE = 99


def _rms(x, n):
    ss = jnp.sum(x * x, axis=-1, keepdims=True)
    return x * lax.rsqrt(ss * (1.0 / n) + RMS_EPS)


def _cparams(sem):
    return pltpu.CompilerParams(dimension_semantics=sem, vmem_limit_bytes=VMEM_LIMIT)


def _const_spec(shape):
    nd = len(shape)
    return pl.BlockSpec(shape, lambda *_: (0,) * nd, pipeline_mode=pl.Buffered(1))


def _split3(x):
    x1 = x.astype(BF16)
    r1 = x - x1.astype(F32)
    x2 = r1.astype(BF16)
    x3 = (r1 - x2.astype(F32)).astype(BF16)
    return x1, x2, x3


def _ffn_body(*refs, pre):
    if pre:
        x_ref, po_ref, pw_ref, g_ref, win_ref, wout_ref, o_ref, act_ref = refs
        x = x_ref[...] + jnp.dot(po_ref[...], pw_ref[...], preferred_element_type=F32)
    else:
        x_ref, g_ref, win_ref, wout_ref, o_ref, act_ref = refs
        x = x_ref[...]
    xn = (_rms(x, D_MODEL) * g_ref[...]).astype(BF16)
    for c in range(D_FF // FF_CHUNK):
        lo = c * FF_CHUNK
        a = jnp.dot(xn, win_ref[:, lo:lo + FF_CHUNK], preferred_element_type=F32)
        b = jnp.dot(xn, win_ref[:, D_FF + lo:D_FF + lo + FF_CHUNK], preferred_element_type=F32)
        act_ref[:, lo:lo + FF_CHUNK] = (a * jax.nn.sigmoid(a) * b).astype(BF16)
    y = jnp.dot(act_ref[...], wout_ref[...], preferred_element_type=F32)
    o_ref[...] = x + 0.5 * y


def _token_spec(rows, seq, time_major):
    nt = seq // rows
    if time_major:
        return pl.BlockSpec((rows, D_MODEL), lambda b, t: (t, b))
    return pl.BlockSpec((rows, D_MODEL), lambda b, t: (b * nt + t, 0))


def _ffn(h, gain, w_in, w_out, bsz, seq, in_tm, out_tm, pre=None):
    rows = min(FFN_ROWS, seq)
    out_shape = (seq, bsz * D_MODEL) if out_tm else (bsz * seq, D_MODEL)
    args, specs = [h], [_token_spec(rows, seq, in_tm)]
    if pre is not None:
        args += [pre[0], pre[1].astype(BF16)]
        specs += [_token_spec(rows, seq, False), _const_spec((D_MODEL, D_MODEL))]
    args += [gain.reshape(1, D_MODEL), w_in.astype(BF16), w_out.astype(BF16)]
    specs += [_const_spec((1, D_MODEL)), _const_spec((D_MODEL, 2 * D_FF)), _const_spec((D_FF, D_MODEL))]
    return pl.pallas_call(
        functools.partial(_ffn_body, pre=pre is not None),
        grid=(bsz, seq // rows),
        in_specs=specs,
        out_specs=_token_spec(rows, seq, out_tm),
        out_shape=jax.ShapeDtypeStruct(out_shape, F32),
        scratch_shapes=[pltpu.VMEM((rows, D_FF), BF16)],
        compiler_params=_cparams(("parallel", "parallel")),
        name="ffn",
    )(*args)


def _s5_body(h_ref, g_ref, bmat_ref, cmat_ref, are_ref, aim_ref, d_ref, wglu_ref,
             o_ref, bu_ref, xb_ref, st_ref, y_ref, *, tc, nb):
    n_slab = D_MODEL // S5_SLAB
    half = S5_SLAB // S5_GROUP_CH * S5_STATE
    rows = tc * nb

    @pl.when(pl.program_id(0) == 0)
    def _():
        st_ref[...] = jnp.zeros_like(st_ref)

    h = h_ref[...].reshape(rows, D_MODEL)
    u = _rms(h, D_MODEL) * g_ref[...]
    ub = u.astype(BF16)

    def project(sl):
        bu_ref[sl % 2] = jnp.dot(ub[:, sl * S5_SLAB:(sl + 1) * S5_SLAB], bmat_ref[sl],
                                 preferred_element_type=F32)

    project(0)
    for sl in range(n_slab):
        if sl + 1 < n_slab:
            project(sl + 1)
        bu, xb = bu_ref.at[sl % 2], xb_ref.at[sl % 2]
        for wb in range(half // S5_LANES):
            re = slice(wb * S5_LANES, (wb + 1) * S5_LANES)
            im = slice(half + wb * S5_LANES, half + (wb + 1) * S5_LANES)
            ar = jnp.broadcast_to(are_ref[sl, :, re], (nb, S5_LANES))
            ai = jnp.broadcast_to(aim_ref[sl, :, re], (nb, S5_LANES))
            xr, xi = st_ref[sl, :, re], st_ref[sl, :, im]
            for t in range(tc):
                rs = slice(t * nb, (t + 1) * nb)
                xr, xi = ar * xr - ai * xi + bu[rs, re], ar * xi + ai * xr + bu[rs, im]
                xb[rs, re] = xr.astype(BF16)
                xb[rs, im] = xi.astype(BF16)
            st_ref[sl, :, re] = xr
            st_ref[sl, :, im] = xi
        y_ref[:, sl * S5_SLAB:(sl + 1) * S5_SLAB] = jnp.dot(
            xb_ref[sl % 2], cmat_ref[sl], preferred_element_type=F32)
    y = jax.nn.gelu(y_ref[...] + d_ref[...] * u)
    gate = jnp.dot(y.astype(BF16), wglu_ref[...], preferred_element_type=F32)
    o_ref[...] = (h + y * jax.nn.sigmoid(gate)).reshape(tc, nb, D_MODEL)


def _s5_params(a_re, a_im, log_dt, b_re, b_im, c_re, c_im):
    dt = jnp.exp(log_dt)[:, None]
    mag = jnp.exp(a_re * dt)
    ab_re = mag * jnp.cos(a_im * dt)
    ab_im = mag * jnp.sin(a_im * dt)
    den = a_re * a_re + a_im * a_im
    z_re = ((ab_re - 1.0) * a_re + ab_im * a_im) / den
    z_im = (ab_im * a_re - (ab_re - 1.0) * a_im) / den
    bb_re = z_re[..., None] * b_re - z_im[..., None] * b_im
    bb_im = z_re[..., None] * b_im + z_im[..., None] * b_re
    n_slab = D_MODEL // S5_SLAB
    gps = S5_GROUPS // n_slab
    eye = jnp.eye(gps, dtype=F32)

    def in_mat(bb):
        bb = bb.reshape(n_slab, gps, S5_STATE, S5_GROUP_CH)
        m = jnp.einsum('sgph,gk->sghkp', bb, eye)
        return m.reshape(n_slab, gps * S5_GROUP_CH, gps * S5_STATE)

    def out_mat(cc):
        cc = cc.reshape(n_slab, gps, S5_GROUP_CH, S5_STATE)
        m = jnp.einsum('sghp,gk->sgpkh', cc, eye)
        return m.reshape(n_slab, gps * S5_STATE, gps * S5_GROUP_CH)

    bmat = jnp.concatenate([in_mat(bb_re), in_mat(bb_im)], axis=2).astype(BF16)
    cmat = jnp.concatenate([out_mat(c_re), out_mat(-c_im)], axis=1).astype(BF16)
    are = ab_re.reshape(n_slab, 1, gps * S5_STATE)
    aim = ab_im.reshape(n_slab, 1, gps * S5_STATE)
    return bmat, cmat, are, aim


def _s5(h_tm, gain, a_re, a_im, log_dt, b_re, b_im, c_re, c_im, d_skip, w_glu, bsz, seq):
    tc = min(S5_TC, seq)
    n_slab = D_MODEL // S5_SLAB
    half = S5_SLAB // S5_GROUP_CH * S5_STATE
    bmat, cmat, are, aim = _s5_params(a_re, a_im, log_dt, b_re, b_im, c_re, c_im)
    blk = pl.BlockSpec((tc, bsz, D_MODEL), lambda i: (i, 0, 0))
    out = pl.pallas_call(
        functools.partial(_s5_body, tc=tc, nb=bsz),
        grid=(seq // tc,),
        in_specs=[blk,
                  _const_spec((1, D_MODEL)),
                  _const_spec((n_slab, S5_SLAB, 2 * half)),
                  _const_spec((n_slab, 2 * half, S5_SLAB)),
                  _const_spec((n_slab, 1, half)),
                  _const_spec((n_slab, 1, half)),
                  _const_spec((1, D_MODEL)),
                  _const_spec((D_MODEL, D_MODEL))],
        out_specs=blk,
        out_shape=jax.ShapeDtypeStruct((seq, bsz, D_MODEL), F32),
        scratch_shapes=[pltpu.VMEM((2, tc * bsz, 2 * half), F32),
                        pltpu.VMEM((2, tc * bsz, 2 * half), BF16),
                        pltpu.VMEM((n_slab, bsz, 2 * half), F32),
                        pltpu.VMEM((tc * bsz, D_MODEL), F32)],
        compiler_params=_cparams(("arbitrary",)),
        name="s5",
    )(h_tm.reshape(seq, bsz, D_MODEL), gain.reshape(1, D_MODEL), bmat, cmat, are, aim,
      d_skip.reshape(1, D_MODEL), w_glu.astype(BF16))
    return out.reshape(seq, bsz * D_MODEL)


def _kv_body(h_ref, g_ref, w_ref, gk_ref, kc_ref, vc_ref, ks_ref, vs_ref, kw_ref, vw_ref, *, n_real):
    s = pl.program_id(1)
    gw = N_KV_HEADS * HEAD_DIM
    pw = N_KV_HEADS * LANES
    tiles = ks_ref.shape[2]
    row = lax.broadcasted_iota(jnp.int32, (LANES, QT), 0)
    key_hi = (lax.broadcasted_iota(jnp.int32, (LANES, QT), 1) >= SEL_LEN).astype(jnp.int32)
    farb_rows = ((row >= FARB_LANE0) & (row < FARB_LANE0 + 3)).astype(F32)
    ones = jnp.ones((QT, LANES), BF16)

    @pl.when(s < n_real)
    def _():
        hn = (_rms(h_ref[...], D_MODEL) * g_ref[...]).astype(BF16)
        kv = jnp.dot(hn, w_ref[...], preferred_element_type=F32)
        kc_ref[...] = kv[:, 0:gw].astype(BF16)
        vc_ref[...] = kv[:, gw:2 * gw].astype(BF16)
        for br, (k_ref, v_ref) in enumerate(((ks_ref, vs_ref), (kw_ref, vw_ref))):
            for g in range(N_KV_HEADS):
                lo = 2 * gw + br * 2 * pw + g * LANES
                k = _rms(kv[:, lo:lo + LANES], HEAD_DIM) * gk_ref[br]
                v = kv[:, lo + pw:lo + pw + LANES].astype(BF16)
                for t in range(tiles):
                    extra = farb_rows
                    if br == 0:
                        blk = 2 * (tiles * s + t) + key_hi
                        extra = extra + (row - SEL_LANE0 == blk).astype(F32)
                    k_ref[0, g, t] = (k[t * QT:(t + 1) * QT].T + extra).astype(BF16)
                    v_ref[0, g, t] = jnp.concatenate([v[t * QT:(t + 1) * QT], ones], axis=1)

    @pl.when(s == n_real)
    def _():
        pad_k = jnp.broadcast_to((row == PAD_LANE).astype(BF16), (N_KV_HEADS, tiles, LANES, QT))
        pad_v = jnp.zeros((N_KV_HEADS, tiles, QT, 2 * LANES), BF16)
        for k_ref, v_ref in ((ks_ref, vs_ref), (kw_ref, vw_ref)):
            k_ref[0] = pad_k
            v_ref[0] = pad_v


def _pad_heads(w, n_heads):
    k = w.shape[0]
    w = w.reshape(k, n_heads, HEAD_DIM)
    return jnp.pad(w, ((0, 0), (0, 0), (0, LANES - HEAD_DIM))).reshape(k, n_heads * LANES)


def _kv_proj(h, kv_norm, w_kv, k_norm_slc, k_norm_win, bsz, seq):
    gw = N_KV_HEADS * HEAD_DIM
    tiles = PAD_TILES
    rows = tiles * QT
    n_real = seq // rows
    nt = seq // QT
    wk = w_kv.reshape(D_MODEL, 2 * N_BRANCH, gw)
    w = jnp.concatenate([wk[:, 0], wk[:, 1]] + [_pad_heads(wk[:, j], N_KV_HEADS) for j in (2, 3, 4, 5)],
                        axis=1).astype(BF16)
    gk = jnp.pad(jnp.stack([k_norm_slc, k_norm_win]), ((0, 0), (0, LANES - HEAD_DIM)))
    gk = gk.reshape(2, 1, LANES)
    n = bsz * seq
    real = lambda s: jnp.minimum(s, n_real - 1)
    tok = lambda c: pl.BlockSpec((rows, c), lambda b, s: (b * n_real + real(s), 0))
    tile_spec = lambda r, c: pl.BlockSpec((1, N_KV_HEADS, tiles, r, c), lambda b, s: (b, 0, s, 0, 0))
    dense = jax.ShapeDtypeStruct((n, gw), BF16)
    k_tiles = jax.ShapeDtypeStruct((bsz, N_KV_HEADS, nt + tiles, LANES, QT), BF16)
    v_tiles = jax.ShapeDtypeStruct((bsz, N_KV_HEADS, nt + tiles, QT, 2 * LANES), BF16)
    return pl.pallas_call(
        functools.partial(_kv_body, n_real=n_real),
        grid=(bsz, n_real + 1),
        in_specs=[tok(D_MODEL), _const_spec((1, D_MODEL)), _const_spec(w.shape),
                  _const_spec((2, 1, LANES))],
        out_specs=[tok(gw), tok(gw), tile_spec(LANES, QT), tile_spec(QT, 2 * LANES),
                   tile_spec(LANES, QT), tile_spec(QT, 2 * LANES)],
        out_shape=[dense, dense, k_tiles, v_tiles, k_tiles, v_tiles],
        compiler_params=_cparams(("parallel", "arbitrary")),
        name="kv_proj",
    )(h, kv_norm.reshape(1, D_MODEL), w, gk)


def _cmp_body(xk_ref, xv_ref, w1k_ref, w1v_ref, bk_ref, bv_ref, w2k_ref, w2v_ref, gk_ref,
              kc_ref, vc_ref, *, nh):
    for g in range(N_KV_HEADS):
        outs = []
        for x_ref, w1_ref, b_ref, w2_ref in ((xk_ref, w1k_ref, bk_ref, w2k_ref),
                                             (xv_ref, w1v_ref, bv_ref, w2v_ref)):
            ab = jnp.dot(x_ref[0, g], w1_ref[...], preferred_element_type=F32)
            hid = ab[:, :CMP_HIDDEN] + pltpu.roll(ab[:, CMP_HIDDEN:], nh - 1, 0) + b_ref[...]
            hid = jax.nn.gelu(hid).astype(BF16)
            outs.append(jnp.dot(hid, w2_ref[...], preferred_element_type=F32))
        k = _rms(outs[0], HEAD_DIM) * gk_ref[...]
        kc_ref[0, g] = k.T.astype(BF16)
        vc_ref[0, g] = outs[1].astype(BF16)


def _compress(kc_raw, vc_raw, k_norm_cmp, pos_k, pos_v, k_w1, k_w2, v_w1, v_w2, bsz, seq):
    nh = seq // CMP_STRIDE
    hb = CMP_STRIDE * HEAD_DIM

    def halfblocks(x):
        x = x.reshape(bsz, nh, CMP_STRIDE, N_KV_HEADS, HEAD_DIM)
        return x.transpose(0, 3, 1, 2, 4).reshape(bsz, N_KV_HEADS, nh, hb)

    def w1cat(w1):
        w = w1.reshape(2, hb, CMP_HIDDEN)
        return jnp.concatenate([w[0], w[1]], axis=1).astype(BF16)

    def w2pad(w2):
        return jnp.pad(w2, ((0, 0), (0, LANES - HEAD_DIM))).astype(BF16)

    bias = lambda pos, w1: jnp.einsum('ld,ldh->h', pos, w1,
                                      precision=lax.Precision.HIGHEST).reshape(1, CMP_HIDDEN)
    gk = jnp.pad(k_norm_cmp, (0, LANES - HEAD_DIM)).reshape(1, LANES)
    xspec = pl.BlockSpec((1, N_KV_HEADS, nh, hb), lambda b: (b, 0, 0, 0))
    ospec = pl.BlockSpec((1, N_KV_HEADS, LANES, nh), lambda b: (b, 0, 0, 0))
    vspec = pl.BlockSpec((1, N_KV_HEADS, nh, LANES), lambda b: (b, 0, 0, 0))
    return pl.pallas_call(
        functools.partial(_cmp_body, nh=nh),
        grid=(bsz,),
        in_specs=[xspec, xspec, _const_spec((hb, 2 * CMP_HIDDEN)), _const_spec((hb, 2 * CMP_HIDDEN)),
                  _const_spec((1, CMP_HIDDEN)), _const_spec((1, CMP_HIDDEN)),
                  _const_spec((CMP_HIDDEN, LANES)), _const_spec((CMP_HIDDEN, LANES)),
                  _const_spec((1, LANES))],
        out_specs=[ospec, vspec],
        out_shape=[jax.ShapeDtypeStruct((bsz, N_KV_HEADS, LANES, nh), BF16),
                   jax.ShapeDtypeStruct((bsz, N_KV_HEADS, nh, LANES), BF16)],
        compiler_params=_cparams(("parallel",)),
        name="kv_compress",
    )(halfblocks(kc_raw), halfblocks(vc_raw), w1cat(k_w1), w1cat(v_w1),
      bias(pos_k, k_w1), bias(pos_v, v_w1), w2pad(k_w2), w2pad(v_w2), gk)


def _qproj_body(h_ref, g_ref, w_ref, qsc_ref, qc_ref, q_ref, gate_ref):
    u = (_rms(h_ref[...], D_MODEL) * g_ref[...]).astype(BF16)
    qg = jnp.dot(u, w_ref[...], preferred_element_type=F32)
    for hh in range(N_HEADS):
        sl = slice(hh * LANES, (hh + 1) * LANES)
        q_ref[:, sl] = (_rms(qg[:, sl], HEAD_DIM) * qsc_ref[:, sl] + qc_ref[:, sl]).astype(BF16)
    gate_ref[...] = jax.nn.sigmoid(qg[:, N_HEADS * LANES:])


def _qproj(h, mix_gain, w_qg, q_norm, rel_bias, bsz, seq):
    rows = min(QP_ROWS, seq)
    n = bsz * seq
    nq = N_HEADS * HEAD_DIM
    wq = jnp.concatenate([_pad_heads(w_qg[:, :nq], N_HEADS),
                          jnp.pad(w_qg[:, nq:], ((0, 0), (0, LANES - N_BRANCH * N_HEADS)))],
                         axis=1).astype(BF16)
    qsc = jnp.tile(jnp.pad(q_norm * (ATTN_SCALE * LOG2E), (0, LANES - HEAD_DIM)), N_HEADS)
    qsc = qsc.reshape(1, N_HEADS * LANES)
    far = _split3(rel_bias[NUM_BUCKETS - 1] * LOG2E)
    qc = jnp.zeros((N_HEADS, LANES), F32)
    for t, term in enumerate(far):
        qc = qc.at[:, FARB_LANE0 + t].set(term.astype(F32))
    qc = qc.at[:, PAD_LANE].set(NEG_INF).reshape(1, N_HEADS * LANES)
    tok = lambda c: pl.BlockSpec((rows, c), lambda i: (i, 0))
    return pl.pallas_call(
        _qproj_body,
        grid=(n // rows,),
        in_specs=[tok(D_MODEL), _const_spec((1, D_MODEL)), _const_spec(wq.shape),
                  _const_spec((1, N_HEADS * LANES)), _const_spec((1, N_HEADS * LANES))],
        out_specs=[tok(N_HEADS * LANES), tok(LANES)],
        out_shape=[jax.ShapeDtypeStruct((n, N_HEADS * LANES), BF16),
                   jax.ShapeDtypeStruct((n, LANES), F32)],
        compiler_params=_cparams(("parallel",)),
        name="q_proj",
    )(h, mix_gain.reshape(1, D_MODEL), wq, qsc, qc)


def _attn_body(q_ref, gate_ref, kc_ref, vc_ref, ks_ref, vs_ref, kw_ref, vw_ref,
               ctab_ref, wtab_ref, ovlt_ref, gexp_ref,
               o_ref, s_scr, sn_scr, m_scr, qf_scr, acc_scr, out_scr, *, n_sel, n_top, nt):
    i = pl.program_id(1)
    rq = Q_PER_KV * QT
    n_win = WIN // QT + 1
    lane = lax.broadcasted_iota(jnp.int32, (QT, LANES), 1)
    sel_lane = (lane >= SEL_LANE0) & (lane < SEL_LANE0 + n_sel)
    far_cut = sel_lane & (lane - SEL_LANE0 >= 2 * (i - 1))
    sel_lane4 = jnp.concatenate([sel_lane] * Q_PER_KV, axis=0)
    blk = lax.broadcasted_iota(jnp.int32, (n_sel, QT), 0)
    blkf = blk.astype(F32)
    qpos = i * QT + lax.broadcasted_iota(jnp.int32, (n_sel, QT), 1)
    cur = lax.shift_right_arithmetic(qpos, int(math.log2(SEL_LEN)))
    forced = (blk == 0) | (blk == cur) | (blk == cur - 1)
    causal = blk * SEL_LEN <= qpos
    tile = lambda t: jnp.where(t < 0, nt, t)

    def gate_of(b):
        return sum(jnp.dot(t, gexp_ref[b], preferred_element_type=F32) for t in _split3(gate_ref[...])[:2])

    gexp = {0: gate_of(0), 2: gate_of(2)}

    def merged(o, g, pr):
        return o[2 * pr * QT:(2 * pr + 1) * QT] + pltpu.roll(o[(2 * pr + 1) * QT:(2 * pr + 2) * QT], HEAD_DIM, 1)

    def normalise(pv):
        return pv[:, :LANES] * (1.0 / pv[:, LANES:])

    def tab(g, lo, width):
        return wtab_ref[Q_PER_KV * g:Q_PER_KV * (g + 1), :, lo:lo + width].reshape(rq, width)

    def queries(g):
        return jnp.concatenate([q_ref[:, (Q_PER_KV * g + r) * LANES:(Q_PER_KV * g + r + 1) * LANES]
                                for r in range(Q_PER_KV)], axis=0)

    cmp_p, pv_cs, picked, win = {}, {}, {}, {}
    w_tiles = [tile(i - (n_win - 1) + j) for j in range(n_win)]

    def cmp_softmax(g):
        sc = jnp.dot(queries(g), kc_ref[0, g], preferred_element_type=F32)
        sc = sc + ctab_ref[Q_PER_KV * g:Q_PER_KV * (g + 1)].reshape(rq, sc.shape[-1])
        e = jnp.exp2(sc - jnp.max(sc, axis=-1, keepdims=True))
        row_ok = i * QT + lax.broadcasted_iota(jnp.int32, (QT, sc.shape[-1]), 0) >= CMP_LEN - 1
        cmp_p[g] = jnp.where(jnp.concatenate([row_ok] * Q_PER_KV, axis=0),
                             e * (1.0 / jnp.sum(e, axis=-1, keepdims=True)), 0.0)

    def cmp_out(g):
        p = cmp_p[g]
        pv_cs[g] = jnp.dot(p.astype(BF16), vc_ref[0, g], preferred_element_type=F32)
        psum = p[0:QT] + p[QT:2 * QT] + p[2 * QT:3 * QT] + p[3 * QT:4 * QT]
        ps_t = sum(jnp.dot(ovlt_ref[...], t, preferred_element_type=F32) for t in _split3(psum.T))
        cmp_p[g] = jnp.where(forced, BIG, jnp.where(causal, ps_t[SEL_LANE0:SEL_LANE0 + n_sel], -BIG))

    def select(g):
        score = cmp_p[g]
        pick = jnp.zeros((n_sel, QT), F32)
        for _ in range(n_top):
            mx = jnp.max(score, axis=0, keepdims=True)
            first = jnp.min(jnp.where(score == mx, blkf, float(LANES)), axis=0, keepdims=True)
            hit = blkf == first
            pick = jnp.where(hit, 1.0, pick)
            score = jnp.where(hit, -jnp.inf, score)
        picked[g] = pick

    def win_logits(g):
        k_w = jnp.concatenate([kw_ref[0, g, t] for t in w_tiles], axis=1)
        s_w = jnp.dot(queries(g), k_w, preferred_element_type=F32)
        win[g] = [s_w[:, :QT] + tab(g, 0, QT), s_w[:, QT:(n_win - 2) * QT],
                  s_w[:, (n_win - 2) * QT:] + tab(g, QT, 2 * QT)]

    def win_softmax(g):
        mw = jnp.max(jnp.concatenate(win[g], axis=1), axis=-1, keepdims=True)
        win[g] = jnp.concatenate([jnp.exp2(t - mw).astype(BF16) for t in win[g]], axis=1)

    def win_out(g):
        v_w = jnp.concatenate([vw_ref[0, g, t] for t in w_tiles], axis=0)
        o_w = normalise(jnp.dot(win[g], v_w, preferred_element_type=F32))
        for pr in range(Q_PER_KV // 2):
            col = (Q_PER_KV * g // 2 + pr) * LANES
            out_scr[:, col:col + LANES] = (gexp[0][:, col:col + LANES] * merged(pv_cs[g], g, pr)
                                           + gexp[2][:, col:col + LANES] * merged(o_w, g, pr))

    def near_logits(g):
        q = queries(g)
        selb_t = jnp.where(picked[g] == 0.0, NEG_INF, 0.0)
        selb = jnp.concatenate([jnp.zeros((SEL_LANE0, QT), F32), selb_t,
                                jnp.zeros((LANES - SEL_LANE0 - n_sel, QT), F32)], axis=0).T
        selb_far = jnp.where(far_cut, NEG_INF, selb)
        q_near = jnp.where(sel_lane4, jnp.concatenate([selb.astype(BF16)] * Q_PER_KV, axis=0), q)
        qf_scr[g] = jnp.where(sel_lane4, jnp.concatenate([selb_far.astype(BF16)] * Q_PER_KV, axis=0), q)
        k_n = jnp.concatenate([ks_ref[0, g, tile(i - 1)], ks_ref[0, g, i]], axis=1)
        s_n = jnp.dot(q_near, k_n, preferred_element_type=F32) + tab(g, QT, 2 * QT)
        sn_scr[g] = s_n
        m_scr[g] = jnp.maximum(s_n[:, :QT], s_n[:, QT:])

    stages = (cmp_softmax, win_logits, cmp_out, win_softmax, select, win_out, near_logits)
    for step in range(N_KV_HEADS + len(stages) - 1):
        for k, stage in enumerate(stages):
            if 0 <= step - k < N_KV_HEADS:
                stage(step - k)

    n_chunks = (jnp.maximum(i - 1, 0) + FAR_TILES - 1) // FAR_TILES

    def far_logits(c, carry):
        for g in range(N_KV_HEADS):
            k_f = jnp.concatenate([ks_ref[0, g, FAR_TILES * c + t] for t in range(FAR_TILES)], axis=1)
            s = jnp.dot(qf_scr[g], k_f, preferred_element_type=F32)
            s_scr[g, c] = s
            m = m_scr[g]
            for t in range(FAR_TILES):
                m = jnp.maximum(m, s[:, t * QT:(t + 1) * QT])
            m_scr[g] = m
        return carry

    lax.fori_loop(0, n_chunks, far_logits, 0)

    for g in range(N_KV_HEADS):
        m = jnp.max(m_scr[g], axis=-1, keepdims=True)
        m_scr[g] = jnp.broadcast_to(m, (rq, QT))
        p_n = jnp.exp2(sn_scr[g] - m).astype(BF16)
        v_n = jnp.concatenate([vs_ref[0, g, tile(i - 1)], vs_ref[0, g, i]], axis=0)
        acc_scr[g] = jnp.dot(p_n, v_n, preferred_element_type=F32)

    def far_pv(c, carry):
        for g in range(N_KV_HEADS):
            m = jnp.concatenate([m_scr[g]] * FAR_TILES, axis=1)
            p = jnp.exp2(s_scr[g, c] - m).astype(BF16)
            v_f = jnp.concatenate([vs_ref[0, g, FAR_TILES * c + t] for t in range(FAR_TILES)], axis=0)
            acc_scr[g] += jnp.dot(p, v_f, preferred_element_type=F32)
        return carry

    lax.fori_loop(0, n_chunks, far_pv, 0)

    gexp[1] = gate_of(1)
    for g in range(N_KV_HEADS):
        o_s = normalise(acc_scr[g])
        for pr in range(Q_PER_KV // 2):
            col = (Q_PER_KV * g // 2 + pr) * LANES
            o = out_scr[:, col:col + LANES] + gexp[1][:, col:col + LANES] * merged(o_s, g, pr)
            o_ref[:, col:col + LANES] = o.astype(BF16)


def _rel_bucket(dist):
    n = jnp.maximum(dist, 0)
    max_exact = NUM_BUCKETS // 2
    logv = jnp.log(jnp.maximum(n, 1).astype(F32) / max_exact) / math.log(MAX_DISTANCE / max_exact)
    large = jnp.minimum(max_exact + (logv * (NUM_BUCKETS - max_exact)).astype(jnp.int32), NUM_BUCKETS - 1)
    return jnp.where(n < max_exact, n, large)


def _bias_tables(rel_bias, seq):
    nt = seq // QT
    nc = seq // CMP_STRIDE
    cpt = QT // CMP_STRIDE
    rb = rel_bias * LOG2E

    def f(dist):
        onehot = (_rel_bucket(dist)[..., None] == jnp.arange(NUM_BUCKETS)).astype(F32)
        return jnp.einsum('...k,kh->h...', onehot, rb, precision=lax.Precision.HIGHEST)

    c_rel = jnp.arange(-cpt * (nt - 1), nc)
    dc = jnp.arange(QT)[:, None] - (c_rel * CMP_STRIDE + CMP_LEN - 1)[None, :]
    t0 = jnp.where((dc >= 0)[None], f(dc), NEG_INF)
    ctab = jnp.stack([t0[:, :, cpt * (nt - 1 - i):cpt * (nt - 1 - i) + nc] for i in range(nt)], axis=1)
    ctab = ctab.reshape(N_HEADS, seq, nc)
    far = rb[NUM_BUCKETS - 1].reshape(N_HEADS, 1, 1)
    dq = jnp.arange(QT)[:, None] - jnp.arange(QT)[None, :]
    d0, d3, d4 = WIN + dq, QT + dq, dq
    wtab = jnp.concatenate([jnp.where((d0 < WIN)[None], f(d0) - far, NEG_INF), f(d3) - far,
                            jnp.where((d4 >= 0)[None], f(d4) - far, NEG_INF)], axis=2)
    return ctab, wtab


def _overlap_matrix_t(seq):
    nc = seq // CMP_STRIDE
    n_sel = seq // SEL_LEN
    c_start = jnp.arange(nc) * CMP_STRIDE
    j_start = jnp.arange(n_sel) * SEL_LEN
    ov = jnp.clip(jnp.minimum(c_start[None, :] + CMP_LEN, j_start[:, None] + SEL_LEN)
                  - jnp.maximum(c_start[None, :], j_start[:, None]), 0, None)
    ov = ov.astype(F32) / CMP_LEN
    return jnp.pad(ov, ((SEL_LANE0, LANES - SEL_LANE0 - n_sel), (0, 0))).astype(BF16)


def _gate_expansion():
    col = jnp.arange(LANES).reshape(1, LANES, 1)
    head = (jnp.arange(D_MODEL) // HEAD_DIM).reshape(1, 1, D_MODEL)
    br = jnp.arange(N_BRANCH).reshape(N_BRANCH, 1, 1)
    return (col == head * N_BRANCH + br).astype(BF16)


def _attention(q, gates, kv, rel_bias, bsz, seq):
    kc, vc, ks, vs, kw, vw = kv
    nt = seq // QT
    nc = seq // CMP_STRIDE
    n_sel = seq // SEL_LEN
    n_top = min(SEL_TOPN, n_sel)
    n_chunk = max(-(-(nt - 2) // FAR_TILES), 1)
    assert n_chunk * FAR_TILES <= nt
    nts = nt + PAD_TILES
    ctab, wtab = _bias_tables(rel_bias, seq)
    rq = Q_PER_KV * QT
    per_batch = lambda shape: pl.BlockSpec((1,) + shape, lambda b, i: (b,) + (0,) * len(shape),
                                           pipeline_mode=pl.Buffered(1))
    tok = lambda c: pl.BlockSpec((QT, c), lambda b, i: (b * nt + i, 0))
    return pl.pallas_call(
        functools.partial(_attn_body, n_sel=n_sel, n_top=n_top, nt=nt),
        grid=(bsz, nt),
        in_specs=[tok(N_HEADS * LANES), tok(LANES),
                  per_batch((N_KV_HEADS, LANES, nc)),
                  per_batch((N_KV_HEADS, nc, LANES)),
                  per_batch((N_KV_HEADS, nts, LANES, QT)),
                  per_batch((N_KV_HEADS, nts, QT, 2 * LANES)),
                  per_batch((N_KV_HEADS, nts, LANES, QT)),
                  per_batch((N_KV_HEADS, nts, QT, 2 * LANES)),
                  pl.BlockSpec((N_HEADS, QT, nc), lambda b, i: (0, i, 0)),
                  _const_spec((N_HEADS, QT, 3 * QT)),
                  _const_spec((LANES, nc)),
                  _const_spec((N_BRANCH, LANES, D_MODEL))],
        out_specs=tok(D_MODEL),
        out_shape=jax.ShapeDtypeStruct((bsz * seq, D_MODEL), BF16),
        scratch_shapes=[pltpu.VMEM((N_KV_HEADS, n_chunk, rq, FAR_TILES * QT), F32),
                        pltpu.VMEM((N_KV_HEADS, rq, 2 * QT), F32),
                        pltpu.VMEM((N_KV_HEADS, rq, QT), F32),
                        pltpu.VMEM((N_KV_HEADS, rq, LANES), BF16),
                        pltpu.VMEM((N_KV_HEADS, rq, 2 * LANES), F32),
                        pltpu.VMEM((QT, D_MODEL), F32)],
        compiler_params=_cparams(("parallel", "arbitrary")),
        name="nsa_attn",
    )(q, gates, kc, vc, ks, vs, kw, vw, ctab, wtab, _overlap_matrix_t(seq), _gate_expansion())


def kernel(x, rel_bias, ffn1_norm, ffn1_w_in, ffn1_w_out, mix_norm, ffn2_norm, ffn2_w_in, ffn2_w_out,
           s5_a_re, s5_a_im, s5_log_dt, s5_b_re, s5_b_im, s5_c_re, s5_c_im, s5_d, s5_w_glu,
           kv_norm, w_kv, k_norm_cmp, k_norm_slc, k_norm_win, cmp_pos_k, cmp_pos_v,
           cmp_k_w1, cmp_k_w2, cmp_v_w1, cmp_v_w2, w_qg, q_norm, w_o):
    bsz, seq, _ = x.shape
    depth = ffn1_norm.shape[0]
    n_a = s5_a_re.shape[0]
    h = x.reshape(bsz * seq, D_MODEL)
    kv = None
    for layer in range(depth):
        s5_layer = layer < n_a
        h = _ffn(h, ffn1_norm[layer], ffn1_w_in[layer], ffn1_w_out[layer], bsz, seq,
                 in_tm=False, out_tm=s5_layer)
        pre = None
        if s5_layer:
            a = layer
            h = _s5(h, mix_norm[layer], s5_a_re[a], s5_a_im[a], s5_log_dt[a], s5_b_re[a], s5_b_im[a],
                    s5_c_re[a], s5_c_im[a], s5_d[a], s5_w_glu[a], bsz, seq)
        else:
            b = layer - n_a
            q, gates = _qproj(h, mix_norm[layer], w_qg[b], q_norm[b], rel_bias, bsz, seq)
            pre = (_attention(q, gates, kv, rel_bias, bsz, seq), w_o[b])
        h = _ffn(h, ffn2_norm[layer], ffn2_w_in[layer], ffn2_w_out[layer], bsz, seq,
                 in_tm=s5_layer, out_tm=False, pre=pre)
        if layer == n_a - 1:
            kc_raw, vc_raw, ks, vs, kw, vw = _kv_proj(h, kv_norm, w_kv, k_norm_slc, k_norm_win, bsz, seq)
            kc, vc = _compress(kc_raw, vc_raw, k_norm_cmp, cmp_pos_k, cmp_pos_v,
                               cmp_k_w1, cmp_k_w2, cmp_v_w1, cmp_v_w2, bsz, seq)
            kv = (kc, vc, ks, vs, kw, vw)
    return h.reshape(bsz, seq, D_MODEL)
```

```python
import functools
import math

import jax
import jax.numpy as jnp
from jax import lax
from jax.experimental import pallas as pl
from jax.experimental.pallas import tpu as pltpu

F32 = jnp.float32
BF16 = jnp.bfloat16

D_MODEL = 1024
D_FF = 2816
RMS_EPS = 1e-6
S5_GROUP_CH = 16
S5_GROUPS = D_MODEL // S5_GROUP_CH
S5_STATE = 64
N_HEADS = 16
HEAD_DIM = 64
N_KV_HEADS = 4
Q_PER_KV = N_HEADS // N_KV_HEADS
CMP_LEN = 32
CMP_STRIDE = 16
CMP_HIDDEN = 2 * HEAD_DIM
SEL_LEN = 64
SEL_TOPN = 8
WIN = 512
N_BRANCH = 3
ATTN_SCALE = HEAD_DIM ** -0.5
NUM_BUCKETS = 32
MAX_DISTANCE = 128
NEG_INF = -1e30
BIG = 1e9

LANES = 128
MXU_DIM = 256
VMEM_LIMIT = 56 * 1024 * 1024

FFN_ROWS = 512
FF_CHUNK = MXU_DIM
S5_TC = 16
S5_SLAB = MXU_DIM
S5_LANES = 256
QP_ROWS = 512
QT = 128
FAR_TILES = 4
PAD_TILES = WIN // QT
LOG2E = math.log2(math.e)

SEL_LANE0 = HEAD_DIM
FARB_LANE0 = 96
PAD_LANE = 99


def _rms(x, n):
    ss = jnp.sum(x * x, axis=-1, keepdims=True)
    return x * lax.rsqrt(ss * (1.0 / n) + RMS_EPS)


def _cparams(sem):
    return pltpu.CompilerParams(dimension_semantics=sem, vmem_limit_bytes=VMEM_LIMIT)


def _const_spec(shape):
    nd = len(shape)
    return pl.BlockSpec(shape, lambda *_: (0,) * nd, pipeline_mode=pl.Buffered(1))


def _split3(x):
    x1 = x.astype(BF16)
    r1 = x - x1.astype(F32)
    x2 = r1.astype(BF16)
    x3 = (r1 - x2.astype(F32)).astype(BF16)
    return x1, x2, x3


def _ffn_body(*refs, pre):
    if pre:
        x_ref, po_ref, pw_ref, g_ref, win_ref, wout_ref, o_ref, act_ref = refs
        x = x_ref[...] + jnp.dot(po_ref[...], pw_ref[...], preferred_element_type=F32)
    else:
        x_ref, g_ref, win_ref, wout_ref, o_ref, act_ref = refs
        x = x_ref[...]
    xn = (_rms(x, D_MODEL) * g_ref[...]).astype(BF16)
    for c in range(D_FF // FF_CHUNK):
        lo = c * FF_CHUNK
        a = jnp.dot(xn, win_ref[:, lo:lo + FF_CHUNK], preferred_element_type=F32)
        b = jnp.dot(xn, win_ref[:, D_FF + lo:D_FF + lo + FF_CHUNK], preferred_element_type=F32)
        act_ref[:, lo:lo + FF_CHUNK] = (a * jax.nn.sigmoid(a) * b).astype(BF16)
    y = jnp.dot(act_ref[...], wout_ref[...], preferred_element_type=F32)
    o_ref[...] = x + 0.5 * y


def _token_spec(rows, seq, time_major):
    nt = seq // rows
    if time_major:
        return pl.BlockSpec((rows, D_MODEL), lambda b, t: (t, b))
    return pl.BlockSpec((rows, D_MODEL), lambda b, t: (b * nt + t, 0))


def _ffn(h, gain, w_in, w_out, bsz, seq, in_tm, out_tm, pre=None):
    rows = min(FFN_ROWS, seq)
    out_shape = (seq, bsz * D_MODEL) if out_tm else (bsz * seq, D_MODEL)
    args, specs = [h], [_token_spec(rows, seq, in_tm)]
    if pre is not None:
        args += [pre[0], pre[1].astype(BF16)]
        specs += [_token_spec(rows, seq, False), _const_spec((D_MODEL, D_MODEL))]
    args += [gain.reshape(1, D_MODEL), w_in.astype(BF16), w_out.astype(BF16)]
    specs += [_const_spec((1, D_MODEL)), _const_spec((D_MODEL, 2 * D_FF)), _const_spec((D_FF, D_MODEL))]
    return pl.pallas_call(
        functools.partial(_ffn_body, pre=pre is not None),
        grid=(bsz, seq // rows),
        in_specs=specs,
        out_specs=_token_spec(rows, seq, out_tm),
        out_shape=jax.ShapeDtypeStruct(out_shape, F32),
        scratch_shapes=[pltpu.VMEM((rows, D_FF), BF16)],
        compiler_params=_cparams(("parallel", "parallel")),
        name="ffn",
    )(*args)


def _s5_body(h_ref, g_ref, bmat_ref, cmat_ref, are_ref, aim_ref, d_ref, wglu_ref,
             o_ref, bu_ref, xb_ref, st_ref, y_ref, *, tc, nb):
    n_slab = D_MODEL // S5_SLAB
    half = S5_SLAB // S5_GROUP_CH * S5_STATE
    rows = tc * nb

    @pl.when(pl.program_id(0) == 0)
    def _():
        st_ref[...] = jnp.zeros_like(st_ref)

    h = h_ref[...].reshape(rows, D_MODEL)
    u = _rms(h, D_MODEL) * g_ref[...]
    ub = u.astype(BF16)

    def project(sl):
        bu_ref[sl % 2] = jnp.dot(ub[:, sl * S5_SLAB:(sl + 1) * S5_SLAB], bmat_ref[sl],
                                 preferred_element_type=F32)

    project(0)
    for sl in range(n_slab):
        if sl + 1 < n_slab:
            project(sl + 1)
        bu, xb = bu_ref.at[sl % 2], xb_ref.at[sl % 2]
        for wb in range(half // S5_LANES):
            re = slice(wb * S5_LANES, (wb + 1) * S5_LANES)
            im = slice(half + wb * S5_LANES, half + (wb + 1) * S5_LANES)
            ar = jnp.broadcast_to(are_ref[sl, :, re], (nb, S5_LANES))
            ai = jnp.broadcast_to(aim_ref[sl, :, re], (nb, S5_LANES))
            xr, xi = st_ref[sl, :, re], st_ref[sl, :, im]
            for t in range(tc):
                rs = slice(t * nb, (t + 1) * nb)
                xr, xi = ar * xr - ai * xi + bu[rs, re], ar * xi + ai * xr + bu[rs, im]
                xb[rs, re] = xr.astype(BF16)
                xb[rs, im] = xi.astype(BF16)
            st_ref[sl, :, re] = xr
            st_ref[sl, :, im] = xi
        y_ref[:, sl * S5_SLAB:(sl + 1) * S5_SLAB] = jnp.dot(
            xb_ref[sl % 2], cmat_ref[sl], preferred_element_type=F32)
    y = jax.nn.gelu(y_ref[...] + d_ref[...] * u)
    gate = jnp.dot(y.astype(BF16), wglu_ref[...], preferred_element_type=F32)
    o_ref[...] = (h + y * jax.nn.sigmoid(gate)).reshape(tc, nb, D_MODEL)


def _s5_params(a_re, a_im, log_dt, b_re, b_im, c_re, c_im):
    dt = jnp.exp(log_dt)[:, None]
    mag = jnp.exp(a_re * dt)
    ab_re = mag * jnp.cos(a_im * dt)
    ab_im = mag * jnp.sin(a_im * dt)
    den = a_re * a_re + a_im * a_im
    z_re = ((ab_re - 1.0) * a_re + ab_im * a_im) / den
    z_im = (ab_im * a_re - (ab_re - 1.0) * a_im) / den
    bb_re = z_re[..., None] * b_re - z_im[..., None] * b_im
    bb_im = z_re[..., None] * b_im + z_im[..., None] * b_re
    n_slab = D_MODEL // S5_SLAB
    gps = S5_GROUPS // n_slab
    eye = jnp.eye(gps, dtype=F32)

    def in_mat(bb):
        bb = bb.reshape(n_slab, gps, S5_STATE, S5_GROUP_CH)
        m = jnp.einsum('sgph,gk->sghkp', bb, eye)
        return m.reshape(n_slab, gps * S5_GROUP_CH, gps * S5_STATE)

    def out_mat(cc):
        cc = cc.reshape(n_slab, gps, S5_GROUP_CH, S5_STATE)
        m = jnp.einsum('sghp,gk->sgpkh', cc, eye)
        return m.reshape(n_slab, gps * S5_STATE, gps * S5_GROUP_CH)

    bmat = jnp.concatenate([in_mat(bb_re), in_mat(bb_im)], axis=2).astype(BF16)
    cmat = jnp.concatenate([out_mat(c_re), out_mat(-c_im)], axis=1).astype(BF16)
    are = ab_re.reshape(n_slab, 1, gps * S5_STATE)
    aim = ab_im.reshape(n_slab, 1, gps * S5_STATE)
    return bmat, cmat, are, aim


def _s5(h_tm, gain, a_re, a_im, log_dt, b_re, b_im, c_re, c_im, d_skip, w_glu, bsz, seq):
    tc = min(S5_TC, seq)
    n_slab = D_MODEL // S5_SLAB
    half = S5_SLAB // S5_GROUP_CH * S5_STATE
    bmat, cmat, are, aim = _s5_params(a_re, a_im, log_dt, b_re, b_im, c_re, c_im)
    blk = pl.BlockSpec((tc, bsz, D_MODEL), lambda i: (i, 0, 0))
    out = pl.pallas_call(
        functools.partial(_s5_body, tc=tc, nb=bsz),
        grid=(seq // tc,),
        in_specs=[blk,
                  _const_spec((1, D_MODEL)),
                  _const_spec((n_slab, S5_SLAB, 2 * half)),
                  _const_spec((n_slab, 2 * half, S5_SLAB)),
                  _const_spec((n_slab, 1, half)),
                  _const_spec((n_slab, 1, half)),
                  _const_spec((1, D_MODEL)),
                  _const_spec((D_MODEL, D_MODEL))],
        out_specs=blk,
        out_shape=jax.ShapeDtypeStruct((seq, bsz, D_MODEL), F32),
        scratch_shapes=[pltpu.VMEM((2, tc * bsz, 2 * half), F32),
                        pltpu.VMEM((2, tc * bsz, 2 * half), BF16),
                        pltpu.VMEM((n_slab, bsz, 2 * half), F32),
                        pltpu.VMEM((tc * bsz, D_MODEL), F32)],
        compiler_params=_cparams(("arbitrary",)),
        name="s5",
    )(h_tm.reshape(seq, bsz, D_MODEL), gain.reshape(1, D_MODEL), bmat, cmat, are, aim,
      d_skip.reshape(1, D_MODEL), w_glu.astype(BF16))
    return out.reshape(seq, bsz * D_MODEL)


def _kv_body(h_ref, g_ref, w_ref, gk_ref, kc_ref, vc_ref, ks_ref, vs_ref, kw_ref, vw_ref, *, n_real):
    s = pl.program_id(1)
    gw = N_KV_HEADS * HEAD_DIM
    pw = N_KV_HEADS * LANES
    tiles = ks_ref.shape[2]
    row = lax.broadcasted_iota(jnp.int32, (LANES, QT), 0)
    key_hi = (lax.broadcasted_iota(jnp.int32, (LANES, QT), 1) >= SEL_LEN).astype(jnp.int32)
    farb_rows = ((row >= FARB_LANE0) & (row < FARB_LANE0 + 3)).astype(F32)
    ones = jnp.ones((QT, LANES), BF16)

    @pl.when(s < n_real)
    def _():
        hn = (_rms(h_ref[...], D_MODEL) * g_ref[...]).astype(BF16)
        kv = jnp.dot(hn, w_ref[...], preferred_element_type=F32)
        kc_ref[...] = kv[:, 0:gw].astype(BF16)
        vc_ref[...] = kv[:, gw:2 * gw].astype(BF16)
        for br, (k_ref, v_ref) in enumerate(((ks_ref, vs_ref), (kw_ref, vw_ref))):
            for g in range(N_KV_HEADS):
                lo = 2 * gw + br * 2 * pw + g * LANES
                k = _rms(kv[:, lo:lo + LANES], HEAD_DIM) * gk_ref[br]
                v = kv[:, lo + pw:lo + pw + LANES].astype(BF16)
                for t in range(tiles):
                    extra = farb_rows
                    if br == 0:
                        blk = 2 * (tiles * s + t) + key_hi
                        extra = extra + (row - SEL_LANE0 == blk).astype(F32)
                    k_ref[0, g, t] = (k[t * QT:(t + 1) * QT].T + extra).astype(BF16)
                    v_ref[0, g, t] = jnp.concatenate([v[t * QT:(t + 1) * QT], ones], axis=1)

    @pl.when(s == n_real)
    def _():
        pad_k = jnp.broadcast_to((row == PAD_LANE).astype(BF16), (N_KV_HEADS, tiles, LANES, QT))
        pad_v = jnp.zeros((N_KV_HEADS, tiles, QT, 2 * LANES), BF16)
        for k_ref, v_ref in ((ks_ref, vs_ref), (kw_ref, vw_ref)):
            k_ref[0] = pad_k
            v_ref[0] = pad_v


def _pad_heads(w, n_heads):
    k = w.shape[0]
    w = w.reshape(k, n_heads, HEAD_DIM)
    return jnp.pad(w, ((0, 0), (0, 0), (0, LANES - HEAD_DIM))).reshape(k, n_heads * LANES)


def _kv_proj(h, kv_norm, w_kv, k_norm_slc, k_norm_win, bsz, seq):
    gw = N_KV_HEADS * HEAD_DIM
    tiles = PAD_TILES
    rows = tiles * QT
    n_real = seq // rows
    nt = seq // QT
    wk = w_kv.reshape(D_MODEL, 2 * N_BRANCH, gw)
    w = jnp.concatenate([wk[:, 0], wk[:, 1]] + [_pad_heads(wk[:, j], N_KV_HEADS) for j in (2, 3, 4, 5)],
                        axis=1).astype(BF16)
    gk = jnp.pad(jnp.stack([k_norm_slc, k_norm_win]), ((0, 0), (0, LANES - HEAD_DIM)))
    gk = gk.reshape(2, 1, LANES)
    n = bsz * seq
    real = lambda s: jnp.minimum(s, n_real - 1)
    tok = lambda c: pl.BlockSpec((rows, c), lambda b, s: (b * n_real + real(s), 0))
    tile_spec = lambda r, c: pl.BlockSpec((1, N_KV_HEADS, tiles, r, c), lambda b, s: (b, 0, s, 0, 0))
    dense = jax.ShapeDtypeStruct((n, gw), BF16)
    k_tiles = jax.ShapeDtypeStruct((bsz, N_KV_HEADS, nt + tiles, LANES, QT), BF16)
    v_tiles = jax.ShapeDtypeStruct((bsz, N_KV_HEADS, nt + tiles, QT, 2 * LANES), BF16)
    return pl.pallas_call(
        functools.partial(_kv_body, n_real=n_real),
        grid=(bsz, n_real + 1),
        in_specs=[tok(D_MODEL), _const_spec((1, D_MODEL)), _const_spec(w.shape),
                  _const_spec((2, 1, LANES))],
        out_specs=[tok(gw), tok(gw), tile_spec(LANES, QT), tile_spec(QT, 2 * LANES),
                   tile_spec(LANES, QT), tile_spec(QT, 2 * LANES)],
        out_shape=[dense, dense, k_tiles, v_tiles, k_tiles, v_tiles],
        compiler_params=_cparams(("parallel", "arbitrary")),
        name="kv_proj",
    )(h, kv_norm.reshape(1, D_MODEL), w, gk)


def _cmp_body(xk_ref, xv_ref, w1k_ref, w1v_ref, bk_ref, bv_ref, w2k_ref, w2v_ref, gk_ref,
              kc_ref, vc_ref, *, nh):
    ab_k = jnp.dot(xk_ref[0], w1k_ref[...], preferred_element_type=F32)
    ab_v = jnp.dot(xv_ref[0], w1v_ref[...], preferred_element_type=F32)
    for g in range(N_KV_HEADS):
        outs = []
        for ab_all, b_ref, w2_ref in ((ab_k, bk_ref, w2k_ref), (ab_v, bv_ref, w2v_ref)):
            ab = ab_all[:, g * 2 * CMP_HIDDEN:(g + 1) * 2 * CMP_HIDDEN]
            hid = ab[:, :CMP_HIDDEN] + pltpu.roll(ab[:, CMP_HIDDEN:], nh - 1, 0) + b_ref[...]
            hid = jax.nn.gelu(hid).astype(BF16)
            outs.append(jnp.dot(hid, w2_ref[...], preferred_element_type=F32))
        k = _rms(outs[0], HEAD_DIM) * gk_ref[...]
        kc_ref[0, g] = k.T.astype(BF16)
        vc_ref[0, g] = outs[1].astype(BF16)


def _compress(kc_raw, vc_raw, k_norm_cmp, pos_k, pos_v, k_w1, k_w2, v_w1, v_w2, bsz, seq):
    nh = seq // CMP_STRIDE
    hb = CMP_STRIDE * N_KV_HEADS * HEAD_DIM
    eye = jnp.eye(N_KV_HEADS, dtype=F32)

    def halfblocks(x):
        return x.reshape(bsz, nh, hb)

    def w1cat(w1):
        w = w1.reshape(2, CMP_STRIDE, HEAD_DIM, CMP_HIDDEN)
        w = jnp.concatenate([w[0], w[1]], axis=-1)
        w = jnp.einsum('ldj,gh->lgdhj', w, eye)
        return w.reshape(hb, N_KV_HEADS * 2 * CMP_HIDDEN).astype(BF16)

    def w2pad(w2):
        return jnp.pad(w2, ((0, 0), (0, LANES - HEAD_DIM))).astype(BF16)

    bias = lambda pos, w1: jnp.einsum('ld,ldh->h', pos, w1,
                                      precision=lax.Precision.HIGHEST).reshape(1, CMP_HIDDEN)
    gk = jnp.pad(k_norm_cmp, (0, LANES - HEAD_DIM)).reshape(1, LANES)
    xspec = pl.BlockSpec((1, nh, hb), lambda b: (b, 0, 0))
    ospec = pl.BlockSpec((1, N_KV_HEADS, LANES, nh), lambda b: (b, 0, 0, 0))
    vspec = pl.BlockSpec((1, N_KV_HEADS, nh, LANES), lambda b: (b, 0, 0, 0))
    w1_shape = (hb, N_KV_HEADS * 2 * CMP_HIDDEN)
    return pl.pallas_call(
        functools.partial(_cmp_body, nh=nh),
        grid=(bsz,),
        in_specs=[xspec, xspec, _const_spec(w1_shape), _const_spec(w1_shape),
                  _const_spec((1, CMP_HIDDEN)), _const_spec((1, CMP_HIDDEN)),
                  _const_spec((CMP_HIDDEN, LANES)), _const_spec((CMP_HIDDEN, LANES)),
                  _const_spec((1, LANES))],
        out_specs=[ospec, vspec],
        out_shape=[jax.ShapeDtypeStruct((bsz, N_KV_HEADS, LANES, nh), BF16),
                   jax.ShapeDtypeStruct((bsz, N_KV_HEADS, nh, LANES), BF16)],
        compiler_params=_cparams(("parallel",)),
        name="kv_compress",
    )(halfblocks(kc_raw), halfblocks(vc_raw), w1cat(k_w1), w1cat(v_w1),
      bias(pos_k, k_w1), bias(pos_v, v_w1), w2pad(k_w2), w2pad(v_w2), gk)


def _qproj_body(h_ref, g_ref, w_ref, qsc_ref, qc_ref, q_ref, gate_ref):
    u = (_rms(h_ref[...], D_MODEL) * g_ref[...]).astype(BF16)
    qg = jnp.dot(u, w_ref[...], preferred_element_type=F32)
    for hh in range(N_HEADS):
        sl = slice(hh * LANES, (hh + 1) * LANES)
        q_ref[:, sl] = (_rms(qg[:, sl], HEAD_DIM) * qsc_ref[:, sl] + qc_ref[:, sl]).astype(BF16)
    gate_ref[...] = jax.nn.sigmoid(qg[:, N_HEADS * LANES:])


def _qproj(h, mix_gain, w_qg, q_norm, rel_bias, bsz, seq):
    rows = min(QP_ROWS, seq)
    n = bsz * seq
    nq = N_HEADS * HEAD_DIM
    wq = jnp.concatenate([_pad_heads(w_qg[:, :nq], N_HEADS),
                          jnp.pad(w_qg[:, nq:], ((0, 0), (0, LANES - N_BRANCH * N_HEADS)))],
                         axis=1).astype(BF16)
    qsc = jnp.tile(jnp.pad(q_norm * (ATTN_SCALE * LOG2E), (0, LANES - HEAD_DIM)), N_HEADS)
    qsc = qsc.reshape(1, N_HEADS * LANES)
    far = _split3(rel_bias[NUM_BUCKETS - 1] * LOG2E)
    qc = jnp.zeros((N_HEADS, LANES), F32)
    for t, term in enumerate(far):
        qc = qc.at[:, FARB_LANE0 + t].set(term.astype(F32))
    qc = qc.at[:, PAD_LANE].set(NEG_INF).reshape(1, N_HEADS * LANES)
    tok = lambda c: pl.BlockSpec((rows, c), lambda i: (i, 0))
    return pl.pallas_call(
        _qproj_body,
        grid=(n // rows,),
        in_specs=[tok(D_MODEL), _const_spec((1, D_MODEL)), _const_spec(wq.shape),
                  _const_spec((1, N_HEADS * LANES)), _const_spec((1, N_HEADS * LANES))],
        out_specs=[tok(N_HEADS * LANES), tok(LANES)],
        out_shape=[jax.ShapeDtypeStruct((n, N_HEADS * LANES), BF16),
                   jax.ShapeDtypeStruct((n, LANES), F32)],
        compiler_params=_cparams(("parallel",)),
        name="q_proj",
    )(h, mix_gain.reshape(1, D_MODEL), wq, qsc, qc)


def _attn_body(q_ref, gate_ref, kc_ref, vc_ref, ks_ref, vs_ref, kw_ref, vw_ref,
               ctab_ref, wtab_ref, ovlt_ref, gexp_ref,
               o_ref, s_scr, sn_scr, m_scr, qf_scr, acc_scr, out_scr, *, n_sel, n_top, nt):
    i = pl.program_id(1)
    rq = Q_PER_KV * QT
    n_win = WIN // QT + 1
    lane = lax.broadcasted_iota(jnp.int32, (QT, LANES), 1)
    sel_lane = (lane >= SEL_LANE0) & (lane < SEL_LANE0 + n_sel)
    far_cut = sel_lane & (lane - SEL_LANE0 >= 2 * (i - 1))
    sel_lane4 = jnp.concatenate([sel_lane] * Q_PER_KV, axis=0)
    blk = lax.broadcasted_iota(jnp.int32, (n_sel, QT), 0)
    blkf = blk.astype(F32)
    qpos = i * QT + lax.broadcasted_iota(jnp.int32, (n_sel, QT), 1)
    cur = lax.shift_right_arithmetic(qpos, int(math.log2(SEL_LEN)))
    forced = (blk == 0) | (blk == cur) | (blk == cur - 1)
    causal = blk * SEL_LEN <= qpos
    tile = lambda t: jnp.where(t < 0, nt, t)

    def gate_of(b):
        return sum(jnp.dot(t, gexp_ref[b], preferred_element_type=F32) for t in _split3(gate_ref[...])[:2])

    gexp = {0: gate_of(0), 2: gate_of(2)}

    def merged(o, g, pr):
        return o[2 * pr * QT:(2 * pr + 1) * QT] + pltpu.roll(o[(2 * pr + 1) * QT:(2 * pr + 2) * QT], HEAD_DIM, 1)

    def normalise(pv):
        return pv[:, :LANES] * (1.0 / pv[:, LANES:])

    def tab(g, lo, width):
        return wtab_ref[Q_PER_KV * g:Q_PER_KV * (g + 1), :, lo:lo + width].reshape(rq, width)

    def queries(g):
        return jnp.concatenate([q_ref[:, (Q_PER_KV * g + r) * LANES:(Q_PER_KV * g + r + 1) * LANES]
                                for r in range(Q_PER_KV)], axis=0)

    cmp_p, pv_cs, picked, win = {}, {}, {}, {}
    w_tiles = [tile(i - (n_win - 1) + j) for j in range(n_win)]

    def cmp_softmax(g):
        sc = jnp.dot(queries(g), kc_ref[0, g], preferred_element_type=F32)
        sc = sc + ctab_ref[Q_PER_KV * g:Q_PER_KV * (g + 1)].reshape(rq, sc.shape[-1])
        e = jnp.exp2(sc - jnp.max(sc, axis=-1, keepdims=True))
        row_ok = i * QT + lax.broadcasted_iota(jnp.int32, (QT, sc.shape[-1]), 0) >= CMP_LEN - 1
        cmp_p[g] = jnp.where(jnp.concatenate([row_ok] * Q_PER_KV, axis=0),
                             e * (1.0 / jnp.sum(e, axis=-1, keepdims=True)), 0.0)

    def cmp_out(g):
        p = cmp_p[g]
        pv_cs[g] = jnp.dot(p.astype(BF16), vc_ref[0, g], preferred_element_type=F32)
        psum = p[0:QT] + p[QT:2 * QT] + p[2 * QT:3 * QT] + p[3 * QT:4 * QT]
        ps_t = sum(jnp.dot(ovlt_ref[...], t, preferred_element_type=F32) for t in _split3(psum.T))
        cmp_p[g] = jnp.where(forced, BIG, jnp.where(causal, ps_t[SEL_LANE0:SEL_LANE0 + n_sel], -BIG))

    def select(g):
        score = cmp_p[g]
        pick = jnp.zeros((n_sel, QT), F32)
        for _ in range(n_top):
            mx = jnp.max(score, axis=0, keepdims=True)
            first = jnp.min(jnp.where(score == mx, blkf, float(LANES)), axis=0, keepdims=True)
            hit = blkf == first
            pick = jnp.where(hit, 1.0, pick)
            score = jnp.where(hit, -jnp.inf, score)
        picked[g] = pick

    def win_logits(g):
        k_w = jnp.concatenate([kw_ref[0, g, t] for t in w_tiles], axis=1)
        s_w = jnp.dot(queries(g), k_w, preferred_element_type=F32)
        win[g] = [s_w[:, :QT] + tab(g, 0, QT), s_w[:, QT:(n_win - 2) * QT],
                  s_w[:, (n_win - 2) * QT:] + tab(g, QT, 2 * QT)]

    def win_softmax(g):
        mw = jnp.max(jnp.concatenate(win[g], axis=1), axis=-1, keepdims=True)
        win[g] = jnp.concatenate([jnp.exp2(t - mw).astype(BF16) for t in win[g]], axis=1)

    def win_out(g):
        v_w = jnp.concatenate([vw_ref[0, g, t] for t in w_tiles], axis=0)
        o_w = normalise(jnp.dot(win[g], v_w, preferred_element_type=F32))
        for pr in range(Q_PER_KV // 2):
            col = (Q_PER_KV * g // 2 + pr) * LANES
            out_scr[:, col:col + LANES] = (gexp[0][:, col:col + LANES] * merged(pv_cs[g], g, pr)
                                           + gexp[2][:, col:col + LANES] * merged(o_w, g, pr))

    def near_logits(g):
        q = queries(g)
        selb_t = jnp.where(picked[g] == 0.0, NEG_INF, 0.0)
        selb = jnp.concatenate([jnp.zeros((SEL_LANE0, QT), F32), selb_t,
                                jnp.zeros((LANES - SEL_LANE0 - n_sel, QT), F32)], axis=0).T
        selb_far = jnp.where(far_cut, NEG_INF, selb)
        q_near = jnp.where(sel_lane4, jnp.concatenate([selb.astype(BF16)] * Q_PER_KV, axis=0), q)
        qf_scr[g] = jnp.where(sel_lane4, jnp.concatenate([selb_far.astype(BF16)] * Q_PER_KV, axis=0), q)
        k_n = jnp.concatenate([ks_ref[0, g, tile(i - 1)], ks_ref[0, g, i]], axis=1)
        s_n = jnp.dot(q_near, k_n, preferred_element_type=F32) + tab(g, QT, 2 * QT)
        sn_scr[g] = s_n
        m_scr[g] = jnp.maximum(s_n[:, :QT], s_n[:, QT:])

    stages = (cmp_softmax, win_logits, cmp_out, win_softmax, select, win_out, near_logits)
    for step in range(N_KV_HEADS + len(stages) - 1):
        for k, stage in enumerate(stages):
            if 0 <= step - k < N_KV_HEADS:
                stage(step - k)

    n_chunks = (jnp.maximum(i - 1, 0) + FAR_TILES - 1) // FAR_TILES

    def far_logits(c, carry):
        for g in range(N_KV_HEADS):
            k_f = jnp.concatenate([ks_ref[0, g, FAR_TILES * c + t] for t in range(FAR_TILES)], axis=1)
            s = jnp.dot(qf_scr[g], k_f, preferred_element_type=F32)
            s_scr[g, c] = s
            m = m_scr[g]
            for t in range(FAR_TILES):
                m = jnp.maximum(m, s[:, t * QT:(t + 1) * QT])
            m_scr[g] = m
        return carry

    lax.fori_loop(0, n_chunks, far_logits, 0)

    near_p = {}

    def near_softmax(g):
        m = jnp.max(m_scr[g], axis=-1, keepdims=True)
        m_scr[g] = jnp.broadcast_to(m, (rq, QT))
        near_p[g] = jnp.exp2(sn_scr[g] - m).astype(BF16)

    def near_pv(g):
        v_n = jnp.concatenate([vs_ref[0, g, tile(i - 1)], vs_ref[0, g, i]], axis=0)
        acc_scr[g] = jnp.dot(near_p[g], v_n, preferred_element_type=F32)

    for step in range(N_KV_HEADS + 1):
        if step < N_KV_HEADS:
            near_softmax(step)
        if step > 0:
            near_pv(step - 1)

    def far_pv(c, carry):
        for g in range(N_KV_HEADS):
            m = jnp.concatenate([m_scr[g]] * FAR_TILES, axis=1)
            p = jnp.exp2(s_scr[g, c] - m).astype(BF16)
            v_f = jnp.concatenate([vs_ref[0, g, FAR_TILES * c + t] for t in range(FAR_TILES)], axis=0)
            acc_scr[g] += jnp.dot(p, v_f, preferred_element_type=F32)
        return carry

    lax.fori_loop(0, n_chunks, far_pv, 0)

    gexp[1] = gate_of(1)
    for g in range(N_KV_HEADS):
        o_s = normalise(acc_scr[g])
        for pr in range(Q_PER_KV // 2):
            col = (Q_PER_KV * g // 2 + pr) * LANES
            o = out_scr[:, col:col + LANES] + gexp[1][:, col:col + LANES] * merged(o_s, g, pr)
            o_ref[:, col:col + LANES] = o.astype(BF16)


def _rel_bucket(dist):
    n = jnp.maximum(dist, 0)
    max_exact = NUM_BUCKETS // 2
    logv = jnp.log(jnp.maximum(n, 1).astype(F32) / max_exact) / math.log(MAX_DISTANCE / max_exact)
    large = jnp.minimum(max_exact + (logv * (NUM_BUCKETS - max_exact)).astype(jnp.int32), NUM_BUCKETS - 1)
    return jnp.where(n < max_exact, n, large)


def _bias_tables(rel_bias, seq):
    nt = seq // QT
    nc = seq // CMP_STRIDE
    cpt = QT // CMP_STRIDE
    rb = rel_bias * LOG2E

    def f(dist):
        onehot = (_rel_bucket(dist)[..., None] == jnp.arange(NUM_BUCKETS)).astype(F32)
        return jnp.einsum('...k,kh->h...', onehot, rb, precision=lax.Precision.HIGHEST)

    c_rel = jnp.arange(-cpt * (nt - 1), nc)
    dc = jnp.arange(QT)[:, None] - (c_rel * CMP_STRIDE + CMP_LEN - 1)[None, :]
    t0 = jnp.where((dc >= 0)[None], f(dc), NEG_INF)
    ctab = jnp.stack([t0[:, :, cpt * (nt - 1 - i):cpt * (nt - 1 - i) + nc] for i in range(nt)], axis=1)
    ctab = ctab.reshape(N_HEADS, seq, nc)
    far = rb[NUM_BUCKETS - 1].reshape(N_HEADS, 1, 1)
    dq = jnp.arange(QT)[:, None] - jnp.arange(QT)[None, :]
    d0, d3, d4 = WIN + dq, QT + dq, dq
    wtab = jnp.concatenate([jnp.where((d0 < WIN)[None], f(d0) - far, NEG_INF), f(d3) - far,
                            jnp.where((d4 >= 0)[None], f(d4) - far, NEG_INF)], axis=2)
    return ctab, wtab


def _overlap_matrix_t(seq):
    nc = seq // CMP_STRIDE
    n_sel = seq // SEL_LEN
    c_start = jnp.arange(nc) * CMP_STRIDE
    j_start = jnp.arange(n_sel) * SEL_LEN
    ov = jnp.clip(jnp.minimum(c_start[None, :] + CMP_LEN, j_start[:, None] + SEL_LEN)
                  - jnp.maximum(c_start[None, :], j_start[:, None]), 0, None)
    ov = ov.astype(F32) / CMP_LEN
    return jnp.pad(ov, ((SEL_LANE0, LANES - SEL_LANE0 - n_sel), (0, 0))).astype(BF16)


def _gate_expansion():
    col = jnp.arange(LANES).reshape(1, LANES, 1)
    head = (jnp.arange(D_MODEL) // HEAD_DIM).reshape(1, 1, D_MODEL)
    br = jnp.arange(N_BRANCH).reshape(N_BRANCH, 1, 1)
    return (col == head * N_BRANCH + br).astype(BF16)


def _attention(q, gates, kv, rel_bias, bsz, seq):
    kc, vc, ks, vs, kw, vw = kv
    nt = seq // QT
    nc = seq // CMP_STRIDE
    n_sel = seq // SEL_LEN
    n_top = min(SEL_TOPN, n_sel)
    n_chunk = max(-(-(nt - 2) // FAR_TILES), 1)
    assert n_chunk * FAR_TILES <= nt
    nts = nt + PAD_TILES
    ctab, wtab = _bias_tables(rel_bias, seq)
    rq = Q_PER_KV * QT
    per_batch = lambda shape: pl.BlockSpec((1,) + shape, lambda b, i: (b,) + (0,) * len(shape),
                                           pipeline_mode=pl.Buffered(1))
    tok = lambda c: pl.BlockSpec((QT, c), lambda b, i: (b * nt + i, 0))
    return pl.pallas_call(
        functools.partial(_attn_body, n_sel=n_sel, n_top=n_top, nt=nt),
        grid=(bsz, nt),
        in_specs=[tok(N_HEADS * LANES), tok(LANES),
                  per_batch((N_KV_HEADS, LANES, nc)),
                  per_batch((N_KV_HEADS, nc, LANES)),
                  per_batch((N_KV_HEADS, nts, LANES, QT)),
                  per_batch((N_KV_HEADS, nts, QT, 2 * LANES)),
                  per_batch((N_KV_HEADS, nts, LANES, QT)),
                  per_batch((N_KV_HEADS, nts, QT, 2 * LANES)),
                  pl.BlockSpec((N_HEADS, QT, nc), lambda b, i: (0, i, 0)),
                  _const_spec((N_HEADS, QT, 3 * QT)),
                  _const_spec((LANES, nc)),
                  _const_spec((N_BRANCH, LANES, D_MODEL))],
        out_specs=tok(D_MODEL),
        out_shape=jax.ShapeDtypeStruct((bsz * seq, D_MODEL), BF16),
        scratch_shapes=[pltpu.VMEM((N_KV_HEADS, n_chunk, rq, FAR_TILES * QT), F32),
                        pltpu.VMEM((N_KV_HEADS, rq, 2 * QT), F32),
                        pltpu.VMEM((N_KV_HEADS, rq, QT), F32),
                        pltpu.VMEM((N_KV_HEADS, rq, LANES), BF16),
                        pltpu.VMEM((N_KV_HEADS, rq, 2 * LANES), F32),
                        pltpu.VMEM((QT, D_MODEL), F32)],
        compiler_params=_cparams(("parallel", "arbitrary")),
        name="nsa_attn",
    )(q, gates, kc, vc, ks, vs, kw, vw, ctab, wtab, _overlap_matrix_t(seq), _gate_expansion())


def kernel(x, rel_bias, ffn1_norm, ffn1_w_in, ffn1_w_out, mix_norm, ffn2_norm, ffn2_w_in, ffn2_w_out,
           s5_a_re, s5_a_im, s5_log_dt, s5_b_re, s5_b_im, s5_c_re, s5_c_im, s5_d, s5_w_glu,
           kv_norm, w_kv, k_norm_cmp, k_norm_slc, k_norm_win, cmp_pos_k, cmp_pos_v,
           cmp_k_w1, cmp_k_w2, cmp_v_w1, cmp_v_w2, w_qg, q_norm, w_o):
    bsz, seq, _ = x.shape
    depth = ffn1_norm.shape[0]
    n_a = s5_a_re.shape[0]
    h = x.reshape(bsz * seq, D_MODEL)
    kv = None
    for layer in range(depth):
        s5_layer = layer < n_a
        h = _ffn(h, ffn1_norm[layer], ffn1_w_in[layer], ffn1_w_out[layer], bsz, seq,
                 in_tm=False, out_tm=s5_layer)
        pre = None
        if s5_layer:
            a = layer
            h = _s5(h, mix_norm[layer], s5_a_re[a], s5_a_im[a], s5_log_dt[a], s5_b_re[a], s5_b_im[a],
                    s5_c_re[a], s5_c_im[a], s5_d[a], s5_w_glu[a], bsz, seq)
        else:
            b = layer - n_a
            q, gates = _qproj(h, mix_norm[layer], w_qg[b], q_norm[b], rel_bias, bsz, seq)
            pre = (_attention(q, gates, kv, rel_bias, bsz, seq), w_o[b])
        h = _ffn(h, ffn2_norm[layer], ffn2_w_in[layer], ffn2_w_out[layer], bsz, seq,
                 in_tm=s5_layer, out_tm=False, pre=pre)
        if layer == n_a - 1:
            kc_raw, vc_raw, ks, vs, kw, vw = _kv_proj(h, kv_norm, w_kv, k_norm_slc, k_norm_win, bsz, seq)
            kc, vc = _compress(kc_raw, vc_raw, k_norm_cmp, cmp_pos_k, cmp_pos_v,
                               cmp_k_w1, cmp_k_w2, cmp_v_w1, cmp_v_w2, bsz, seq)
            kv = (kc, vc, ks, vs, kw, vw)
    return h.reshape(bsz, seq, D_MODEL)
```

```python
import functools
import math

import jax
import jax.numpy as jnp
from jax import lax
from jax.experimental import pallas as pl
from jax.experimental.pallas import tpu as pltpu

F32 = jnp.float32
BF16 = jnp.bfloat16

D_MODEL = 1024
D_FF = 2816
RMS_EPS = 1e-6
S5_GROUP_CH = 16
S5_GROUPS = D_MODEL // S5_GROUP_CH
S5_STATE = 64
N_HEADS = 16
HEAD_DIM = 64
N_KV_HEADS = 4
Q_PER_KV = N_HEADS // N_KV_HEADS
CMP_LEN = 32
CMP_STRIDE = 16
CMP_HIDDEN = 2 * HEAD_DIM
SEL_LEN = 64
SEL_TOPN = 8
WIN = 512
N_BRANCH = 3
ATTN_SCALE = HEAD_DIM ** -0.5
NUM_BUCKETS = 32
MAX_DISTANCE = 128
NEG_INF = -1e30
BIG = 1e9

LANES = 128
MXU_DIM = 256
VMEM_LIMIT = 56 * 1024 * 1024

FFN_ROWS = 1024
FF_CHUNK = MXU_DIM
S5_TC = 16
S5_SLAB = MXU_DIM
S5_LANES = 256
QP_ROWS = 512
QT = 128
FAR_TILES = 4
PAD_TILES = WIN // QT
LOG2E = math.log2(math.e)

SEL_LANE0 = HEAD_DIM
FARB_LANE0 = 96
PAD_LANE = 99


def _rms(x, n):
    ss = jnp.sum(x * x, axis=-1, keepdims=True)
    return x * lax.rsqrt(ss * (1.0 / n) + RMS_EPS)


def _cparams(sem):
    return pltpu.CompilerParams(dimension_semantics=sem, vmem_limit_bytes=VMEM_LIMIT)


def _const_spec(shape):
    nd = len(shape)
    return pl.BlockSpec(shape, lambda *_: (0,) * nd, pipeline_mode=pl.Buffered(1))


def _split3(x):
    x1 = x.astype(BF16)
    r1 = x - x1.astype(F32)
    x2 = r1.astype(BF16)
    x3 = (r1 - x2.astype(F32)).astype(BF16)
    return x1, x2, x3


def _ffn_body(*refs, pre):
    if pre:
        x_ref, po_ref, pw_ref, g_ref, win_ref, wout_ref, o_ref, act_ref = refs
        x = x_ref[...] + jnp.dot(po_ref[...], pw_ref[...], preferred_element_type=F32)
    else:
        x_ref, g_ref, win_ref, wout_ref, o_ref, act_ref = refs
        x = x_ref[...]
    xn = (_rms(x, D_MODEL) * g_ref[...]).astype(BF16)
    for c in range(D_FF // FF_CHUNK):
        lo = c * FF_CHUNK
        a = jnp.dot(xn, win_ref[:, lo:lo + FF_CHUNK], preferred_element_type=F32)
        b = jnp.dot(xn, win_ref[:, D_FF + lo:D_FF + lo + FF_CHUNK], preferred_element_type=F32)
        act_ref[:, lo:lo + FF_CHUNK] = (a * jax.nn.sigmoid(a) * b).astype(BF16)
    y = jnp.dot(act_ref[...], wout_ref[...], preferred_element_type=F32)
    o_ref[...] = x + 0.5 * y


def _token_spec(rows, seq, time_major):
    nt = seq // rows
    if time_major:
        return pl.BlockSpec((rows, D_MODEL), lambda b, t: (t, b))
    return pl.BlockSpec((rows, D_MODEL), lambda b, t: (b * nt + t, 0))


def _layer_spec(shape, layer):
    return pl.BlockSpec((None,) + shape, lambda *_: (layer, 0, 0), pipeline_mode=pl.Buffered(1))


def _ffn(h, gain, w_in, w_out, layer, bsz, seq, in_tm, out_tm, pre=None):
    rows = min(FFN_ROWS, seq)
    out_shape = (seq, bsz * D_MODEL) if out_tm else (bsz * seq, D_MODEL)
    args, specs = [h], [_token_spec(rows, seq, in_tm)]
    if pre is not None:
        args += [pre[0], pre[1].astype(BF16)]
        specs += [_token_spec(rows, seq, False), _const_spec((D_MODEL, D_MODEL))]
    args += [gain.reshape(1, D_MODEL), w_in, w_out]
    specs += [_const_spec((1, D_MODEL)), _layer_spec((D_MODEL, 2 * D_FF), layer),
              _layer_spec((D_FF, D_MODEL), layer)]
    return pl.pallas_call(
        functools.partial(_ffn_body, pre=pre is not None),
        grid=(bsz, seq // rows),
        in_specs=specs,
        out_specs=_token_spec(rows, seq, out_tm),
        out_shape=jax.ShapeDtypeStruct(out_shape, F32),
        scratch_shapes=[pltpu.VMEM((rows, D_FF), BF16)],
        compiler_params=_cparams(("parallel", "parallel")),
        name="ffn",
    )(*args)


def _s5_body(h_ref, g_ref, bmat_ref, cmat_ref, are_ref, aim_ref, d_ref, wglu_ref,
             o_ref, bu_ref, xb_ref, st_ref, y_ref, *, tc, nb):
    n_slab = D_MODEL // S5_SLAB
    half = S5_SLAB // S5_GROUP_CH * S5_STATE
    rows = tc * nb

    @pl.when(pl.program_id(0) == 0)
    def _():
        st_ref[...] = jnp.zeros_like(st_ref)

    h = h_ref[...].reshape(rows, D_MODEL)
    u = _rms(h, D_MODEL) * g_ref[...]
    ub = u.astype(BF16)

    def project(sl):
        bu_ref[sl % 2] = jnp.dot(ub[:, sl * S5_SLAB:(sl + 1) * S5_SLAB], bmat_ref[sl],
                                 preferred_element_type=F32)

    project(0)
    for sl in range(n_slab):
        if sl + 1 < n_slab:
            project(sl + 1)
        bu, xb = bu_ref.at[sl % 2], xb_ref.at[sl % 2]
        for wb in range(half // S5_LANES):
            re = slice(wb * S5_LANES, (wb + 1) * S5_LANES)
            im = slice(half + wb * S5_LANES, half + (wb + 1) * S5_LANES)
            ar = jnp.broadcast_to(are_ref[sl, :, re], (nb, S5_LANES))
            ai = jnp.broadcast_to(aim_ref[sl, :, re], (nb, S5_LANES))
            xr, xi = st_ref[sl, :, re], st_ref[sl, :, im]
            for t in range(tc):
                rs = slice(t * nb, (t + 1) * nb)
                xr, xi = ar * xr - ai * xi + bu[rs, re], ar * xi + ai * xr + bu[rs, im]
                xb[rs, re] = xr.astype(BF16)
                xb[rs, im] = xi.astype(BF16)
            st_ref[sl, :, re] = xr
            st_ref[sl, :, im] = xi
        y_ref[:, sl * S5_SLAB:(sl + 1) * S5_SLAB] = jnp.dot(
            xb_ref[sl % 2], cmat_ref[sl], preferred_element_type=F32)
    y = jax.nn.gelu(y_ref[...] + d_ref[...] * u)
    gate = jnp.dot(y.astype(BF16), wglu_ref[...], preferred_element_type=F32)
    o_ref[...] = (h + y * jax.nn.sigmoid(gate)).reshape(tc, nb, D_MODEL)


def _s5_params(a_re, a_im, log_dt, b_re, b_im, c_re, c_im):
    dt = jnp.exp(log_dt)[:, None]
    mag = jnp.exp(a_re * dt)
    ab_re = mag * jnp.cos(a_im * dt)
    ab_im = mag * jnp.sin(a_im * dt)
    den = a_re * a_re + a_im * a_im
    z_re = ((ab_re - 1.0) * a_re + ab_im * a_im) / den
    z_im = (ab_im * a_re - (ab_re - 1.0) * a_im) / den
    bb_re = z_re[..., None] * b_re - z_im[..., None] * b_im
    bb_im = z_re[..., None] * b_im + z_im[..., None] * b_re
    n_slab = D_MODEL // S5_SLAB
    gps = S5_GROUPS // n_slab
    eye = jnp.eye(gps, dtype=F32)

    def in_mat(bb):
        bb = bb.reshape(n_slab, gps, S5_STATE, S5_GROUP_CH)
        m = jnp.einsum('sgph,gk->sghkp', bb, eye)
        return m.reshape(n_slab, gps * S5_GROUP_CH, gps * S5_STATE)

    def out_mat(cc):
        cc = cc.reshape(n_slab, gps, S5_GROUP_CH, S5_STATE)
        m = jnp.einsum('sghp,gk->sgpkh', cc, eye)
        return m.reshape(n_slab, gps * S5_STATE, gps * S5_GROUP_CH)

    bmat = jnp.concatenate([in_mat(bb_re), in_mat(bb_im)], axis=2).astype(BF16)
    cmat = jnp.concatenate([out_mat(c_re), out_mat(-c_im)], axis=1).astype(BF16)
    are = ab_re.reshape(n_slab, 1, gps * S5_STATE)
    aim = ab_im.reshape(n_slab, 1, gps * S5_STATE)
    return bmat, cmat, are, aim


def _s5(h_tm, gain, a_re, a_im, log_dt, b_re, b_im, c_re, c_im, d_skip, w_glu, bsz, seq):
    tc = min(S5_TC, seq)
    n_slab = D_MODEL // S5_SLAB
    half = S5_SLAB // S5_GROUP_CH * S5_STATE
    bmat, cmat, are, aim = _s5_params(a_re, a_im, log_dt, b_re, b_im, c_re, c_im)
    blk = pl.BlockSpec((tc, bsz, D_MODEL), lambda i: (i, 0, 0))
    out = pl.pallas_call(
        functools.partial(_s5_body, tc=tc, nb=bsz),
        grid=(seq // tc,),
        in_specs=[blk,
                  _const_spec((1, D_MODEL)),
                  _const_spec((n_slab, S5_SLAB, 2 * half)),
                  _const_spec((n_slab, 2 * half, S5_SLAB)),
                  _const_spec((n_slab, 1, half)),
                  _const_spec((n_slab, 1, half)),
                  _const_spec((1, D_MODEL)),
                  _const_spec((D_MODEL, D_MODEL))],
        out_specs=blk,
        out_shape=jax.ShapeDtypeStruct((seq, bsz, D_MODEL), F32),
        scratch_shapes=[pltpu.VMEM((2, tc * bsz, 2 * half), F32),
                        pltpu.VMEM((2, tc * bsz, 2 * half), BF16),
                        pltpu.VMEM((n_slab, bsz, 2 * half), F32),
                        pltpu.VMEM((tc * bsz, D_MODEL), F32)],
        compiler_params=_cparams(("arbitrary",)),
        name="s5",
    )(h_tm.reshape(seq, bsz, D_MODEL), gain.reshape(1, D_MODEL), bmat, cmat, are, aim,
      d_skip.reshape(1, D_MODEL), w_glu.astype(BF16))
    return out.reshape(seq, bsz * D_MODEL)


def _kv_body(h_ref, g_ref, w_ref, gk_ref, kc_ref, vc_ref, ks_ref, vs_ref, kw_ref, vw_ref, cs_ref,
             *, n_real):
    s = pl.program_id(1)
    gw = N_KV_HEADS * HEAD_DIM
    pw = N_KV_HEADS * LANES
    tiles = ks_ref.shape[2]
    row = lax.broadcasted_iota(jnp.int32, (LANES, QT), 0)
    key_hi = (lax.broadcasted_iota(jnp.int32, (LANES, QT), 1) >= SEL_LEN).astype(jnp.int32)
    farb_rows = ((row >= FARB_LANE0) & (row < FARB_LANE0 + 3)).astype(F32)
    ones = jnp.ones((QT, LANES), BF16)

    @pl.when(s < n_real)
    def _():
        hn = (_rms(h_ref[...], D_MODEL) * g_ref[...]).astype(BF16)
        kv = jnp.dot(hn, w_ref[...], preferred_element_type=F32)
        n_hb = kv.shape[0] // CMP_STRIDE
        for j in range(2 * gw // LANES):
            cs_ref[j] = kv[:, j * LANES:(j + 1) * LANES]
        for c_ref, j0 in ((kc_ref, 0), (vc_ref, gw // LANES)):
            for l in range(CMP_STRIDE):
                for j in range(gw // LANES):
                    col = l * gw + j * LANES
                    c_ref[:, col:col + LANES] = cs_ref[j0 + j, pl.ds(l, n_hb, stride=CMP_STRIDE), :].astype(BF16)
        for br, (k_ref, v_ref) in enumerate(((ks_ref, vs_ref), (kw_ref, vw_ref))):
            for g in range(N_KV_HEADS):
                lo = 2 * gw + br * 2 * pw + g * LANES
                k = _rms(kv[:, lo:lo + LANES], HEAD_DIM) * gk_ref[br]
                v = kv[:, lo + pw:lo + pw + LANES].astype(BF16)
                for t in range(tiles):
                    extra = farb_rows
                    if br == 0:
                        blk = 2 * (tiles * s + t) + key_hi
                        extra = extra + (row - SEL_LANE0 == blk).astype(F32)
                    k_ref[0, g, t] = (k[t * QT:(t + 1) * QT].T + extra).astype(BF16)
                    v_ref[0, g, t] = jnp.concatenate([v[t * QT:(t + 1) * QT], ones], axis=1)

    @pl.when(s == n_real)
    def _():
        pad_k = jnp.broadcast_to((row == PAD_LANE).astype(BF16), (N_KV_HEADS, tiles, LANES, QT))
        pad_v = jnp.zeros((N_KV_HEADS, tiles, QT, 2 * LANES), BF16)
        for k_ref, v_ref in ((ks_ref, vs_ref), (kw_ref, vw_ref)):
            k_ref[0] = pad_k
            v_ref[0] = pad_v


def _pad_heads(w, n_heads):
    k = w.shape[0]
    w = w.reshape(k, n_heads, HEAD_DIM)
    return jnp.pad(w, ((0, 0), (0, 0), (0, LANES - HEAD_DIM))).reshape(k, n_heads * LANES)


def _kv_proj(h, kv_norm, w_kv, k_norm_slc, k_norm_win, bsz, seq):
    gw = N_KV_HEADS * HEAD_DIM
    tiles = PAD_TILES
    rows = tiles * QT
    n_real = seq // rows
    nt = seq // QT
    wk = w_kv.reshape(D_MODEL, 2 * N_BRANCH, gw)
    w = jnp.concatenate([wk[:, 0], wk[:, 1]] + [_pad_heads(wk[:, j], N_KV_HEADS) for j in (2, 3, 4, 5)],
                        axis=1).astype(BF16)
    gk = jnp.pad(jnp.stack([k_norm_slc, k_norm_win]), ((0, 0), (0, LANES - HEAD_DIM)))
    gk = gk.reshape(2, 1, LANES)
    n = bsz * seq
    real = lambda s: jnp.minimum(s, n_real - 1)
    tok = lambda c: pl.BlockSpec((rows, c), lambda b, s: (b * n_real + real(s), 0))
    tile_spec = lambda r, c: pl.BlockSpec((1, N_KV_HEADS, tiles, r, c), lambda b, s: (b, 0, s, 0, 0))
    hb_spec = pl.BlockSpec((rows // CMP_STRIDE, CMP_STRIDE * gw), lambda b, s: (b * n_real + real(s), 0))
    dense = jax.ShapeDtypeStruct((n // CMP_STRIDE, CMP_STRIDE * gw), BF16)
    k_tiles = jax.ShapeDtypeStruct((bsz, N_KV_HEADS, nt + tiles, LANES, QT), BF16)
    v_tiles = jax.ShapeDtypeStruct((bsz, N_KV_HEADS, nt + tiles, QT, 2 * LANES), BF16)
    return pl.pallas_call(
        functools.partial(_kv_body, n_real=n_real),
        grid=(bsz, n_real + 1),
        in_specs=[tok(D_MODEL), _const_spec((1, D_MODEL)), _const_spec(w.shape),
                  _const_spec((2, 1, LANES))],
        out_specs=[hb_spec, hb_spec, tile_spec(LANES, QT), tile_spec(QT, 2 * LANES),
                   tile_spec(LANES, QT), tile_spec(QT, 2 * LANES)],
        out_shape=[dense, dense, k_tiles, v_tiles, k_tiles, v_tiles],
        scratch_shapes=[pltpu.VMEM((2 * gw // LANES, rows, LANES), F32)],
        compiler_params=_cparams(("parallel", "arbitrary")),
        name="kv_proj",
    )(h, kv_norm.reshape(1, D_MODEL), w, gk)


def _cmp_body(xk_ref, xv_ref, w1k_ref, w1v_ref, bk_ref, bv_ref, w2k_ref, w2v_ref, gk_ref,
              kc_ref, vc_ref, *, nh):
    ab_k = jnp.dot(xk_ref[0], w1k_ref[...], preferred_element_type=F32)
    ab_v = jnp.dot(xv_ref[0], w1v_ref[...], preferred_element_type=F32)
    for g in range(N_KV_HEADS):
        outs = []
        for ab_all, b_ref, w2_ref in ((ab_k, bk_ref, w2k_ref), (ab_v, bv_ref, w2v_ref)):
            ab = ab_all[:, g * 2 * CMP_HIDDEN:(g + 1) * 2 * CMP_HIDDEN]
            hid = ab[:, :CMP_HIDDEN] + pltpu.roll(ab[:, CMP_HIDDEN:], nh - 1, 0) + b_ref[...]
            hid = jax.nn.gelu(hid).astype(BF16)
            outs.append(jnp.dot(hid, w2_ref[...], preferred_element_type=F32))
        k = _rms(outs[0], HEAD_DIM) * gk_ref[...]
        kc_ref[0, g] = k.T.astype(BF16)
        vc_ref[0, g] = outs[1].astype(BF16)


def _compress(kc_raw, vc_raw, k_norm_cmp, pos_k, pos_v, k_w1, k_w2, v_w1, v_w2, bsz, seq):
    nh = seq // CMP_STRIDE
    hb = CMP_STRIDE * N_KV_HEADS * HEAD_DIM
    eye = jnp.eye(N_KV_HEADS, dtype=F32)

    def halfblocks(x):
        return x.reshape(bsz, nh, hb)

    def w1cat(w1):
        w = w1.reshape(2, CMP_STRIDE, HEAD_DIM, CMP_HIDDEN)
        w = jnp.concatenate([w[0], w[1]], axis=-1)
        w = jnp.einsum('ldj,gh->lgdhj', w, eye)
        return w.reshape(hb, N_KV_HEADS * 2 * CMP_HIDDEN).astype(BF16)

    def w2pad(w2):
        return jnp.pad(w2, ((0, 0), (0, LANES - HEAD_DIM))).astype(BF16)

    bias = lambda pos, w1: jnp.einsum('ld,ldh->h', pos, w1,
                                      precision=lax.Precision.HIGHEST).reshape(1, CMP_HIDDEN)
    gk = jnp.pad(k_norm_cmp, (0, LANES - HEAD_DIM)).reshape(1, LANES)
    xspec = pl.BlockSpec((1, nh, hb), lambda b: (b, 0, 0))
    ospec = pl.BlockSpec((1, N_KV_HEADS, LANES, nh), lambda b: (b, 0, 0, 0))
    vspec = pl.BlockSpec((1, N_KV_HEADS, nh, LANES), lambda b: (b, 0, 0, 0))
    w1_shape = (hb, N_KV_HEADS * 2 * CMP_HIDDEN)
    return pl.pallas_call(
        functools.partial(_cmp_body, nh=nh),
        grid=(bsz,),
        in_specs=[xspec, xspec, _const_spec(w1_shape), _const_spec(w1_shape),
                  _const_spec((1, CMP_HIDDEN)), _const_spec((1, CMP_HIDDEN)),
                  _const_spec((CMP_HIDDEN, LANES)), _const_spec((CMP_HIDDEN, LANES)),
                  _const_spec((1, LANES))],
        out_specs=[ospec, vspec],
        out_shape=[jax.ShapeDtypeStruct((bsz, N_KV_HEADS, LANES, nh), BF16),
                   jax.ShapeDtypeStruct((bsz, N_KV_HEADS, nh, LANES), BF16)],
        compiler_params=_cparams(("parallel",)),
        name="kv_compress",
    )(halfblocks(kc_raw), halfblocks(vc_raw), w1cat(k_w1), w1cat(v_w1),
      bias(pos_k, k_w1), bias(pos_v, v_w1), w2pad(k_w2), w2pad(v_w2), gk)


def _qproj_body(h_ref, g_ref, w_ref, qsc_ref, qc_ref, q_ref, gate_ref):
    u = (_rms(h_ref[...], D_MODEL) * g_ref[...]).astype(BF16)
    qg = jnp.dot(u, w_ref[...], preferred_element_type=F32)
    for hh in range(N_HEADS):
        sl = slice(hh * LANES, (hh + 1) * LANES)
        q_ref[:, sl] = (_rms(qg[:, sl], HEAD_DIM) * qsc_ref[:, sl] + qc_ref[:, sl]).astype(BF16)
    gate_ref[...] = jax.nn.sigmoid(qg[:, N_HEADS * LANES:])


def _qproj(h, mix_gain, w_qg, q_norm, rel_bias, bsz, seq):
    rows = min(QP_ROWS, seq)
    n = bsz * seq
    nq = N_HEADS * HEAD_DIM
    wq = jnp.concatenate([_pad_heads(w_qg[:, :nq], N_HEADS),
                          jnp.pad(w_qg[:, nq:], ((0, 0), (0, LANES - N_BRANCH * N_HEADS)))],
                         axis=1).astype(BF16)
    qsc = jnp.tile(jnp.pad(q_norm * (ATTN_SCALE * LOG2E), (0, LANES - HEAD_DIM)), N_HEADS)
    qsc = qsc.reshape(1, N_HEADS * LANES)
    far = _split3(rel_bias[NUM_BUCKETS - 1] * LOG2E)
    qc = jnp.zeros((N_HEADS, LANES), F32)
    for t, term in enumerate(far):
        qc = qc.at[:, FARB_LANE0 + t].set(term.astype(F32))
    qc = qc.at[:, PAD_LANE].set(NEG_INF).reshape(1, N_HEADS * LANES)
    tok = lambda c: pl.BlockSpec((rows, c), lambda i: (i, 0))
    return pl.pallas_call(
        _qproj_body,
        grid=(n // rows,),
        in_specs=[tok(D_MODEL), _const_spec((1, D_MODEL)), _const_spec(wq.shape),
                  _const_spec((1, N_HEADS * LANES)), _const_spec((1, N_HEADS * LANES))],
        out_specs=[tok(N_HEADS * LANES), tok(LANES)],
        out_shape=[jax.ShapeDtypeStruct((n, N_HEADS * LANES), BF16),
                   jax.ShapeDtypeStruct((n, LANES), F32)],
        compiler_params=_cparams(("parallel",)),
        name="q_proj",
    )(h, mix_gain.reshape(1, D_MODEL), wq, qsc, qc)


def _attn_body(q_ref, gate_ref, kc_ref, vc_ref, ks_ref, vs_ref, kw_ref, vw_ref,
               ctab_ref, wtab_ref, ovlt_ref, gexp_ref,
               o_ref, s_scr, sn_scr, m_scr, qf_scr, acc_scr, out_scr, *, n_sel, n_top, nt):
    i = pl.program_id(1)
    rq = Q_PER_KV * QT
    n_win = WIN // QT + 1
    lane = lax.broadcasted_iota(jnp.int32, (QT, LANES), 1)
    sel_lane = (lane >= SEL_LANE0) & (lane < SEL_LANE0 + n_sel)
    far_cut = sel_lane & (lane - SEL_LANE0 >= 2 * (i - 1))
    sel_lane4 = jnp.concatenate([sel_lane] * Q_PER_KV, axis=0)
    blk = lax.broadcasted_iota(jnp.int32, (n_sel, QT), 0)
    blkf = blk.astype(F32)
    qpos = i * QT + lax.broadcasted_iota(jnp.int32, (n_sel, QT), 1)
    cur = lax.shift_right_arithmetic(qpos, int(math.log2(SEL_LEN)))
    forced = (blk == 0) | (blk == cur) | (blk == cur - 1)
    causal = blk * SEL_LEN <= qpos
    tile = lambda t: jnp.where(t < 0, nt, t)

    def gate_of(b):
        return sum(jnp.dot(t, gexp_ref[b], preferred_element_type=F32) for t in _split3(gate_ref[...])[:2])

    gexp = {0: gate_of(0), 2: gate_of(2)}

    def merged(o, g, pr):
        return o[2 * pr * QT:(2 * pr + 1) * QT] + pltpu.roll(o[(2 * pr + 1) * QT:(2 * pr + 2) * QT], HEAD_DIM, 1)

    def normalise(pv):
        return pv[:, :LANES] * (1.0 / pv[:, LANES:])

    def tab(g, lo, width):
        return wtab_ref[Q_PER_KV * g:Q_PER_KV * (g + 1), :, lo:lo + width].reshape(rq, width)

    def queries(g):
        return jnp.concatenate([q_ref[:, (Q_PER_KV * g + r) * LANES:(Q_PER_KV * g + r + 1) * LANES]
                                for r in range(Q_PER_KV)], axis=0)

    cmp_p, pv_cs, picked, win = {}, {}, {}, {}
    w_tiles = [tile(i - (n_win - 1) + j) for j in range(n_win)]

    def cmp_softmax(g):
        sc = jnp.dot(queries(g), kc_ref[0, g], preferred_element_type=F32)
        sc = sc + ctab_ref[Q_PER_KV * g:Q_PER_KV * (g + 1)].reshape(rq, sc.shape[-1])
        e = jnp.exp2(sc - jnp.max(sc, axis=-1, keepdims=True))
        row_ok = i * QT + lax.broadcasted_iota(jnp.int32, (QT, sc.shape[-1]), 0) >= CMP_LEN - 1
        cmp_p[g] = jnp.where(jnp.concatenate([row_ok] * Q_PER_KV, axis=0),
                             e * (1.0 / jnp.sum(e, axis=-1, keepdims=True)), 0.0)

    def cmp_out(g):
        p = cmp_p[g]
        pv_cs[g] = jnp.dot(p.astype(BF16), vc_ref[0, g], preferred_element_type=F32)
        psum = p[0:QT] + p[QT:2 * QT] + p[2 * QT:3 * QT] + p[3 * QT:4 * QT]
        ps_t = sum(jnp.dot(ovlt_ref[...], t, preferred_element_type=F32) for t in _split3(psum.T))
        cmp_p[g] = jnp.where(forced, BIG, jnp.where(causal, ps_t[SEL_LANE0:SEL_LANE0 + n_sel], -BIG))

    def select(g):
        score = cmp_p[g]
        pick = jnp.zeros((n_sel, QT), F32)
        for _ in range(n_top):
            mx = jnp.max(score, axis=0, keepdims=True)
            first = jnp.min(jnp.where(score == mx, blkf, float(LANES)), axis=0, keepdims=True)
            hit = blkf == first
            pick = jnp.where(hit, 1.0, pick)
            score = jnp.where(hit, -jnp.inf, score)
        picked[g] = pick

    def win_logits(g):
        k_w = jnp.concatenate([kw_ref[0, g, t] for t in w_tiles], axis=1)
        s_w = jnp.dot(queries(g), k_w, preferred_element_type=F32)
        win[g] = [s_w[:, :QT] + tab(g, 0, QT), s_w[:, QT:(n_win - 2) * QT],
                  s_w[:, (n_win - 2) * QT:] + tab(g, QT, 2 * QT)]

    def win_softmax(g):
        mw = jnp.max(jnp.concatenate(win[g], axis=1), axis=-1, keepdims=True)
        win[g] = jnp.concatenate([jnp.exp2(t - mw).astype(BF16) for t in win[g]], axis=1)

    def win_out(g):
        v_w = jnp.concatenate([vw_ref[0, g, t] for t in w_tiles], axis=0)
        o_w = normalise(jnp.dot(win[g], v_w, preferred_element_type=F32))
        for pr in range(Q_PER_KV // 2):
            col = (Q_PER_KV * g // 2 + pr) * LANES
            out_scr[:, col:col + LANES] = (gexp[0][:, col:col + LANES] * merged(pv_cs[g], g, pr)
                                           + gexp[2][:, col:col + LANES] * merged(o_w, g, pr))

    def near_logits(g):
        q = queries(g)
        selb_t = jnp.where(picked[g] == 0.0, NEG_INF, 0.0)
        selb = jnp.concatenate([jnp.zeros((SEL_LANE0, QT), F32), selb_t,
                                jnp.zeros((LANES - SEL_LANE0 - n_sel, QT), F32)], axis=0).T
        selb_far = jnp.where(far_cut, NEG_INF, selb)
        q_near = jnp.where(sel_lane4, jnp.concatenate([selb.astype(BF16)] * Q_PER_KV, axis=0), q)
        qf_scr[g] = jnp.where(sel_lane4, jnp.concatenate([selb_far.astype(BF16)] * Q_PER_KV, axis=0), q)
        k_n = jnp.concatenate([ks_ref[0, g, tile(i - 1)], ks_ref[0, g, i]], axis=1)
        s_n = jnp.dot(q_near, k_n, preferred_element_type=F32) + tab(g, QT, 2 * QT)
        sn_scr[g] = s_n
        m_scr[g] = jnp.maximum(s_n[:, :QT], s_n[:, QT:])

    stages = (cmp_softmax, win_logits, cmp_out, win_softmax, select, win_out, near_logits)
    for step in range(N_KV_HEADS + len(stages) - 1):
        for k, stage in enumerate(stages):
            if 0 <= step - k < N_KV_HEADS:
                stage(step - k)

    n_chunks = (jnp.maximum(i - 1, 0) + FAR_TILES - 1) // FAR_TILES

    def far_logits(c, carry):
        for g in range(N_KV_HEADS):
            k_f = jnp.concatenate([ks_ref[0, g, FAR_TILES * c + t] for t in range(FAR_TILES)], axis=1)
            s = jnp.dot(qf_scr[g], k_f, preferred_element_type=F32)
            s_scr[g, c] = s
            m = m_scr[g]
            for t in range(FAR_TILES):
                m = jnp.maximum(m, s[:, t * QT:(t + 1) * QT])
            m_scr[g] = m
        return carry

    lax.fori_loop(0, n_chunks, far_logits, 0)

    near_p = {}

    def near_softmax(g):
        m = jnp.max(m_scr[g], axis=-1, keepdims=True)
        m_scr[g] = jnp.broadcast_to(m, (rq, QT))
        near_p[g] = jnp.exp2(sn_scr[g] - m).astype(BF16)

    def near_pv(g):
        v_n = jnp.concatenate([vs_ref[0, g, tile(i - 1)], vs_ref[0, g, i]], axis=0)
        acc_scr[g] = jnp.dot(near_p[g], v_n, preferred_element_type=F32)

    for step in range(N_KV_HEADS + 1):
        if step < N_KV_HEADS:
            near_softmax(step)
        if step > 0:
            near_pv(step - 1)

    def far_pv(c, carry):
        for g in range(N_KV_HEADS):
            m = jnp.concatenate([m_scr[g]] * FAR_TILES, axis=1)
            p = jnp.exp2(s_scr[g, c] - m).astype(BF16)
            v_f = jnp.concatenate([vs_ref[0, g, FAR_TILES * c + t] for t in range(FAR_TILES)], axis=0)
            acc_scr[g] += jnp.dot(p, v_f, preferred_element_type=F32)
        return carry

    lax.fori_loop(0, n_chunks, far_pv, 0)

    gexp[1] = gate_of(1)
    for g in range(N_KV_HEADS):
        o_s = normalise(acc_scr[g])
        for pr in range(Q_PER_KV // 2):
            col = (Q_PER_KV * g // 2 + pr) * LANES
            o = out_scr[:, col:col + LANES] + gexp[1][:, col:col + LANES] * merged(o_s, g, pr)
            o_ref[:, col:col + LANES] = o.astype(BF16)


def _rel_bucket(dist):
    n = jnp.maximum(dist, 0)
    max_exact = NUM_BUCKETS // 2
    logv = jnp.log(jnp.maximum(n, 1).astype(F32) / max_exact) / math.log(MAX_DISTANCE / max_exact)
    large = jnp.minimum(max_exact + (logv * (NUM_BUCKETS - max_exact)).astype(jnp.int32), NUM_BUCKETS - 1)
    return jnp.where(n < max_exact, n, large)


def _bias_tables(rel_bias, seq):
    nt = seq // QT
    nc = seq // CMP_STRIDE
    cpt = QT // CMP_STRIDE
    rb = rel_bias * LOG2E

    def f(dist):
        onehot = (_rel_bucket(dist)[..., None] == jnp.arange(NUM_BUCKETS)).astype(F32)
        return jnp.einsum('...k,kh->h...', onehot, rb, precision=lax.Precision.HIGHEST)

    c_rel = jnp.arange(-cpt * (nt - 1), nc)
    dc = jnp.arange(QT)[:, None] - (c_rel * CMP_STRIDE + CMP_LEN - 1)[None, :]
    t0 = jnp.where((dc >= 0)[None], f(dc), NEG_INF)
    ctab = jnp.stack([t0[:, :, cpt * (nt - 1 - i):cpt * (nt - 1 - i) + nc] for i in range(nt)], axis=1)
    ctab = ctab.reshape(N_HEADS, seq, nc)
    far = rb[NUM_BUCKETS - 1].reshape(N_HEADS, 1, 1)
    dq = jnp.arange(QT)[:, None] - jnp.arange(QT)[None, :]
    d0, d3, d4 = WIN + dq, QT + dq, dq
    wtab = jnp.concatenate([jnp.where((d0 < WIN)[None], f(d0) - far, NEG_INF), f(d3) - far,
                            jnp.where((d4 >= 0)[None], f(d4) - far, NEG_INF)], axis=2)
    return ctab, wtab


def _overlap_matrix_t(seq):
    nc = seq // CMP_STRIDE
    n_sel = seq // SEL_LEN
    c_start = jnp.arange(nc) * CMP_STRIDE
    j_start = jnp.arange(n_sel) * SEL_LEN
    ov = jnp.clip(jnp.minimum(c_start[None, :] + CMP_LEN, j_start[:, None] + SEL_LEN)
                  - jnp.maximum(c_start[None, :], j_start[:, None]), 0, None)
    ov = ov.astype(F32) / CMP_LEN
    return jnp.pad(ov, ((SEL_LANE0, LANES - SEL_LANE0 - n_sel), (0, 0))).astype(BF16)


def _gate_expansion():
    col = jnp.arange(LANES).reshape(1, LANES, 1)
    head = (jnp.arange(D_MODEL) // HEAD_DIM).reshape(1, 1, D_MODEL)
    br = jnp.arange(N_BRANCH).reshape(N_BRANCH, 1, 1)
    return (col == head * N_BRANCH + br).astype(BF16)


def _attention(q, gates, kv, rel_bias, bsz, seq):
    kc, vc, ks, vs, kw, vw = kv
    nt = seq // QT
    nc = seq // CMP_STRIDE
    n_sel = seq // SEL_LEN
    n_top = min(SEL_TOPN, n_sel)
    n_chunk = max(-(-(nt - 2) // FAR_TILES), 1)
    assert n_chunk * FAR_TILES <= nt
    nts = nt + PAD_TILES
    ctab, wtab = _bias_tables(rel_bias, seq)
    rq = Q_PER_KV * QT
    per_batch = lambda shape: pl.BlockSpec((1,) + shape, lambda b, i: (b,) + (0,) * len(shape),
                                           pipeline_mode=pl.Buffered(1))
    tok = lambda c: pl.BlockSpec((QT, c), lambda b, i: (b * nt + i, 0))
    return pl.pallas_call(
        functools.partial(_attn_body, n_sel=n_sel, n_top=n_top, nt=nt),
        grid=(bsz, nt),
        in_specs=[tok(N_HEADS * LANES), tok(LANES),
                  per_batch((N_KV_HEADS, LANES, nc)),
                  per_batch((N_KV_HEADS, nc, LANES)),
                  per_batch((N_KV_HEADS, nts, LANES, QT)),
                  per_batch((N_KV_HEADS, nts, QT, 2 * LANES)),
                  per_batch((N_KV_HEADS, nts, LANES, QT)),
                  per_batch((N_KV_HEADS, nts, QT, 2 * LANES)),
                  pl.BlockSpec((N_HEADS, QT, nc), lambda b, i: (0, i, 0)),
                  _const_spec((N_HEADS, QT, 3 * QT)),
                  _const_spec((LANES, nc)),
                  _const_spec((N_BRANCH, LANES, D_MODEL))],
        out_specs=tok(D_MODEL),
        out_shape=jax.ShapeDtypeStruct((bsz * seq, D_MODEL), BF16),
        scratch_shapes=[pltpu.VMEM((N_KV_HEADS, n_chunk, rq, FAR_TILES * QT), F32),
                        pltpu.VMEM((N_KV_HEADS, rq, 2 * QT), F32),
                        pltpu.VMEM((N_KV_HEADS, rq, QT), F32),
                        pltpu.VMEM((N_KV_HEADS, rq, LANES), BF16),
                        pltpu.VMEM((N_KV_HEADS, rq, 2 * LANES), F32),
                        pltpu.VMEM((QT, D_MODEL), F32)],
        compiler_params=_cparams(("parallel", "arbitrary")),
        name="nsa_attn",
    )(q, gates, kc, vc, ks, vs, kw, vw, ctab, wtab, _overlap_matrix_t(seq), _gate_expansion())


def kernel(x, rel_bias, ffn1_norm, ffn1_w_in, ffn1_w_out, mix_norm, ffn2_norm, ffn2_w_in, ffn2_w_out,
           s5_a_re, s5_a_im, s5_log_dt, s5_b_re, s5_b_im, s5_c_re, s5_c_im, s5_d, s5_w_glu,
           kv_norm, w_kv, k_norm_cmp, k_norm_slc, k_norm_win, cmp_pos_k, cmp_pos_v,
           cmp_k_w1, cmp_k_w2, cmp_v_w1, cmp_v_w2, w_qg, q_norm, w_o):
    bsz, seq, _ = x.shape
    depth = ffn1_norm.shape[0]
    n_a = s5_a_re.shape[0]
    h = x.reshape(bsz * seq, D_MODEL)
    kv = None
    ffn1_w_in, ffn1_w_out, ffn2_w_in, ffn2_w_out = (
        w.astype(BF16) for w in (ffn1_w_in, ffn1_w_out, ffn2_w_in, ffn2_w_out))
    for layer in range(depth):
        s5_layer = layer < n_a
        h = _ffn(h, ffn1_norm[layer], ffn1_w_in, ffn1_w_out, layer, bsz, seq,
                 in_tm=False, out_tm=s5_layer)
        pre = None
        if s5_layer:
            a = layer
            h = _s5(h, mix_norm[layer], s5_a_re[a], s5_a_im[a], s5_log_dt[a], s5_b_re[a], s5_b_im[a],
                    s5_c_re[a], s5_c_im[a], s5_d[a], s5_w_glu[a], bsz, seq)
        else:
            b = layer - n_a
            q, gates = _qproj(h, mix_norm[layer], w_qg[b], q_norm[b], rel_bias, bsz, seq)
            pre = (_attention(q, gates, kv, rel_bias, bsz, seq), w_o[b])
        h = _ffn(h, ffn2_norm[layer], ffn2_w_in, ffn2_w_out, layer, bsz, seq,
                 in_tm=s5_layer, out_tm=False, pre=pre)
        if layer == n_a - 1:
            kc_raw, vc_raw, ks, vs, kw, vw = _kv_proj(h, kv_norm, w_kv, k_norm_slc, k_norm_win, bsz, seq)
            kc, vc = _compress(kc_raw, vc_raw, k_norm_cmp, cmp_pos_k, cmp_pos_v,
                               cmp_k_w1, cmp_k_w2, cmp_v_w1, cmp_v_w2, bsz, seq)
            kv = (kc, vc, ks, vs, kw, vw)
    return h.reshape(bsz, seq, D_MODEL)
```

```python
import functools
import math

import jax
import jax.numpy as jnp
from jax import lax
from jax.experimental import pallas as pl
from jax.experimental.pallas import tpu as pltpu

F32 = jnp.float32
BF16 = jnp.bfloat16

D_MODEL = 1024
D_FF = 2816
RMS_EPS = 1e-6
S5_GROUP_CH = 16
S5_GROUPS = D_MODEL // S5_GROUP_CH
S5_STATE = 64
N_HEADS = 16
HEAD_DIM = 64
N_KV_HEADS = 4
Q_PER_KV = N_HEADS // N_KV_HEADS
CMP_LEN = 32
CMP_STRIDE = 16
CMP_HIDDEN = 2 * HEAD_DIM
SEL_LEN = 64
SEL_TOPN = 8
WIN = 512
N_BRANCH = 3
ATTN_SCALE = HEAD_DIM ** -0.5
NUM_BUCKETS = 32
MAX_DISTANCE = 128
NEG_INF = -1e30
BIG = 1e9

LANES = 128
MXU_DIM = 256
VMEM_LIMIT = 56 * 1024 * 1024

FFN_ROWS = 1024
FF_CHUNK = MXU_DIM
S5_TC = 16
S5_SLAB = MXU_DIM
S5_LANES = 256
QP_ROWS = 1024
QT = 128
FAR_TILES = 4
PAD_TILES = WIN // QT
LOG2E = math.log2(math.e)

SEL_LANE0 = HEAD_DIM
FARB_LANE0 = 96
PAD_LANE = 99


def _rms(x, n):
    ss = jnp.sum(x * x, axis=-1, keepdims=True)
    return x * lax.rsqrt(ss * (1.0 / n) + RMS_EPS)


def _cparams(sem):
    return pltpu.CompilerParams(dimension_semantics=sem, vmem_limit_bytes=VMEM_LIMIT)


def _const_spec(shape):
    nd = len(shape)
    return pl.BlockSpec(shape, lambda *_: (0,) * nd, pipeline_mode=pl.Buffered(1))


def _split3(x):
    x1 = x.astype(BF16)
    r1 = x - x1.astype(F32)
    x2 = r1.astype(BF16)
    x3 = (r1 - x2.astype(F32)).astype(BF16)
    return x1, x2, x3


def _ffn_body(*refs, pre):
    if pre:
        x_ref, po_ref, pw_ref, g_ref, win_ref, wout_ref, o_ref, act_ref = refs
        x = x_ref[...] + jnp.dot(po_ref[...], pw_ref[...], preferred_element_type=F32)
    else:
        x_ref, g_ref, win_ref, wout_ref, o_ref, act_ref = refs
        x = x_ref[...]
    xn = (_rms(x, D_MODEL) * g_ref[...]).astype(BF16)
    for c in range(D_FF // FF_CHUNK):
        lo = c * FF_CHUNK
        a = jnp.dot(xn, win_ref[:, lo:lo + FF_CHUNK], preferred_element_type=F32)
        b = jnp.dot(xn, win_ref[:, D_FF + lo:D_FF + lo + FF_CHUNK], preferred_element_type=F32)
        act_ref[:, lo:lo + FF_CHUNK] = (a * jax.nn.sigmoid(a) * b).astype(BF16)
    y = jnp.dot(act_ref[...], wout_ref[...], preferred_element_type=F32)
    o_ref[...] = x + 0.5 * y


def _token_spec(rows, seq, time_major):
    nt = seq // rows
    if time_major:
        return pl.BlockSpec((rows, D_MODEL), lambda b, t: (t, b))
    return pl.BlockSpec((rows, D_MODEL), lambda b, t: (b * nt + t, 0))


def _layer_spec(shape, layer):
    return pl.BlockSpec((None,) + shape, lambda *_: (layer, 0, 0), pipeline_mode=pl.Buffered(1))


def _ffn(h, gain, w_in, w_out, layer, bsz, seq, in_tm, out_tm, pre=None):
    rows = min(FFN_ROWS, seq)
    out_shape = (seq, bsz * D_MODEL) if out_tm else (bsz * seq, D_MODEL)
    args, specs = [h], [_token_spec(rows, seq, in_tm)]
    if pre is not None:
        args += [pre[0], pre[1].astype(BF16)]
        specs += [_token_spec(rows, seq, False), _const_spec((D_MODEL, D_MODEL))]
    args += [gain.reshape(1, D_MODEL), w_in, w_out]
    specs += [_const_spec((1, D_MODEL)), _layer_spec((D_MODEL, 2 * D_FF), layer),
              _layer_spec((D_FF, D_MODEL), layer)]
    return pl.pallas_call(
        functools.partial(_ffn_body, pre=pre is not None),
        grid=(bsz, seq // rows),
        in_specs=specs,
        out_specs=_token_spec(rows, seq, out_tm),
        out_shape=jax.ShapeDtypeStruct(out_shape, F32),
        scratch_shapes=[pltpu.VMEM((rows, D_FF), BF16)],
        compiler_params=_cparams(("parallel", "parallel")),
        name="ffn",
    )(*args)


def _s5_body(h_ref, g_ref, bmat_ref, cmat_ref, are_ref, aim_ref, d_ref, wglu_ref,
             o_ref, bu_ref, xb_ref, st_ref, y_ref, *, tc, nb):
    n_slab = D_MODEL // S5_SLAB
    half = S5_SLAB // S5_GROUP_CH * S5_STATE
    rows = tc * nb

    @pl.when(pl.program_id(0) == 0)
    def _():
        st_ref[...] = jnp.zeros_like(st_ref)

    h = h_ref[...].reshape(rows, D_MODEL)
    u = _rms(h, D_MODEL) * g_ref[...]
    ub = u.astype(BF16)

    def project(sl):
        bu_ref[sl % 2] = jnp.dot(ub[:, sl * S5_SLAB:(sl + 1) * S5_SLAB], bmat_ref[sl],
                                 preferred_element_type=F32)

    project(0)
    for sl in range(n_slab):
        if sl + 1 < n_slab:
            project(sl + 1)
        bu, xb = bu_ref.at[sl % 2], xb_ref.at[sl % 2]
        for wb in range(half // S5_LANES):
            re = slice(wb * S5_LANES, (wb + 1) * S5_LANES)
            im = slice(half + wb * S5_LANES, half + (wb + 1) * S5_LANES)
            ar = jnp.broadcast_to(are_ref[sl, :, re], (nb, S5_LANES))
            ai = jnp.broadcast_to(aim_ref[sl, :, re], (nb, S5_LANES))
            xr, xi = st_ref[sl, :, re], st_ref[sl, :, im]
            for t in range(tc):
                rs = slice(t * nb, (t + 1) * nb)
                xr, xi = ar * xr - ai * xi + bu[rs, re], ar * xi + ai * xr + bu[rs, im]
                xb[rs, re] = xr.astype(BF16)
                xb[rs, im] = xi.astype(BF16)
            st_ref[sl, :, re] = xr
            st_ref[sl, :, im] = xi
        y_ref[:, sl * S5_SLAB:(sl + 1) * S5_SLAB] = jnp.dot(
            xb_ref[sl % 2], cmat_ref[sl], preferred_element_type=F32)
    y = jax.nn.gelu(y_ref[...] + d_ref[...] * u)
    gate = jnp.dot(y.astype(BF16), wglu_ref[...], preferred_element_type=F32)
    o_ref[...] = (h + y * jax.nn.sigmoid(gate)).reshape(tc, nb, D_MODEL)


def _s5_params(a_re, a_im, log_dt, b_re, b_im, c_re, c_im):
    dt = jnp.exp(log_dt)[:, None]
    mag = jnp.exp(a_re * dt)
    ab_re = mag * jnp.cos(a_im * dt)
    ab_im = mag * jnp.sin(a_im * dt)
    den = a_re * a_re + a_im * a_im
    z_re = ((ab_re - 1.0) * a_re + ab_im * a_im) / den
    z_im = (ab_im * a_re - (ab_re - 1.0) * a_im) / den
    bb_re = z_re[..., None] * b_re - z_im[..., None] * b_im
    bb_im = z_re[..., None] * b_im + z_im[..., None] * b_re
    n_slab = D_MODEL // S5_SLAB
    gps = S5_GROUPS // n_slab
    eye = jnp.eye(gps, dtype=F32)

    def in_mat(bb):
        bb = bb.reshape(n_slab, gps, S5_STATE, S5_GROUP_CH)
        m = jnp.einsum('sgph,gk->sghkp', bb, eye)
        return m.reshape(n_slab, gps * S5_GROUP_CH, gps * S5_STATE)

    def out_mat(cc):
        cc = cc.reshape(n_slab, gps, S5_GROUP_CH, S5_STATE)
        m = jnp.einsum('sghp,gk->sgpkh', cc, eye)
        return m.reshape(n_slab, gps * S5_STATE, gps * S5_GROUP_CH)

    bmat = jnp.concatenate([in_mat(bb_re), in_mat(bb_im)], axis=2).astype(BF16)
    cmat = jnp.concatenate([out_mat(c_re), out_mat(-c_im)], axis=1).astype(BF16)
    are = ab_re.reshape(n_slab, 1, gps * S5_STATE)
    aim = ab_im.reshape(n_slab, 1, gps * S5_STATE)
    return bmat, cmat, are, aim


def _s5(h_tm, gain, a_re, a_im, log_dt, b_re, b_im, c_re, c_im, d_skip, w_glu, bsz, seq):
    tc = min(S5_TC, seq)
    n_slab = D_MODEL // S5_SLAB
    half = S5_SLAB // S5_GROUP_CH * S5_STATE
    bmat, cmat, are, aim = _s5_params(a_re, a_im, log_dt, b_re, b_im, c_re, c_im)
    blk = pl.BlockSpec((tc, bsz, D_MODEL), lambda i: (i, 0, 0))
    out = pl.pallas_call(
        functools.partial(_s5_body, tc=tc, nb=bsz),
        grid=(seq // tc,),
        in_specs=[blk,
                  _const_spec((1, D_MODEL)),
                  _const_spec((n_slab, S5_SLAB, 2 * half)),
                  _const_spec((n_slab, 2 * half, S5_SLAB)),
                  _const_spec((n_slab, 1, half)),
                  _const_spec((n_slab, 1, half)),
                  _const_spec((1, D_MODEL)),
                  _const_spec((D_MODEL, D_MODEL))],
        out_specs=blk,
        out_shape=jax.ShapeDtypeStruct((seq, bsz, D_MODEL), F32),
        scratch_shapes=[pltpu.VMEM((2, tc * bsz, 2 * half), F32),
                        pltpu.VMEM((2, tc * bsz, 2 * half), BF16),
                        pltpu.VMEM((n_slab, bsz, 2 * half), F32),
                        pltpu.VMEM((tc * bsz, D_MODEL), F32)],
        compiler_params=_cparams(("arbitrary",)),
        name="s5",
    )(h_tm.reshape(seq, bsz, D_MODEL), gain.reshape(1, D_MODEL), bmat, cmat, are, aim,
      d_skip.reshape(1, D_MODEL), w_glu.astype(BF16))
    return out.reshape(seq, bsz * D_MODEL)


def _kv_body(h_ref, g_ref, w_ref, gk_ref, kc_ref, vc_ref, ks_ref, vs_ref, kw_ref, vw_ref, cs_ref,
             *, n_real):
    s = pl.program_id(1)
    gw = N_KV_HEADS * HEAD_DIM
    pw = N_KV_HEADS * LANES
    tiles = ks_ref.shape[2]
    row = lax.broadcasted_iota(jnp.int32, (LANES, QT), 0)
    key_hi = (lax.broadcasted_iota(jnp.int32, (LANES, QT), 1) >= SEL_LEN).astype(jnp.int32)
    farb_rows = ((row >= FARB_LANE0) & (row < FARB_LANE0 + 3)).astype(F32)
    ones = jnp.ones((QT, LANES), BF16)

    @pl.when(s < n_real)
    def _():
        hn = (_rms(h_ref[...], D_MODEL) * g_ref[...]).astype(BF16)
        kv = jnp.dot(hn, w_ref[...], preferred_element_type=F32)
        n_hb = kv.shape[0] // CMP_STRIDE
        for j in range(2 * gw // LANES):
            cs_ref[j] = kv[:, j * LANES:(j + 1) * LANES]
        for c_ref, j0 in ((kc_ref, 0), (vc_ref, gw // LANES)):
            for l in range(CMP_STRIDE):
                for j in range(gw // LANES):
                    col = l * gw + j * LANES
                    c_ref[:, col:col + LANES] = cs_ref[j0 + j, pl.ds(l, n_hb, stride=CMP_STRIDE), :].astype(BF16)
        for br, (k_ref, v_ref) in enumerate(((ks_ref, vs_ref), (kw_ref, vw_ref))):
            for g in range(N_KV_HEADS):
                lo = 2 * gw + br * 2 * pw + g * LANES
                k = _rms(kv[:, lo:lo + LANES], HEAD_DIM) * gk_ref[br]
                v = kv[:, lo + pw:lo + pw + LANES].astype(BF16)
                for t in range(tiles):
                    extra = farb_rows
                    if br == 0:
                        blk = 2 * (tiles * s + t) + key_hi
                        extra = extra + (row - SEL_LANE0 == blk).astype(F32)
                    k_ref[0, g, t] = (k[t * QT:(t + 1) * QT].T + extra).astype(BF16)
                    v_ref[0, g, t] = jnp.concatenate([v[t * QT:(t + 1) * QT], ones], axis=1)

    @pl.when(s == n_real)
    def _():
        pad_k = jnp.broadcast_to((row == PAD_LANE).astype(BF16), (N_KV_HEADS, tiles, LANES, QT))
        pad_v = jnp.zeros((N_KV_HEADS, tiles, QT, 2 * LANES), BF16)
        for k_ref, v_ref in ((ks_ref, vs_ref), (kw_ref, vw_ref)):
            k_ref[0] = pad_k
            v_ref[0] = pad_v


def _pad_heads(w, n_heads):
    k = w.shape[0]
    w = w.reshape(k, n_heads, HEAD_DIM)
    return jnp.pad(w, ((0, 0), (0, 0), (0, LANES - HEAD_DIM))).reshape(k, n_heads * LANES)


def _kv_proj(h, kv_norm, w_kv, k_norm_slc, k_norm_win, bsz, seq):
    gw = N_KV_HEADS * HEAD_DIM
    tiles = PAD_TILES
    rows = tiles * QT
    n_real = seq // rows
    nt = seq // QT
    wk = w_kv.reshape(D_MODEL, 2 * N_BRANCH, gw)
    w = jnp.concatenate([wk[:, 0], wk[:, 1]] + [_pad_heads(wk[:, j], N_KV_HEADS) for j in (2, 3, 4, 5)],
                        axis=1).astype(BF16)
    gk = jnp.pad(jnp.stack([k_norm_slc, k_norm_win]), ((0, 0), (0, LANES - HEAD_DIM)))
    gk = gk.reshape(2, 1, LANES)
    n = bsz * seq
    real = lambda s: jnp.minimum(s, n_real - 1)
    tok = lambda c: pl.BlockSpec((rows, c), lambda b, s: (b * n_real + real(s), 0))
    tile_spec = lambda r, c: pl.BlockSpec((1, N_KV_HEADS, tiles, r, c), lambda b, s: (b, 0, s, 0, 0))
    hb_spec = pl.BlockSpec((rows // CMP_STRIDE, CMP_STRIDE * gw), lambda b, s: (b * n_real + real(s), 0))
    dense = jax.ShapeDtypeStruct((n // CMP_STRIDE, CMP_STRIDE * gw), BF16)
    k_tiles = jax.ShapeDtypeStruct((bsz, N_KV_HEADS, nt + tiles, LANES, QT), BF16)
    v_tiles = jax.ShapeDtypeStruct((bsz, N_KV_HEADS, nt + tiles, QT, 2 * LANES), BF16)
    return pl.pallas_call(
        functools.partial(_kv_body, n_real=n_real),
        grid=(bsz, n_real + 1),
        in_specs=[tok(D_MODEL), _const_spec((1, D_MODEL)), _const_spec(w.shape),
                  _const_spec((2, 1, LANES))],
        out_specs=[hb_spec, hb_spec, tile_spec(LANES, QT), tile_spec(QT, 2 * LANES),
                   tile_spec(LANES, QT), tile_spec(QT, 2 * LANES)],
        out_shape=[dense, dense, k_tiles, v_tiles, k_tiles, v_tiles],
        scratch_shapes=[pltpu.VMEM((2 * gw // LANES, rows, LANES), F32)],
        compiler_params=_cparams(("parallel", "arbitrary")),
        name="kv_proj",
    )(h, kv_norm.reshape(1, D_MODEL), w, gk)


def _cmp_body(xk_ref, xv_ref, w1k_ref, w1v_ref, bk_ref, bv_ref, w2k_ref, w2v_ref, gk_ref,
              kc_ref, vc_ref, *, nh):
    ab_k = jnp.dot(xk_ref[0], w1k_ref[...], preferred_element_type=F32)
    ab_v = jnp.dot(xv_ref[0], w1v_ref[...], preferred_element_type=F32)
    for g in range(N_KV_HEADS):
        outs = []
        for ab_all, b_ref, w2_ref in ((ab_k, bk_ref, w2k_ref), (ab_v, bv_ref, w2v_ref)):
            ab = ab_all[:, g * 2 * CMP_HIDDEN:(g + 1) * 2 * CMP_HIDDEN]
            hid = ab[:, :CMP_HIDDEN] + pltpu.roll(ab[:, CMP_HIDDEN:], nh - 1, 0) + b_ref[...]
            hid = jax.nn.gelu(hid).astype(BF16)
            outs.append(jnp.dot(hid, w2_ref[...], preferred_element_type=F32))
        k = _rms(outs[0], HEAD_DIM) * gk_ref[...]
        kc_ref[0, g] = k.T.astype(BF16)
        vc_ref[0, g] = outs[1].astype(BF16)


def _compress(kc_raw, vc_raw, k_norm_cmp, pos_k, pos_v, k_w1, k_w2, v_w1, v_w2, bsz, seq):
    nh = seq // CMP_STRIDE
    hb = CMP_STRIDE * N_KV_HEADS * HEAD_DIM
    eye = jnp.eye(N_KV_HEADS, dtype=F32)

    def halfblocks(x):
        return x.reshape(bsz, nh, hb)

    def w1cat(w1):
        w = w1.reshape(2, CMP_STRIDE, HEAD_DIM, CMP_HIDDEN)
        w = jnp.concatenate([w[0], w[1]], axis=-1)
        w = jnp.einsum('ldj,gh->lgdhj', w, eye)
        return w.reshape(hb, N_KV_HEADS * 2 * CMP_HIDDEN).astype(BF16)

    def w2pad(w2):
        return jnp.pad(w2, ((0, 0), (0, LANES - HEAD_DIM))).astype(BF16)

    bias = lambda pos, w1: jnp.einsum('ld,ldh->h', pos, w1,
                                      precision=lax.Precision.HIGHEST).reshape(1, CMP_HIDDEN)
    gk = jnp.pad(k_norm_cmp, (0, LANES - HEAD_DIM)).reshape(1, LANES)
    xspec = pl.BlockSpec((1, nh, hb), lambda b: (b, 0, 0))
    ospec = pl.BlockSpec((1, N_KV_HEADS, LANES, nh), lambda b: (b, 0, 0, 0))
    vspec = pl.BlockSpec((1, N_KV_HEADS, nh, LANES), lambda b: (b, 0, 0, 0))
    w1_shape = (hb, N_KV_HEADS * 2 * CMP_HIDDEN)
    return pl.pallas_call(
        functools.partial(_cmp_body, nh=nh),
        grid=(bsz,),
        in_specs=[xspec, xspec, _const_spec(w1_shape), _const_spec(w1_shape),
                  _const_spec((1, CMP_HIDDEN)), _const_spec((1, CMP_HIDDEN)),
                  _const_spec((CMP_HIDDEN, LANES)), _const_spec((CMP_HIDDEN, LANES)),
                  _const_spec((1, LANES))],
        out_specs=[ospec, vspec],
        out_shape=[jax.ShapeDtypeStruct((bsz, N_KV_HEADS, LANES, nh), BF16),
                   jax.ShapeDtypeStruct((bsz, N_KV_HEADS, nh, LANES), BF16)],
        compiler_params=_cparams(("parallel",)),
        name="kv_compress",
    )(halfblocks(kc_raw), halfblocks(vc_raw), w1cat(k_w1), w1cat(v_w1),
      bias(pos_k, k_w1), bias(pos_v, v_w1), w2pad(k_w2), w2pad(v_w2), gk)


def _qproj_body(h_ref, g_ref, w_ref, qsc_ref, qc_ref, q_ref, gate_ref):
    u = (_rms(h_ref[...], D_MODEL) * g_ref[...]).astype(BF16)
    qg = jnp.dot(u, w_ref[...], preferred_element_type=F32)
    for hh in range(N_HEADS):
        sl = slice(hh * LANES, (hh + 1) * LANES)
        q_ref[:, sl] = (_rms(qg[:, sl], HEAD_DIM) * qsc_ref[:, sl] + qc_ref[:, sl]).astype(BF16)
    gate_ref[...] = jax.nn.sigmoid(qg[:, N_HEADS * LANES:])


def _qproj(h, mix_gain, w_qg, q_norm, rel_bias, bsz, seq):
    rows = min(QP_ROWS, seq)
    n = bsz * seq
    nq = N_HEADS * HEAD_DIM
    wq = jnp.concatenate([_pad_heads(w_qg[:, :nq], N_HEADS),
                          jnp.pad(w_qg[:, nq:], ((0, 0), (0, LANES - N_BRANCH * N_HEADS)))],
                         axis=1).astype(BF16)
    qsc = jnp.tile(jnp.pad(q_norm * (ATTN_SCALE * LOG2E), (0, LANES - HEAD_DIM)), N_HEADS)
    qsc = qsc.reshape(1, N_HEADS * LANES)
    far = _split3(rel_bias[NUM_BUCKETS - 1] * LOG2E)
    qc = jnp.zeros((N_HEADS, LANES), F32)
    for t, term in enumerate(far):
        qc = qc.at[:, FARB_LANE0 + t].set(term.astype(F32))
    qc = qc.at[:, PAD_LANE].set(NEG_INF).reshape(1, N_HEADS * LANES)
    tok = lambda c: pl.BlockSpec((rows, c), lambda i: (i, 0))
    return pl.pallas_call(
        _qproj_body,
        grid=(n // rows,),
        in_specs=[tok(D_MODEL), _const_spec((1, D_MODEL)), _const_spec(wq.shape),
                  _const_spec((1, N_HEADS * LANES)), _const_spec((1, N_HEADS * LANES))],
        out_specs=[tok(N_HEADS * LANES), tok(LANES)],
        out_shape=[jax.ShapeDtypeStruct((n, N_HEADS * LANES), BF16),
                   jax.ShapeDtypeStruct((n, LANES), F32)],
        compiler_params=_cparams(("parallel",)),
        name="q_proj",
    )(h, mix_gain.reshape(1, D_MODEL), wq, qsc, qc)


def _attn_body(q_ref, gate_ref, kc_ref, vc_ref, ks_ref, vs_ref, kw_ref, vw_ref,
               ctab_ref, wtab_ref, ovlt_ref, gexp_ref,
               o_ref, s_scr, sn_scr, m_scr, qf_scr, acc_scr, out_scr, *, n_sel, n_top, nt):
    i = pl.program_id(1)
    rq = Q_PER_KV * QT
    n_win = WIN // QT + 1
    lane = lax.broadcasted_iota(jnp.int32, (QT, LANES), 1)
    sel_lane = (lane >= SEL_LANE0) & (lane < SEL_LANE0 + n_sel)
    far_cut = sel_lane & (lane - SEL_LANE0 >= 2 * (i - 1))
    sel_lane4 = jnp.concatenate([sel_lane] * Q_PER_KV, axis=0)
    blk = lax.broadcasted_iota(jnp.int32, (n_sel, QT), 0)
    blkf = blk.astype(F32)
    qpos = i * QT + lax.broadcasted_iota(jnp.int32, (n_sel, QT), 1)
    cur = lax.shift_right_arithmetic(qpos, int(math.log2(SEL_LEN)))
    forced = (blk == 0) | (blk == cur) | (blk == cur - 1)
    causal = blk * SEL_LEN <= qpos
    tile = lambda t: jnp.where(t < 0, nt, t)

    def gate_of(b):
        return sum(jnp.dot(t, gexp_ref[b], preferred_element_type=F32) for t in _split3(gate_ref[...])[:2])

    gexp = {0: gate_of(0), 2: gate_of(2)}

    def merged(o, g, pr):
        return o[2 * pr * QT:(2 * pr + 1) * QT] + pltpu.roll(o[(2 * pr + 1) * QT:(2 * pr + 2) * QT], HEAD_DIM, 1)

    def normalise(pv):
        return pv[:, :LANES] * (1.0 / pv[:, LANES:])

    def tab(g, lo, width):
        return wtab_ref[Q_PER_KV * g:Q_PER_KV * (g + 1), :, lo:lo + width].reshape(rq, width)

    def queries(g):
        return jnp.concatenate([q_ref[:, (Q_PER_KV * g + r) * LANES:(Q_PER_KV * g + r + 1) * LANES]
                                for r in range(Q_PER_KV)], axis=0)

    cmp_p, pv_cs, picked, win = {}, {}, {}, {}
    w_tiles = [tile(i - (n_win - 1) + j) for j in range(n_win)]

    def cmp_softmax(g):
        sc = jnp.dot(queries(g), kc_ref[0, g], preferred_element_type=F32)
        sc = sc + ctab_ref[Q_PER_KV * g:Q_PER_KV * (g + 1)].reshape(rq, sc.shape[-1])
        e = jnp.exp2(sc - jnp.max(sc, axis=-1, keepdims=True))
        row_ok = i * QT + lax.broadcasted_iota(jnp.int32, (QT, sc.shape[-1]), 0) >= CMP_LEN - 1
        cmp_p[g] = jnp.where(jnp.concatenate([row_ok] * Q_PER_KV, axis=0),
                             e * (1.0 / jnp.sum(e, axis=-1, keepdims=True)), 0.0)

    def cmp_out(g):
        p = cmp_p[g]
        pv_cs[g] = jnp.dot(p.astype(BF16), vc_ref[0, g], preferred_element_type=F32)
        psum = p[0:QT] + p[QT:2 * QT] + p[2 * QT:3 * QT] + p[3 * QT:4 * QT]
        ps_t = sum(jnp.dot(ovlt_ref[...], t, preferred_element_type=F32) for t in _split3(psum.T))
        cmp_p[g] = jnp.where(forced, BIG, jnp.where(causal, ps_t[SEL_LANE0:SEL_LANE0 + n_sel], -BIG))

    def select(g):
        score = cmp_p[g]
        pick = jnp.zeros((n_sel, QT), F32)
        for _ in range(n_top):
            mx = jnp.max(score, axis=0, keepdims=True)
            first = jnp.min(jnp.where(score == mx, blkf, float(LANES)), axis=0, keepdims=True)
            hit = blkf == first
            pick = jnp.where(hit, 1.0, pick)
            score = jnp.where(hit, -jnp.inf, score)
        picked[g] = pick

    def win_logits(g):
        k_w = jnp.concatenate([kw_ref[0, g, t] for t in w_tiles], axis=1)
        s_w = jnp.dot(queries(g), k_w, preferred_element_type=F32)
        win[g] = [s_w[:, :QT] + tab(g, 0, QT), s_w[:, QT:(n_win - 2) * QT],
                  s_w[:, (n_win - 2) * QT:] + tab(g, QT, 2 * QT)]

    def win_softmax(g):
        mw = jnp.max(jnp.concatenate(win[g], axis=1), axis=-1, keepdims=True)
        win[g] = jnp.concatenate([jnp.exp2(t - mw).astype(BF16) for t in win[g]], axis=1)

    def win_out(g):
        v_w = jnp.concatenate([vw_ref[0, g, t] for t in w_tiles], axis=0)
        o_w = normalise(jnp.dot(win[g], v_w, preferred_element_type=F32))
        for pr in range(Q_PER_KV // 2):
            col = (Q_PER_KV * g // 2 + pr) * LANES
            out_scr[:, col:col + LANES] = (gexp[0][:, col:col + LANES] * merged(pv_cs[g], g, pr)
                                           + gexp[2][:, col:col + LANES] * merged(o_w, g, pr))

    def near_logits(g):
        q = queries(g)
        selb_t = jnp.where(picked[g] == 0.0, NEG_INF, 0.0)
        selb = jnp.concatenate([jnp.zeros((SEL_LANE0, QT), F32), selb_t,
                                jnp.zeros((LANES - SEL_LANE0 - n_sel, QT), F32)], axis=0).T
        selb_far = jnp.where(far_cut, NEG_INF, selb)
        q_near = jnp.where(sel_lane4, jnp.concatenate([selb.astype(BF16)] * Q_PER_KV, axis=0), q)
        qf_scr[g] = jnp.where(sel_lane4, jnp.concatenate([selb_far.astype(BF16)] * Q_PER_KV, axis=0), q)
        k_n = jnp.concatenate([ks_ref[0, g, tile(i - 1)], ks_ref[0, g, i]], axis=1)
        s_n = jnp.dot(q_near, k_n, preferred_element_type=F32) + tab(g, QT, 2 * QT)
        sn_scr[g] = s_n
        m_scr[g] = jnp.maximum(s_n[:, :QT], s_n[:, QT:])

    stages = (cmp_softmax, win_logits, cmp_out, win_softmax, select, win_out, near_logits)
    for step in range(N_KV_HEADS + len(stages) - 1):
        for k, stage in enumerate(stages):
            if 0 <= step - k < N_KV_HEADS:
                stage(step - k)

    n_chunks = (jnp.maximum(i - 1, 0) + FAR_TILES - 1) // FAR_TILES

    def chunk_loop(body):
        def pair(p, carry):
            body(2 * p)
            body(2 * p + 1)
            return carry

        lax.fori_loop(0, n_chunks // 2, pair, 0)

        @pl.when(n_chunks % 2 == 1)
        def _():
            body(n_chunks - 1)

    def far_logits(c):
        for g in range(N_KV_HEADS):
            k_f = jnp.concatenate([ks_ref[0, g, FAR_TILES * c + t] for t in range(FAR_TILES)], axis=1)
            s = jnp.dot(qf_scr[g], k_f, preferred_element_type=F32)
            s_scr[g, c] = s
            m = m_scr[g]
            for t in range(FAR_TILES):
                m = jnp.maximum(m, s[:, t * QT:(t + 1) * QT])
            m_scr[g] = m

    chunk_loop(far_logits)

    near_p = {}

    def near_softmax(g):
        m = jnp.max(m_scr[g], axis=-1, keepdims=True)
        m_scr[g] = jnp.broadcast_to(m, (rq, QT))
        near_p[g] = jnp.exp2(sn_scr[g] - m).astype(BF16)

    def near_pv(g):
        v_n = jnp.concatenate([vs_ref[0, g, tile(i - 1)], vs_ref[0, g, i]], axis=0)
        acc_scr[g] = jnp.dot(near_p[g], v_n, preferred_element_type=F32)

    for step in range(N_KV_HEADS + 1):
        if step < N_KV_HEADS:
            near_softmax(step)
        if step > 0:
            near_pv(step - 1)

    def far_pv(c):
        for g in range(N_KV_HEADS):
            m = jnp.concatenate([m_scr[g]] * FAR_TILES, axis=1)
            p = jnp.exp2(s_scr[g, c] - m).astype(BF16)
            v_f = jnp.concatenate([vs_ref[0, g, FAR_TILES * c + t] for t in range(FAR_TILES)], axis=0)
            acc_scr[g] += jnp.dot(p, v_f, preferred_element_type=F32)

    chunk_loop(far_pv)

    gexp[1] = gate_of(1)
    for g in range(N_KV_HEADS):
        o_s = normalise(acc_scr[g])
        for pr in range(Q_PER_KV // 2):
            col = (Q_PER_KV * g // 2 + pr) * LANES
            o = out_scr[:, col:col + LANES] + gexp[1][:, col:col + LANES] * merged(o_s, g, pr)
            o_ref[:, col:col + LANES] = o.astype(BF16)


def _rel_bucket(dist):
    n = jnp.maximum(dist, 0)
    max_exact = NUM_BUCKETS // 2
    logv = jnp.log(jnp.maximum(n, 1).astype(F32) / max_exact) / math.log(MAX_DISTANCE / max_exact)
    large = jnp.minimum(max_exact + (logv * (NUM_BUCKETS - max_exact)).astype(jnp.int32), NUM_BUCKETS - 1)
    return jnp.where(n < max_exact, n, large)


def _bias_tables(rel_bias, seq):
    nt = seq // QT
    nc = seq // CMP_STRIDE
    cpt = QT // CMP_STRIDE
    rb = rel_bias * LOG2E

    def f(dist):
        onehot = (_rel_bucket(dist)[..., None] == jnp.arange(NUM_BUCKETS)).astype(F32)
        return jnp.einsum('...k,kh->h...', onehot, rb, precision=lax.Precision.HIGHEST)

    c_rel = jnp.arange(-cpt * (nt - 1), nc)
    dc = jnp.arange(QT)[:, None] - (c_rel * CMP_STRIDE + CMP_LEN - 1)[None, :]
    t0 = jnp.where((dc >= 0)[None], f(dc), NEG_INF)
    ctab = jnp.stack([t0[:, :, cpt * (nt - 1 - i):cpt * (nt - 1 - i) + nc] for i in range(nt)], axis=1)
    ctab = ctab.reshape(N_HEADS, seq, nc)
    far = rb[NUM_BUCKETS - 1].reshape(N_HEADS, 1, 1)
    dq = jnp.arange(QT)[:, None] - jnp.arange(QT)[None, :]
    d0, d3, d4 = WIN + dq, QT + dq, dq
    wtab = jnp.concatenate([jnp.where((d0 < WIN)[None], f(d0) - far, NEG_INF), f(d3) - far,
                            jnp.where((d4 >= 0)[None], f(d4) - far, NEG_INF)], axis=2)
    return ctab, wtab


def _overlap_matrix_t(seq):
    nc = seq // CMP_STRIDE
    n_sel = seq // SEL_LEN
    c_start = jnp.arange(nc) * CMP_STRIDE
    j_start = jnp.arange(n_sel) * SEL_LEN
    ov = jnp.clip(jnp.minimum(c_start[None, :] + CMP_LEN, j_start[:, None] + SEL_LEN)
                  - jnp.maximum(c_start[None, :], j_start[:, None]), 0, None)
    ov = ov.astype(F32) / CMP_LEN
    return jnp.pad(ov, ((SEL_LANE0, LANES - SEL_LANE0 - n_sel), (0, 0))).astype(BF16)


def _gate_expansion():
    col = jnp.arange(LANES).reshape(1, LANES, 1)
    head = (jnp.arange(D_MODEL) // HEAD_DIM).reshape(1, 1, D_MODEL)
    br = jnp.arange(N_BRANCH).reshape(N_BRANCH, 1, 1)
    return (col == head * N_BRANCH + br).astype(BF16)


def _attention(q, gates, kv, rel_bias, bsz, seq):
    kc, vc, ks, vs, kw, vw = kv
    nt = seq // QT
    nc = seq // CMP_STRIDE
    n_sel = seq // SEL_LEN
    n_top = min(SEL_TOPN, n_sel)
    n_chunk = max(-(-(nt - 2) // FAR_TILES), 1)
    assert n_chunk * FAR_TILES <= nt
    nts = nt + PAD_TILES
    ctab, wtab = _bias_tables(rel_bias, seq)
    rq = Q_PER_KV * QT
    per_batch = lambda shape: pl.BlockSpec((1,) + shape, lambda b, i: (b,) + (0,) * len(shape),
                                           pipeline_mode=pl.Buffered(1))
    tok = lambda c: pl.BlockSpec((QT, c), lambda b, i: (b * nt + i, 0))
    return pl.pallas_call(
        functools.partial(_attn_body, n_sel=n_sel, n_top=n_top, nt=nt),
        grid=(bsz, nt),
        in_specs=[tok(N_HEADS * LANES), tok(LANES),
                  per_batch((N_KV_HEADS, LANES, nc)),
                  per_batch((N_KV_HEADS, nc, LANES)),
                  per_batch((N_KV_HEADS, nts, LANES, QT)),
                  per_batch((N_KV_HEADS, nts, QT, 2 * LANES)),
                  per_batch((N_KV_HEADS, nts, LANES, QT)),
                  per_batch((N_KV_HEADS, nts, QT, 2 * LANES)),
                  pl.BlockSpec((N_HEADS, QT, nc), lambda b, i: (0, i, 0)),
                  _const_spec((N_HEADS, QT, 3 * QT)),
                  _const_spec((LANES, nc)),
                  _const_spec((N_BRANCH, LANES, D_MODEL))],
        out_specs=tok(D_MODEL),
        out_shape=jax.ShapeDtypeStruct((bsz * seq, D_MODEL), BF16),
        scratch_shapes=[pltpu.VMEM((N_KV_HEADS, n_chunk, rq, FAR_TILES * QT), F32),
                        pltpu.VMEM((N_KV_HEADS, rq, 2 * QT), F32),
                        pltpu.VMEM((N_KV_HEADS, rq, QT), F32),
                        pltpu.VMEM((N_KV_HEADS, rq, LANES), BF16),
                        pltpu.VMEM((N_KV_HEADS, rq, 2 * LANES), F32),
                        pltpu.VMEM((QT, D_MODEL), F32)],
        compiler_params=_cparams(("parallel", "arbitrary")),
        name="nsa_attn",
    )(q, gates, kc, vc, ks, vs, kw, vw, ctab, wtab, _overlap_matrix_t(seq), _gate_expansion())


def kernel(x, rel_bias, ffn1_norm, ffn1_w_in, ffn1_w_out, mix_norm, ffn2_norm, ffn2_w_in, ffn2_w_out,
           s5_a_re, s5_a_im, s5_log_dt, s5_b_re, s5_b_im, s5_c_re, s5_c_im, s5_d, s5_w_glu,
           kv_norm, w_kv, k_norm_cmp, k_norm_slc, k_norm_win, cmp_pos_k, cmp_pos_v,
           cmp_k_w1, cmp_k_w2, cmp_v_w1, cmp_v_w2, w_qg, q_norm, w_o):
    bsz, seq, _ = x.shape
    depth = ffn1_norm.shape[0]
    n_a = s5_a_re.shape[0]
    h = x.reshape(bsz * seq, D_MODEL)
    kv = None
    ffn1_w_in, ffn1_w_out, ffn2_w_in, ffn2_w_out = (
        w.astype(BF16) for w in (ffn1_w_in, ffn1_w_out, ffn2_w_in, ffn2_w_out))
    for layer in range(depth):
        s5_layer = layer < n_a
        h = _ffn(h, ffn1_norm[layer], ffn1_w_in, ffn1_w_out, layer, bsz, seq,
                 in_tm=False, out_tm=s5_layer)
        pre = None
        if s5_layer:
            a = layer
            h = _s5(h, mix_norm[layer], s5_a_re[a], s5_a_im[a], s5_log_dt[a], s5_b_re[a], s5_b_im[a],
                    s5_c_re[a], s5_c_im[a], s5_d[a], s5_w_glu[a], bsz, seq)
        else:
            b = layer - n_a
            q, gates = _qproj(h, mix_norm[layer], w_qg[b], q_norm[b], rel_bias, bsz, seq)
            pre = (_attention(q, gates, kv, rel_bias, bsz, seq), w_o[b])
        h = _ffn(h, ffn2_norm[layer], ffn2_w_in, ffn2_w_out, layer, bsz, seq,
                 in_tm=s5_layer, out_tm=False, pre=pre)
        if layer == n_a - 1:
            kc_raw, vc_raw, ks, vs, kw, vw = _kv_proj(h, kv_norm, w_kv, k_norm_slc, k_norm_win, bsz, seq)
            kc, vc = _compress(kc_raw, vc_raw, k_norm_cmp, cmp_pos_k, cmp_pos_v,
                               cmp_k_w1, cmp_k_w2, cmp_v_w1, cmp_v_w2, bsz, seq)
            kv = (kc, vc, ks, vs, kw, vw)
    return h.reshape(bsz, seq, D_MODEL)
```

```python
import functools
import math

import jax
import jax.numpy as jnp
from jax import lax
from jax.experimental import pallas as pl
from jax.experimental.pallas import tpu as pltpu

F32 = jnp.float32
BF16 = jnp.bfloat16

D_MODEL = 1024
D_FF = 2816
RMS_EPS = 1e-6
S5_GROUP_CH = 16
S5_GROUPS = D_MODEL // S5_GROUP_CH
S5_STATE = 64
N_HEADS = 16
HEAD_DIM = 64
N_KV_HEADS = 4
Q_PER_KV = N_HEADS // N_KV_HEADS
CMP_LEN = 32
CMP_STRIDE = 16
CMP_HIDDEN = 2 * HEAD_DIM
SEL_LEN = 64
SEL_TOPN = 8
WIN = 512
N_BRANCH = 3
ATTN_SCALE = HEAD_DIM ** -0.5
NUM_BUCKETS = 32
MAX_DISTANCE = 128
NEG_INF = -1e30
BIG = 1e9

LANES = 128
MXU_DIM = 256
VMEM_LIMIT = 56 * 1024 * 1024

FFN_ROWS = 1024
FFN_BATCH = 8
FF_CHUNK = MXU_DIM
S5_TC = 16
S5_SLAB = MXU_DIM
S5_LANES = 256
QP_ROWS = 1024
QT = 128
FAR_TILES = 4
PAD_TILES = WIN // QT
LOG2E = math.log2(math.e)

SEL_LANE0 = HEAD_DIM
FARB_LANE0 = 96
PAD_LANE = 99


def _rms(x, n):
    ss = jnp.sum(x * x, axis=-1, keepdims=True)
    return x * lax.rsqrt(ss * (1.0 / n) + RMS_EPS)


def _cparams(sem):
    return pltpu.CompilerParams(dimension_semantics=sem, vmem_limit_bytes=VMEM_LIMIT)


def _const_spec(shape):
    nd = len(shape)
    return pl.BlockSpec(shape, lambda *_: (0,) * nd, pipeline_mode=pl.Buffered(1))


def _split3(x):
    x1 = x.astype(BF16)
    r1 = x - x1.astype(F32)
    x2 = r1.astype(BF16)
    x3 = (r1 - x2.astype(F32)).astype(BF16)
    return x1, x2, x3


def _ffn_body(*refs, pre, in_tm, out_tm):
    if pre:
        x_ref, po_ref, pw_ref, g_ref, win_ref, wout_ref, o_ref, act_ref = refs
    else:
        x_ref, g_ref, win_ref, wout_ref, o_ref, act_ref = refs
    x = pltpu.einshape("tbd->(bt)d", x_ref[...]) if in_tm else pltpu.einshape("btd->(bt)d", x_ref[...])
    if pre:
        po = pltpu.einshape("btd->(bt)d", po_ref[...])
        x = x + jnp.dot(po, pw_ref[...], preferred_element_type=F32)
    xn = (_rms(x, D_MODEL) * g_ref[...]).astype(BF16)
    for c in range(D_FF // FF_CHUNK):
        lo = c * FF_CHUNK
        a = jnp.dot(xn, win_ref[:, lo:lo + FF_CHUNK], preferred_element_type=F32)
        b = jnp.dot(xn, win_ref[:, D_FF + lo:D_FF + lo + FF_CHUNK], preferred_element_type=F32)
        act_ref[:, lo:lo + FF_CHUNK] = (a * jax.nn.sigmoid(a) * b).astype(BF16)
    y = jnp.dot(act_ref[...], wout_ref[...], preferred_element_type=F32)
    res = x + 0.5 * y
    nb = o_ref.shape[1] if out_tm else o_ref.shape[0]
    o_ref[...] = pltpu.einshape("(bt)d->tbd" if out_tm else "(bt)d->btd", res, b=nb)


def _token_spec(tb, tt, time_major):
    if time_major:
        return pl.BlockSpec((tt, tb, D_MODEL), lambda b, t: (t, b, 0))
    return pl.BlockSpec((tb, tt, D_MODEL), lambda b, t: (b, t, 0))


def _layer_spec(shape, layer):
    return pl.BlockSpec((None,) + shape, lambda *_: (layer, 0, 0), pipeline_mode=pl.Buffered(1))


def _ffn(h, gain, w_in, w_out, layer, bsz, seq, in_tm, out_tm, pre=None):
    tb = FFN_BATCH if (in_tm or out_tm) else 1
    tt = min(FFN_ROWS // tb, seq)
    rows = tb * tt
    as_bm = lambda a: a.reshape(bsz, seq, D_MODEL)
    out_shape = (seq, bsz, D_MODEL) if out_tm else (bsz, seq, D_MODEL)
    args, specs = [h if in_tm else as_bm(h)], [_token_spec(tb, tt, in_tm)]
    if pre is not None:
        args += [as_bm(pre[0]), pre[1].astype(BF16)]
        specs += [_token_spec(tb, tt, False), _const_spec((D_MODEL, D_MODEL))]
    args += [gain.reshape(1, D_MODEL), w_in, w_out]
    specs += [_const_spec((1, D_MODEL)), _layer_spec((D_MODEL, 2 * D_FF), layer),
              _layer_spec((D_FF, D_MODEL), layer)]
    out = pl.pallas_call(
        functools.partial(_ffn_body, pre=pre is not None, in_tm=in_tm, out_tm=out_tm),
        grid=(bsz // tb, seq // tt),
        in_specs=specs,
        out_specs=_token_spec(tb, tt, out_tm),
        out_shape=jax.ShapeDtypeStruct(out_shape, F32),
        scratch_shapes=[pltpu.VMEM((rows, D_FF), BF16)],
        compiler_params=_cparams(("parallel", "parallel")),
        name="ffn",
    )(*args)
    return out if out_tm else out.reshape(bsz * seq, D_MODEL)


def _s5_body(h_ref, g_ref, bmat_ref, cmat_ref, are_ref, aim_ref, d_ref, wglu_ref,
             o_ref, bu_ref, xb_ref, st_ref, y_ref, *, tc, nb):
    n_slab = D_MODEL // S5_SLAB
    half = S5_SLAB // S5_GROUP_CH * S5_STATE
    rows = tc * nb

    @pl.when(pl.program_id(0) == 0)
    def _():
        st_ref[...] = jnp.zeros_like(st_ref)

    h = h_ref[...].reshape(rows, D_MODEL)
    u = _rms(h, D_MODEL) * g_ref[...]
    ub = u.astype(BF16)

    def project(sl):
        bu_ref[sl % 2] = jnp.dot(ub[:, sl * S5_SLAB:(sl + 1) * S5_SLAB], bmat_ref[sl],
                                 preferred_element_type=F32)

    project(0)
    for sl in range(n_slab):
        if sl + 1 < n_slab:
            project(sl + 1)
        bu, xb = bu_ref.at[sl % 2], xb_ref.at[sl % 2]
        for wb in range(half // S5_LANES):
            re = slice(wb * S5_LANES, (wb + 1) * S5_LANES)
            im = slice(half + wb * S5_LANES, half + (wb + 1) * S5_LANES)
            ar = jnp.broadcast_to(are_ref[sl, :, re], (nb, S5_LANES))
            ai = jnp.broadcast_to(aim_ref[sl, :, re], (nb, S5_LANES))
            xr, xi = st_ref[sl, :, re], st_ref[sl, :, im]
            for t in range(tc):
                rs = slice(t * nb, (t + 1) * nb)
                xr, xi = ar * xr - ai * xi + bu[rs, re], ar * xi + ai * xr + bu[rs, im]
                xb[rs, re] = xr.astype(BF16)
                xb[rs, im] = xi.astype(BF16)
            st_ref[sl, :, re] = xr
            st_ref[sl, :, im] = xi
        y_ref[:, sl * S5_SLAB:(sl + 1) * S5_SLAB] = jnp.dot(
            xb_ref[sl % 2], cmat_ref[sl], preferred_element_type=F32)
    y = jax.nn.gelu(y_ref[...] + d_ref[...] * u)
    gate = jnp.dot(y.astype(BF16), wglu_ref[...], preferred_element_type=F32)
    o_ref[...] = (h + y * jax.nn.sigmoid(gate)).reshape(tc, nb, D_MODEL)


def _s5_params(a_re, a_im, log_dt, b_re, b_im, c_re, c_im):
    dt = jnp.exp(log_dt)[:, None]
    mag = jnp.exp(a_re * dt)
    ab_re = mag * jnp.cos(a_im * dt)
    ab_im = mag * jnp.sin(a_im * dt)
    den = a_re * a_re + a_im * a_im
    z_re = ((ab_re - 1.0) * a_re + ab_im * a_im) / den
    z_im = (ab_im * a_re - (ab_re - 1.0) * a_im) / den
    bb_re = z_re[..., None] * b_re - z_im[..., None] * b_im
    bb_im = z_re[..., None] * b_im + z_im[..., None] * b_re
    n_slab = D_MODEL // S5_SLAB
    gps = S5_GROUPS // n_slab
    eye = jnp.eye(gps, dtype=F32)

    def in_mat(bb):
        bb = bb.reshape(n_slab, gps, S5_STATE, S5_GROUP_CH)
        m = jnp.einsum('sgph,gk->sghkp', bb, eye)
        return m.reshape(n_slab, gps * S5_GROUP_CH, gps * S5_STATE)

    def out_mat(cc):
        cc = cc.reshape(n_slab, gps, S5_GROUP_CH, S5_STATE)
        m = jnp.einsum('sghp,gk->sgpkh', cc, eye)
        return m.reshape(n_slab, gps * S5_STATE, gps * S5_GROUP_CH)

    bmat = jnp.concatenate([in_mat(bb_re), in_mat(bb_im)], axis=2).astype(BF16)
    cmat = jnp.concatenate([out_mat(c_re), out_mat(-c_im)], axis=1).astype(BF16)
    are = ab_re.reshape(n_slab, 1, gps * S5_STATE)
    aim = ab_im.reshape(n_slab, 1, gps * S5_STATE)
    return bmat, cmat, are, aim


def _s5(h_tm, gain, a_re, a_im, log_dt, b_re, b_im, c_re, c_im, d_skip, w_glu, bsz, seq):
    tc = min(S5_TC, seq)
    n_slab = D_MODEL // S5_SLAB
    half = S5_SLAB // S5_GROUP_CH * S5_STATE
    bmat, cmat, are, aim = _s5_params(a_re, a_im, log_dt, b_re, b_im, c_re, c_im)
    blk = pl.BlockSpec((tc, bsz, D_MODEL), lambda i: (i, 0, 0))
    out = pl.pallas_call(
        functools.partial(_s5_body, tc=tc, nb=bsz),
        grid=(seq // tc,),
        in_specs=[blk,
                  _const_spec((1, D_MODEL)),
                  _const_spec((n_slab, S5_SLAB, 2 * half)),
                  _const_spec((n_slab, 2 * half, S5_SLAB)),
                  _const_spec((n_slab, 1, half)),
                  _const_spec((n_slab, 1, half)),
                  _const_spec((1, D_MODEL)),
                  _const_spec((D_MODEL, D_MODEL))],
        out_specs=blk,
        out_shape=jax.ShapeDtypeStruct((seq, bsz, D_MODEL), F32),
        scratch_shapes=[pltpu.VMEM((2, tc * bsz, 2 * half), F32),
                        pltpu.VMEM((2, tc * bsz, 2 * half), BF16),
                        pltpu.VMEM((n_slab, bsz, 2 * half), F32),
                        pltpu.VMEM((tc * bsz, D_MODEL), F32)],
        compiler_params=_cparams(("arbitrary",)),
        name="s5",
    )(h_tm, gain.reshape(1, D_MODEL), bmat, cmat, are, aim,
      d_skip.reshape(1, D_MODEL), w_glu.astype(BF16))
    return out


def _kv_body(h_ref, g_ref, w_ref, gk_ref, kc_ref, vc_ref, ks_ref, vs_ref, kw_ref, vw_ref, cs_ref,
             *, n_real):
    s = pl.program_id(1)
    gw = N_KV_HEADS * HEAD_DIM
    pw = N_KV_HEADS * LANES
    tiles = ks_ref.shape[2]
    row = lax.broadcasted_iota(jnp.int32, (LANES, QT), 0)
    key_hi = (lax.broadcasted_iota(jnp.int32, (LANES, QT), 1) >= SEL_LEN).astype(jnp.int32)
    farb_rows = ((row >= FARB_LANE0) & (row < FARB_LANE0 + 3)).astype(F32)
    ones = jnp.ones((QT, LANES), BF16)

    @pl.when(s < n_real)
    def _():
        hn = (_rms(h_ref[...], D_MODEL) * g_ref[...]).astype(BF16)
        kv = jnp.dot(hn, w_ref[...], preferred_element_type=F32)
        n_hb = kv.shape[0] // CMP_STRIDE
        for j in range(2 * gw // LANES):
            cs_ref[j] = kv[:, j * LANES:(j + 1) * LANES]
        for c_ref, j0 in ((kc_ref, 0), (vc_ref, gw // LANES)):
            for l in range(CMP_STRIDE):
                for j in range(gw // LANES):
                    col = l * gw + j * LANES
                    c_ref[:, col:col + LANES] = cs_ref[j0 + j, pl.ds(l, n_hb, stride=CMP_STRIDE), :].astype(BF16)
        for br, (k_ref, v_ref) in enumerate(((ks_ref, vs_ref), (kw_ref, vw_ref))):
            for g in range(N_KV_HEADS):
                lo = 2 * gw + br * 2 * pw + g * LANES
                k = _rms(kv[:, lo:lo + LANES], HEAD_DIM) * gk_ref[br]
                v = kv[:, lo + pw:lo + pw + LANES].astype(BF16)
                for t in range(tiles):
                    extra = farb_rows
                    if br == 0:
                        blk = 2 * (tiles * s + t) + key_hi
                        extra = extra + (row - SEL_LANE0 == blk).astype(F32)
                    k_ref[0, g, t] = (k[t * QT:(t + 1) * QT].T + extra).astype(BF16)
                    v_ref[0, g, t] = jnp.concatenate([v[t * QT:(t + 1) * QT], ones], axis=1)

    @pl.when(s == n_real)
    def _():
        pad_k = jnp.broadcast_to((row == PAD_LANE).astype(BF16), (N_KV_HEADS, tiles, LANES, QT))
        pad_v = jnp.zeros((N_KV_HEADS, tiles, QT, 2 * LANES), BF16)
        for k_ref, v_ref in ((ks_ref, vs_ref), (kw_ref, vw_ref)):
            k_ref[0] = pad_k
            v_ref[0] = pad_v


def _pad_heads(w, n_heads):
    k = w.shape[0]
    w = w.reshape(k, n_heads, HEAD_DIM)
    return jnp.pad(w, ((0, 0), (0, 0), (0, LANES - HEAD_DIM))).reshape(k, n_heads * LANES)


def _kv_proj(h, kv_norm, w_kv, k_norm_slc, k_norm_win, bsz, seq):
    gw = N_KV_HEADS * HEAD_DIM
    tiles = PAD_TILES
    rows = tiles * QT
    n_real = seq // rows
    nt = seq // QT
    wk = w_kv.reshape(D_MODEL, 2 * N_BRANCH, gw)
    w = jnp.concatenate([wk[:, 0], wk[:, 1]] + [_pad_heads(wk[:, j], N_KV_HEADS) for j in (2, 3, 4, 5)],
                        axis=1).astype(BF16)
    gk = jnp.pad(jnp.stack([k_norm_slc, k_norm_win]), ((0, 0), (0, LANES - HEAD_DIM)))
    gk = gk.reshape(2, 1, LANES)
    n = bsz * seq
    real = lambda s: jnp.minimum(s, n_real - 1)
    tok = lambda c: pl.BlockSpec((rows, c), lambda b, s: (b * n_real + real(s), 0))
    tile_spec = lambda r, c: pl.BlockSpec((1, N_KV_HEADS, tiles, r, c), lambda b, s: (b, 0, s, 0, 0))
    hb_spec = pl.BlockSpec((rows // CMP_STRIDE, CMP_STRIDE * gw), lambda b, s: (b * n_real + real(s), 0))
    dense = jax.ShapeDtypeStruct((n // CMP_STRIDE, CMP_STRIDE * gw), BF16)
    k_tiles = jax.ShapeDtypeStruct((bsz, N_KV_HEADS, nt + tiles, LANES, QT), BF16)
    v_tiles = jax.ShapeDtypeStruct((bsz, N_KV_HEADS, nt + tiles, QT, 2 * LANES), BF16)
    return pl.pallas_call(
        functools.partial(_kv_body, n_real=n_real),
        grid=(bsz, n_real + 1),
        in_specs=[tok(D_MODEL), _const_spec((1, D_MODEL)), _const_spec(w.shape),
                  _const_spec((2, 1, LANES))],
        out_specs=[hb_spec, hb_spec, tile_spec(LANES, QT), tile_spec(QT, 2 * LANES),
                   tile_spec(LANES, QT), tile_spec(QT, 2 * LANES)],
        out_shape=[dense, dense, k_tiles, v_tiles, k_tiles, v_tiles],
        scratch_shapes=[pltpu.VMEM((2 * gw // LANES, rows, LANES), F32)],
        compiler_params=_cparams(("parallel", "arbitrary")),
        name="kv_proj",
    )(h, kv_norm.reshape(1, D_MODEL), w, gk)


def _cmp_body(xk_ref, xv_ref, w1k_ref, w1v_ref, bk_ref, bv_ref, w2k_ref, w2v_ref, gk_ref,
              kc_ref, vc_ref, *, nh):
    ab_k = jnp.dot(xk_ref[0], w1k_ref[...], preferred_element_type=F32)
    ab_v = jnp.dot(xv_ref[0], w1v_ref[...], preferred_element_type=F32)
    for g in range(N_KV_HEADS):
        outs = []
        for ab_all, b_ref, w2_ref in ((ab_k, bk_ref, w2k_ref), (ab_v, bv_ref, w2v_ref)):
            ab = ab_all[:, g * 2 * CMP_HIDDEN:(g + 1) * 2 * CMP_HIDDEN]
            hid = ab[:, :CMP_HIDDEN] + pltpu.roll(ab[:, CMP_HIDDEN:], nh - 1, 0) + b_ref[...]
            hid = jax.nn.gelu(hid).astype(BF16)
            outs.append(jnp.dot(hid, w2_ref[...], preferred_element_type=F32))
        k = _rms(outs[0], HEAD_DIM) * gk_ref[...]
        kc_ref[0, g] = k.T.astype(BF16)
        vc_ref[0, g] = outs[1].astype(BF16)


def _compress(kc_raw, vc_raw, k_norm_cmp, pos_k, pos_v, k_w1, k_w2, v_w1, v_w2, bsz, seq):
    nh = seq // CMP_STRIDE
    hb = CMP_STRIDE * N_KV_HEADS * HEAD_DIM
    eye = jnp.eye(N_KV_HEADS, dtype=F32)

    def halfblocks(x):
        return x.reshape(bsz, nh, hb)

    def w1cat(w1):
        w = w1.reshape(2, CMP_STRIDE, HEAD_DIM, CMP_HIDDEN)
        w = jnp.concatenate([w[0], w[1]], axis=-1)
        w = jnp.einsum('ldj,gh->lgdhj', w, eye)
        return w.reshape(hb, N_KV_HEADS * 2 * CMP_HIDDEN).astype(BF16)

    def w2pad(w2):
        return jnp.pad(w2, ((0, 0), (0, LANES - HEAD_DIM))).astype(BF16)

    bias = lambda pos, w1: jnp.einsum('ld,ldh->h', pos, w1,
                                      precision=lax.Precision.HIGHEST).reshape(1, CMP_HIDDEN)
    gk = jnp.pad(k_norm_cmp, (0, LANES - HEAD_DIM)).reshape(1, LANES)
    xspec = pl.BlockSpec((1, nh, hb), lambda b: (b, 0, 0))
    ospec = pl.BlockSpec((1, N_KV_HEADS, LANES, nh), lambda b: (b, 0, 0, 0))
    vspec = pl.BlockSpec((1, N_KV_HEADS, nh, LANES), lambda b: (b, 0, 0, 0))
    w1_shape = (hb, N_KV_HEADS * 2 * CMP_HIDDEN)
    return pl.pallas_call(
        functools.partial(_cmp_body, nh=nh),
        grid=(bsz,),
        in_specs=[xspec, xspec, _const_spec(w1_shape), _const_spec(w1_shape),
                  _const_spec((1, CMP_HIDDEN)), _const_spec((1, CMP_HIDDEN)),
                  _const_spec((CMP_HIDDEN, LANES)), _const_spec((CMP_HIDDEN, LANES)),
                  _const_spec((1, LANES))],
        out_specs=[ospec, vspec],
        out_shape=[jax.ShapeDtypeStruct((bsz, N_KV_HEADS, LANES, nh), BF16),
                   jax.ShapeDtypeStruct((bsz, N_KV_HEADS, nh, LANES), BF16)],
        compiler_params=_cparams(("parallel",)),
        name="kv_compress",
    )(halfblocks(kc_raw), halfblocks(vc_raw), w1cat(k_w1), w1cat(v_w1),
      bias(pos_k, k_w1), bias(pos_v, v_w1), w2pad(k_w2), w2pad(v_w2), gk)


def _qproj_body(h_ref, g_ref, w_ref, qsc_ref, qc_ref, q_ref, gate_ref):
    u = (_rms(h_ref[...], D_MODEL) * g_ref[...]).astype(BF16)
    qg = jnp.dot(u, w_ref[...], preferred_element_type=F32)
    for hh in range(N_HEADS):
        sl = slice(hh * LANES, (hh + 1) * LANES)
        q_ref[:, sl] = (_rms(qg[:, sl], HEAD_DIM) * qsc_ref[:, sl] + qc_ref[:, sl]).astype(BF16)
    gate_ref[...] = jax.nn.sigmoid(qg[:, N_HEADS * LANES:])


def _qproj(h, mix_gain, w_qg, q_norm, rel_bias, bsz, seq):
    rows = min(QP_ROWS, seq)
    n = bsz * seq
    nq = N_HEADS * HEAD_DIM
    wq = jnp.concatenate([_pad_heads(w_qg[:, :nq], N_HEADS),
                          jnp.pad(w_qg[:, nq:], ((0, 0), (0, LANES - N_BRANCH * N_HEADS)))],
                         axis=1).astype(BF16)
    qsc = jnp.tile(jnp.pad(q_norm * (ATTN_SCALE * LOG2E), (0, LANES - HEAD_DIM)), N_HEADS)
    qsc = qsc.reshape(1, N_HEADS * LANES)
    far = _split3(rel_bias[NUM_BUCKETS - 1] * LOG2E)
    qc = jnp.zeros((N_HEADS, LANES), F32)
    for t, term in enumerate(far):
        qc = qc.at[:, FARB_LANE0 + t].set(term.astype(F32))
    qc = qc.at[:, PAD_LANE].set(NEG_INF).reshape(1, N_HEADS * LANES)
    tok = lambda c: pl.BlockSpec((rows, c), lambda i: (i, 0))
    return pl.pallas_call(
        _qproj_body,
        grid=(n // rows,),
        in_specs=[tok(D_MODEL), _const_spec((1, D_MODEL)), _const_spec(wq.shape),
                  _const_spec((1, N_HEADS * LANES)), _const_spec((1, N_HEADS * LANES))],
        out_specs=[tok(N_HEADS * LANES), tok(LANES)],
        out_shape=[jax.ShapeDtypeStruct((n, N_HEADS * LANES), BF16),
                   jax.ShapeDtypeStruct((n, LANES), F32)],
        compiler_params=_cparams(("parallel",)),
        name="q_proj",
    )(h, mix_gain.reshape(1, D_MODEL), wq, qsc, qc)


def _attn_body(q_ref, gate_ref, kc_ref, vc_ref, ks_ref, vs_ref, kw_ref, vw_ref,
               ctab_ref, wtab_ref, ovlt_ref, gexp_ref,
               o_ref, s_scr, sn_scr, m_scr, qf_scr, acc_scr, out_scr, *, n_sel, n_top, nt):
    i = pl.program_id(1)
    rq = Q_PER_KV * QT
    n_win = WIN // QT + 1
    lane = lax.broadcasted_iota(jnp.int32, (QT, LANES), 1)
    sel_lane = (lane >= SEL_LANE0) & (lane < SEL_LANE0 + n_sel)
    far_cut = sel_lane & (lane - SEL_LANE0 >= 2 * (i - 1))
    sel_lane4 = jnp.concatenate([sel_lane] * Q_PER_KV, axis=0)
    blk = lax.broadcasted_iota(jnp.int32, (n_sel, QT), 0)
    blkf = blk.astype(F32)
    qpos = i * QT + lax.broadcasted_iota(jnp.int32, (n_sel, QT), 1)
    cur = lax.shift_right_arithmetic(qpos, int(math.log2(SEL_LEN)))
    forced = (blk == 0) | (blk == cur) | (blk == cur - 1)
    causal = blk * SEL_LEN <= qpos
    tile = lambda t: jnp.where(t < 0, nt, t)

    low = lane < HEAD_DIM

    def gate_tile(b, tl):
        ca, cb = 2 * tl * N_BRANCH + b, (2 * tl + 1) * N_BRANCH + b
        bc = lambda c: jnp.broadcast_to(gate_ref[:, c:c + 1], (QT, LANES))
        return jnp.where(low, bc(ca), bc(cb))

    def merged(o, g, pr):
        return o[2 * pr * QT:(2 * pr + 1) * QT] + pltpu.roll(o[(2 * pr + 1) * QT:(2 * pr + 2) * QT], HEAD_DIM, 1)

    def normalise(pv):
        return pv[:, :LANES] * (1.0 / pv[:, LANES:])

    def tab(g, lo, width):
        return wtab_ref[Q_PER_KV * g:Q_PER_KV * (g + 1), :, lo:lo + width].reshape(rq, width)

    def queries(g):
        return jnp.concatenate([q_ref[:, (Q_PER_KV * g + r) * LANES:(Q_PER_KV * g + r + 1) * LANES]
                                for r in range(Q_PER_KV)], axis=0)

    cmp_p, pv_cs, picked, win = {}, {}, {}, {}
    w_tiles = [tile(i - (n_win - 1) + j) for j in range(n_win)]

    def cmp_softmax(g):
        sc = jnp.dot(queries(g), kc_ref[0, g], preferred_element_type=F32)
        sc = sc + ctab_ref[Q_PER_KV * g:Q_PER_KV * (g + 1)].reshape(rq, sc.shape[-1])
        e = jnp.exp2(sc - jnp.max(sc, axis=-1, keepdims=True))
        row_ok = i * QT + lax.broadcasted_iota(jnp.int32, (QT, sc.shape[-1]), 0) >= CMP_LEN - 1
        cmp_p[g] = jnp.where(jnp.concatenate([row_ok] * Q_PER_KV, axis=0),
                             e * (1.0 / jnp.sum(e, axis=-1, keepdims=True)), 0.0)

    def cmp_out(g):
        p = cmp_p[g]
        pv_cs[g] = jnp.dot(p.astype(BF16), vc_ref[0, g], preferred_element_type=F32)
        psum = p[0:QT] + p[QT:2 * QT] + p[2 * QT:3 * QT] + p[3 * QT:4 * QT]
        ps_t = sum(jnp.dot(ovlt_ref[...], t, preferred_element_type=F32) for t in _split3(psum.T))
        cmp_p[g] = jnp.where(forced, BIG, jnp.where(causal, ps_t[SEL_LANE0:SEL_LANE0 + n_sel], -BIG))

    def select(g):
        score = cmp_p[g]
        pick = jnp.zeros((n_sel, QT), F32)
        for _ in range(n_top):
            mx = jnp.max(score, axis=0, keepdims=True)
            first = jnp.min(jnp.where(score == mx, blkf, float(LANES)), axis=0, keepdims=True)
            hit = blkf == first
            pick = jnp.where(hit, 1.0, pick)
            score = jnp.where(hit, -jnp.inf, score)
        picked[g] = pick

    def win_logits(g):
        k_w = jnp.concatenate([kw_ref[0, g, t] for t in w_tiles], axis=1)
        s_w = jnp.dot(queries(g), k_w, preferred_element_type=F32)
        win[g] = [s_w[:, :QT] + tab(g, 0, QT), s_w[:, QT:(n_win - 2) * QT],
                  s_w[:, (n_win - 2) * QT:] + tab(g, QT, 2 * QT)]

    def win_softmax(g):
        mw = jnp.max(jnp.concatenate(win[g], axis=1), axis=-1, keepdims=True)
        win[g] = jnp.concatenate([jnp.exp2(t - mw).astype(BF16) for t in win[g]], axis=1)

    def win_pv(g):
        v_w = jnp.concatenate([vw_ref[0, g, t] for t in w_tiles], axis=0)
        win[g] = normalise(jnp.dot(win[g], v_w, preferred_element_type=F32))

    def win_out(g):
        o_w = win[g]
        for pr in range(Q_PER_KV // 2):
            tl = Q_PER_KV * g // 2 + pr
            out_scr[:, tl * LANES:(tl + 1) * LANES] = (gate_tile(0, tl) * merged(pv_cs[g], g, pr)
                                                       + gate_tile(2, tl) * merged(o_w, g, pr))

    def near_logits(g):
        q = queries(g)
        selb_t = jnp.where(picked[g] == 0.0, NEG_INF, 0.0)
        selb = jnp.concatenate([jnp.zeros((SEL_LANE0, QT), F32), selb_t,
                                jnp.zeros((LANES - SEL_LANE0 - n_sel, QT), F32)], axis=0).T
        selb_far = jnp.where(far_cut, NEG_INF, selb)
        q_near = jnp.where(sel_lane4, jnp.concatenate([selb.astype(BF16)] * Q_PER_KV, axis=0), q)
        qf_scr[g] = jnp.where(sel_lane4, jnp.concatenate([selb_far.astype(BF16)] * Q_PER_KV, axis=0), q)
        k_n = jnp.concatenate([ks_ref[0, g, tile(i - 1)], ks_ref[0, g, i]], axis=1)
        s_n = jnp.dot(q_near, k_n, preferred_element_type=F32) + tab(g, QT, 2 * QT)
        sn_scr[g] = s_n
        m_scr[g] = jnp.maximum(s_n[:, :QT], s_n[:, QT:])

    stages = (cmp_softmax, win_logits, cmp_out, win_softmax, select, win_pv, near_logits, win_out)
    for step in range(N_KV_HEADS + len(stages) - 1):
        for k, stage in enumerate(stages):
            if 0 <= step - k < N_KV_HEADS:
                stage(step - k)

    n_chunks = (jnp.maximum(i - 1, 0) + FAR_TILES - 1) // FAR_TILES

    def chunk_loop(body):
        def pair(p, carry):
            body(2 * p)
            body(2 * p + 1)
            return carry

        lax.fori_loop(0, n_chunks // 2, pair, 0)

        @pl.when(n_chunks % 2 == 1)
        def _():
            body(n_chunks - 1)

    def far_logits(c):
        for g in range(N_KV_HEADS):
            k_f = jnp.concatenate([ks_ref[0, g, FAR_TILES * c + t] for t in range(FAR_TILES)], axis=1)
            s = jnp.dot(qf_scr[g], k_f, preferred_element_type=F32)
            s_scr[g, c] = s
            m = m_scr[g]
            for t in range(FAR_TILES):
                m = jnp.maximum(m, s[:, t * QT:(t + 1) * QT])
            m_scr[g] = m

    chunk_loop(far_logits)

    near_p = {}

    def near_softmax(g):
        m = jnp.max(m_scr[g], axis=-1, keepdims=True)
        m_scr[g] = jnp.broadcast_to(m, (rq, QT))
        near_p[g] = jnp.exp2(sn_scr[g] - m).astype(BF16)

    def near_pv(g):
        v_n = jnp.concatenate([vs_ref[0, g, tile(i - 1)], vs_ref[0, g, i]], axis=0)
        acc_scr[g] = jnp.dot(near_p[g], v_n, preferred_element_type=F32)

    for step in range(N_KV_HEADS + 1):
        if step < N_KV_HEADS:
            near_softmax(step)
        if step > 0:
            near_pv(step - 1)

    def far_pv(c):
        for g in range(N_KV_HEADS):
            m = jnp.concatenate([m_scr[g]] * FAR_TILES, axis=1)
            p = jnp.exp2(s_scr[g, c] - m).astype(BF16)
            v_f = jnp.concatenate([vs_ref[0, g, FAR_TILES * c + t] for t in range(FAR_TILES)], axis=0)
            acc_scr[g] += jnp.dot(p, v_f, preferred_element_type=F32)

    chunk_loop(far_pv)

    for g in range(N_KV_HEADS):
        o_s = normalise(acc_scr[g])
        for pr in range(Q_PER_KV // 2):
            tl = Q_PER_KV * g // 2 + pr
            o = out_scr[:, tl * LANES:(tl + 1) * LANES] + gate_tile(1, tl) * merged(o_s, g, pr)
            o_ref[:, tl * LANES:(tl + 1) * LANES] = o.astype(BF16)


def _rel_bucket(dist):
    n = jnp.maximum(dist, 0)
    max_exact = NUM_BUCKETS // 2
    logv = jnp.log(jnp.maximum(n, 1).astype(F32) / max_exact) / math.log(MAX_DISTANCE / max_exact)
    large = jnp.minimum(max_exact + (logv * (NUM_BUCKETS - max_exact)).astype(jnp.int32), NUM_BUCKETS - 1)
    return jnp.where(n < max_exact, n, large)


def _bias_tables(rel_bias, seq):
    nt = seq // QT
    nc = seq // CMP_STRIDE
    cpt = QT // CMP_STRIDE
    rb = rel_bias * LOG2E

    def f(dist):
        onehot = (_rel_bucket(dist)[..., None] == jnp.arange(NUM_BUCKETS)).astype(F32)
        return jnp.einsum('...k,kh->h...', onehot, rb, precision=lax.Precision.HIGHEST)

    c_rel = jnp.arange(-cpt * (nt - 1), nc)
    dc = jnp.arange(QT)[:, None] - (c_rel * CMP_STRIDE + CMP_LEN - 1)[None, :]
    t0 = jnp.where((dc >= 0)[None], f(dc), NEG_INF)
    ctab = jnp.stack([t0[:, :, cpt * (nt - 1 - i):cpt * (nt - 1 - i) + nc] for i in range(nt)], axis=1)
    ctab = ctab.reshape(N_HEADS, seq, nc)
    far = rb[NUM_BUCKETS - 1].reshape(N_HEADS, 1, 1)
    dq = jnp.arange(QT)[:, None] - jnp.arange(QT)[None, :]
    d0, d3, d4 = WIN + dq, QT + dq, dq
    wtab = jnp.concatenate([jnp.where((d0 < WIN)[None], f(d0) - far, NEG_INF), f(d3) - far,
                            jnp.where((d4 >= 0)[None], f(d4) - far, NEG_INF)], axis=2)
    return ctab, wtab


def _overlap_matrix_t(seq):
    nc = seq // CMP_STRIDE
    n_sel = seq // SEL_LEN
    c_start = jnp.arange(nc) * CMP_STRIDE
    j_start = jnp.arange(n_sel) * SEL_LEN
    ov = jnp.clip(jnp.minimum(c_start[None, :] + CMP_LEN, j_start[:, None] + SEL_LEN)
                  - jnp.maximum(c_start[None, :], j_start[:, None]), 0, None)
    ov = ov.astype(F32) / CMP_LEN
    return jnp.pad(ov, ((SEL_LANE0, LANES - SEL_LANE0 - n_sel), (0, 0))).astype(BF16)


def _gate_expansion():
    col = jnp.arange(LANES).reshape(1, LANES, 1)
    head = (jnp.arange(D_MODEL) // HEAD_DIM).reshape(1, 1, D_MODEL)
    br = jnp.arange(N_BRANCH).reshape(N_BRANCH, 1, 1)
    return (col == head * N_BRANCH + br).astype(BF16)


def _attention(q, gates, kv, rel_bias, bsz, seq):
    kc, vc, ks, vs, kw, vw = kv
    nt = seq // QT
    nc = seq // CMP_STRIDE
    n_sel = seq // SEL_LEN
    n_top = min(SEL_TOPN, n_sel)
    n_chunk = max(-(-(nt - 2) // FAR_TILES), 1)
    assert n_chunk * FAR_TILES <= nt
    nts = nt + PAD_TILES
    ctab, wtab = _bias_tables(rel_bias, seq)
    rq = Q_PER_KV * QT
    per_batch = lambda shape: pl.BlockSpec((1,) + shape, lambda b, i: (b,) + (0,) * len(shape),
                                           pipeline_mode=pl.Buffered(1))
    tok = lambda c: pl.BlockSpec((QT, c), lambda b, i: (b * nt + i, 0))
    return pl.pallas_call(
        functools.partial(_attn_body, n_sel=n_sel, n_top=n_top, nt=nt),
        grid=(bsz, nt),
        in_specs=[tok(N_HEADS * LANES), tok(LANES),
                  per_batch((N_KV_HEADS, LANES, nc)),
                  per_batch((N_KV_HEADS, nc, LANES)),
                  per_batch((N_KV_HEADS, nts, LANES, QT)),
                  per_batch((N_KV_HEADS, nts, QT, 2 * LANES)),
                  per_batch((N_KV_HEADS, nts, LANES, QT)),
                  per_batch((N_KV_HEADS, nts, QT, 2 * LANES)),
                  pl.BlockSpec((N_HEADS, QT, nc), lambda b, i: (0, i, 0)),
                  _const_spec((N_HEADS, QT, 3 * QT)),
                  _const_spec((LANES, nc)),
                  _const_spec((N_BRANCH, LANES, D_MODEL))],
        out_specs=tok(D_MODEL),
        out_shape=jax.ShapeDtypeStruct((bsz * seq, D_MODEL), BF16),
        scratch_shapes=[pltpu.VMEM((N_KV_HEADS, n_chunk, rq, FAR_TILES * QT), F32),
                        pltpu.VMEM((N_KV_HEADS, rq, 2 * QT), F32),
                        pltpu.VMEM((N_KV_HEADS, rq, QT), F32),
                        pltpu.VMEM((N_KV_HEADS, rq, LANES), BF16),
                        pltpu.VMEM((N_KV_HEADS, rq, 2 * LANES), F32),
                        pltpu.VMEM((QT, D_MODEL), F32)],
        compiler_params=_cparams(("parallel", "arbitrary")),
        name="nsa_attn",
    )(q, gates, kc, vc, ks, vs, kw, vw, ctab, wtab, _overlap_matrix_t(seq), _gate_expansion())


def kernel(x, rel_bias, ffn1_norm, ffn1_w_in, ffn1_w_out, mix_norm, ffn2_norm, ffn2_w_in, ffn2_w_out,
           s5_a_re, s5_a_im, s5_log_dt, s5_b_re, s5_b_im, s5_c_re, s5_c_im, s5_d, s5_w_glu,
           kv_norm, w_kv, k_norm_cmp, k_norm_slc, k_norm_win, cmp_pos_k, cmp_pos_v,
           cmp_k_w1, cmp_k_w2, cmp_v_w1, cmp_v_w2, w_qg, q_norm, w_o):
    bsz, seq, _ = x.shape
    depth = ffn1_norm.shape[0]
    n_a = s5_a_re.shape[0]
    h = x.reshape(bsz * seq, D_MODEL)
    kv = None
    ffn1_w_in, ffn1_w_out, ffn2_w_in, ffn2_w_out = (
        w.astype(BF16) for w in (ffn1_w_in, ffn1_w_out, ffn2_w_in, ffn2_w_out))
    for layer in range(depth):
        s5_layer = layer < n_a
        h = _ffn(h, ffn1_norm[layer], ffn1_w_in, ffn1_w_out, layer, bsz, seq,
                 in_tm=False, out_tm=s5_layer)
        pre = None
        if s5_layer:
            a = layer
            h = _s5(h, mix_norm[layer], s5_a_re[a], s5_a_im[a], s5_log_dt[a], s5_b_re[a], s5_b_im[a],
                    s5_c_re[a], s5_c_im[a], s5_d[a], s5_w_glu[a], bsz, seq)
        else:
            b = layer - n_a
            q, gates = _qproj(h, mix_norm[layer], w_qg[b], q_norm[b], rel_bias, bsz, seq)
            pre = (_attention(q, gates, kv, rel_bias, bsz, seq), w_o[b])
        h = _ffn(h, ffn2_norm[layer], ffn2_w_in, ffn2_w_out, layer, bsz, seq,
                 in_tm=s5_layer, out_tm=False, pre=pre)
        if layer == n_a - 1:
            kc_raw, vc_raw, ks, vs, kw, vw = _kv_proj(h, kv_norm, w_kv, k_norm_slc, k_norm_win, bsz, seq)
            kc, vc = _compress(kc_raw, vc_raw, k_norm_cmp, cmp_pos_k, cmp_pos_v,
                               cmp_k_w1, cmp_k_w2, cmp_v_w1, cmp_v_w2, bsz, seq)
            kv = (kc, vc, ks, vs, kw, vw)
    return h.reshape(bsz, seq, D_MODEL)
```

```python
import functools
import math

import jax
import jax.numpy as jnp
from jax import lax
from jax.experimental import pallas as pl
from jax.experimental.pallas import tpu as pltpu

F32 = jnp.float32
BF16 = jnp.bfloat16

D_MODEL = 1024
D_FF = 2816
RMS_EPS = 1e-6
S5_GROUP_CH = 16
S5_GROUPS = D_MODEL // S5_GROUP_CH
S5_STATE = 64
N_HEADS = 16
HEAD_DIM = 64
N_KV_HEADS = 4
Q_PER_KV = N_HEADS // N_KV_HEADS
CMP_LEN = 32
CMP_STRIDE = 16
CMP_HIDDEN = 2 * HEAD_DIM
SEL_LEN = 64
SEL_TOPN = 8
WIN = 512
N_BRANCH = 3
ATTN_SCALE = HEAD_DIM ** -0.5
NUM_BUCKETS = 32
MAX_DISTANCE = 128
NEG_INF = -1e30
BIG = 1e9

LANES = 128
MXU_DIM = 256
VMEM_LIMIT = 56 * 1024 * 1024

FFN_ROWS = 1024
FFN_BATCH = 8
FF_CHUNK = MXU_DIM
S5_TC = 16
S5_SLAB = MXU_DIM
S5_LANES = 256
QP_ROWS = 1024
QT = 128
FAR_TILES = 4
PAD_TILES = WIN // QT
LOG2E = math.log2(math.e)

SEL_LANE0 = HEAD_DIM
FARB_LANE0 = 96
PAD_LANE = 99


def _rms(x, n):
    ss = jnp.sum(x * x, axis=-1, keepdims=True)
    return x * lax.rsqrt(ss * (1.0 / n) + RMS_EPS)


def _cparams(sem):
    return pltpu.CompilerParams(dimension_semantics=sem, vmem_limit_bytes=VMEM_LIMIT)


def _const_spec(shape):
    nd = len(shape)
    return pl.BlockSpec(shape, lambda *_: (0,) * nd, pipeline_mode=pl.Buffered(1))


def _split3(x):
    x1 = x.astype(BF16)
    r1 = x - x1.astype(F32)
    x2 = r1.astype(BF16)
    x3 = (r1 - x2.astype(F32)).astype(BF16)
    return x1, x2, x3


def _ffn_body(*refs, pre, in_tm, out_tm):
    if pre:
        x_ref, po_ref, pw_ref, g_ref, win_ref, wout_ref, o_ref, act_ref = refs
    else:
        x_ref, g_ref, win_ref, wout_ref, o_ref, act_ref = refs
    rows = act_ref.shape[0]
    x = (jnp.swapaxes(x_ref[...], 0, 1) if in_tm else x_ref[...]).reshape(rows, D_MODEL)
    if pre:
        x = x + jnp.dot(po_ref[...].reshape(rows, D_MODEL), pw_ref[...], preferred_element_type=F32)
    xn = (_rms(x, D_MODEL) * g_ref[...]).astype(BF16)
    for c in range(D_FF // FF_CHUNK):
        lo = c * FF_CHUNK
        a = jnp.dot(xn, win_ref[:, lo:lo + FF_CHUNK], preferred_element_type=F32)
        b = jnp.dot(xn, win_ref[:, D_FF + lo:D_FF + lo + FF_CHUNK], preferred_element_type=F32)
        act_ref[:, lo:lo + FF_CHUNK] = (a * jax.nn.sigmoid(a) * b).astype(BF16)
    y = jnp.dot(act_ref[...], wout_ref[...], preferred_element_type=F32)
    res = x + 0.5 * y
    nb = o_ref.shape[1] if out_tm else o_ref.shape[0]
    res = res.reshape(nb, rows // nb, D_MODEL)
    o_ref[...] = jnp.swapaxes(res, 0, 1) if out_tm else res


def _token_spec(tb, tt, time_major):
    if time_major:
        return pl.BlockSpec((tt, tb, D_MODEL), lambda b, t: (t, b, 0))
    return pl.BlockSpec((tb, tt, D_MODEL), lambda b, t: (b, t, 0))


def _layer_spec(shape, layer):
    return pl.BlockSpec((None,) + shape, lambda *_: (layer, 0, 0), pipeline_mode=pl.Buffered(1))


def _ffn(h, gain, w_in, w_out, layer, bsz, seq, in_tm, out_tm, pre=None):
    tb = FFN_BATCH if (in_tm or out_tm) else 1
    tt = min(FFN_ROWS // tb, seq)
    rows = tb * tt
    as_bm = lambda a: a.reshape(bsz, seq, D_MODEL)
    out_shape = (seq, bsz, D_MODEL) if out_tm else (bsz, seq, D_MODEL)
    args, specs = [h if in_tm else as_bm(h)], [_token_spec(tb, tt, in_tm)]
    if pre is not None:
        args += [as_bm(pre[0]), pre[1].astype(BF16)]
        specs += [_token_spec(tb, tt, False), _const_spec((D_MODEL, D_MODEL))]
    args += [gain.reshape(1, D_MODEL), w_in, w_out]
    specs += [_const_spec((1, D_MODEL)), _layer_spec((D_MODEL, 2 * D_FF), layer),
              _layer_spec((D_FF, D_MODEL), layer)]
    out = pl.pallas_call(
        functools.partial(_ffn_body, pre=pre is not None, in_tm=in_tm, out_tm=out_tm),
        grid=(bsz // tb, seq // tt),
        in_specs=specs,
        out_specs=_token_spec(tb, tt, out_tm),
        out_shape=jax.ShapeDtypeStruct(out_shape, F32),
        scratch_shapes=[pltpu.VMEM((rows, D_FF), BF16)],
        compiler_params=_cparams(("parallel", "parallel")),
        name="ffn",
    )(*args)
    return out if out_tm else out.reshape(bsz * seq, D_MODEL)


def _s5_body(h_ref, g_ref, bmat_ref, cmat_ref, are_ref, aim_ref, d_ref, wglu_ref,
             o_ref, bu_ref, xb_ref, st_ref, y_ref, *, tc, nb):
    n_slab = D_MODEL // S5_SLAB
    half = S5_SLAB // S5_GROUP_CH * S5_STATE
    rows = tc * nb

    @pl.when(pl.program_id(0) == 0)
    def _():
        st_ref[...] = jnp.zeros_like(st_ref)

    h = h_ref[...].reshape(rows, D_MODEL)
    u = _rms(h, D_MODEL) * g_ref[...]
    ub = u.astype(BF16)

    def project(sl):
        bu_ref[sl % 2] = jnp.dot(ub[:, sl * S5_SLAB:(sl + 1) * S5_SLAB], bmat_ref[sl],
                                 preferred_element_type=F32)

    project(0)
    for sl in range(n_slab):
        if sl + 1 < n_slab:
            project(sl + 1)
        bu, xb = bu_ref.at[sl % 2], xb_ref.at[sl % 2]
        for wb in range(half // S5_LANES):
            re = slice(wb * S5_LANES, (wb + 1) * S5_LANES)
            im = slice(half + wb * S5_LANES, half + (wb + 1) * S5_LANES)
            ar = jnp.broadcast_to(are_ref[sl, :, re], (nb, S5_LANES))
            ai = jnp.broadcast_to(aim_ref[sl, :, re], (nb, S5_LANES))
            xr, xi = st_ref[sl, :, re], st_ref[sl, :, im]
            for t in range(tc):
                rs = slice(t * nb, (t + 1) * nb)
                xr, xi = ar * xr - ai * xi + bu[rs, re], ar * xi + ai * xr + bu[rs, im]
                xb[rs, re] = xr.astype(BF16)
                xb[rs, im] = xi.astype(BF16)
            st_ref[sl, :, re] = xr
            st_ref[sl, :, im] = xi
        y_ref[:, sl * S5_SLAB:(sl + 1) * S5_SLAB] = jnp.dot(
            xb_ref[sl % 2], cmat_ref[sl], preferred_element_type=F32)
    y = jax.nn.gelu(y_ref[...] + d_ref[...] * u)
    gate = jnp.dot(y.astype(BF16), wglu_ref[...], preferred_element_type=F32)
    o_ref[...] = (h + y * jax.nn.sigmoid(gate)).reshape(tc, nb, D_MODEL)


def _s5_params(a_re, a_im, log_dt, b_re, b_im, c_re, c_im):
    dt = jnp.exp(log_dt)[:, None]
    mag = jnp.exp(a_re * dt)
    ab_re = mag * jnp.cos(a_im * dt)
    ab_im = mag * jnp.sin(a_im * dt)
    den = a_re * a_re + a_im * a_im
    z_re = ((ab_re - 1.0) * a_re + ab_im * a_im) / den
    z_im = (ab_im * a_re - (ab_re - 1.0) * a_im) / den
    bb_re = z_re[..., None] * b_re - z_im[..., None] * b_im
    bb_im = z_re[..., None] * b_im + z_im[..., None] * b_re
    n_slab = D_MODEL // S5_SLAB
    gps = S5_GROUPS // n_slab
    eye = jnp.eye(gps, dtype=F32)

    def in_mat(bb):
        bb = bb.reshape(n_slab, gps, S5_STATE, S5_GROUP_CH)
        m = jnp.einsum('sgph,gk->sghkp', bb, eye)
        return m.reshape(n_slab, gps * S5_GROUP_CH, gps * S5_STATE)

    def out_mat(cc):
        cc = cc.reshape(n_slab, gps, S5_GROUP_CH, S5_STATE)
        m = jnp.einsum('sghp,gk->sgpkh', cc, eye)
        return m.reshape(n_slab, gps * S5_STATE, gps * S5_GROUP_CH)

    bmat = jnp.concatenate([in_mat(bb_re), in_mat(bb_im)], axis=2).astype(BF16)
    cmat = jnp.concatenate([out_mat(c_re), out_mat(-c_im)], axis=1).astype(BF16)
    are = ab_re.reshape(n_slab, 1, gps * S5_STATE)
    aim = ab_im.reshape(n_slab, 1, gps * S5_STATE)
    return bmat, cmat, are, aim


def _s5(h_tm, gain, a_re, a_im, log_dt, b_re, b_im, c_re, c_im, d_skip, w_glu, bsz, seq):
    tc = min(S5_TC, seq)
    n_slab = D_MODEL // S5_SLAB
    half = S5_SLAB // S5_GROUP_CH * S5_STATE
    bmat, cmat, are, aim = _s5_params(a_re, a_im, log_dt, b_re, b_im, c_re, c_im)
    blk = pl.BlockSpec((tc, bsz, D_MODEL), lambda i: (i, 0, 0))
    out = pl.pallas_call(
        functools.partial(_s5_body, tc=tc, nb=bsz),
        grid=(seq // tc,),
        in_specs=[blk,
                  _const_spec((1, D_MODEL)),
                  _const_spec((n_slab, S5_SLAB, 2 * half)),
                  _const_spec((n_slab, 2 * half, S5_SLAB)),
                  _const_spec((n_slab, 1, half)),
                  _const_spec((n_slab, 1, half)),
                  _const_spec((1, D_MODEL)),
                  _const_spec((D_MODEL, D_MODEL))],
        out_specs=blk,
        out_shape=jax.ShapeDtypeStruct((seq, bsz, D_MODEL), F32),
        scratch_shapes=[pltpu.VMEM((2, tc * bsz, 2 * half), F32),
                        pltpu.VMEM((2, tc * bsz, 2 * half), BF16),
                        pltpu.VMEM((n_slab, bsz, 2 * half), F32),
                        pltpu.VMEM((tc * bsz, D_MODEL), F32)],
        compiler_params=_cparams(("arbitrary",)),
        name="s5",
    )(h_tm, gain.reshape(1, D_MODEL), bmat, cmat, are, aim,
      d_skip.reshape(1, D_MODEL), w_glu.astype(BF16))
    return out


def _kv_body(h_ref, g_ref, w_ref, gk_ref, kc_ref, vc_ref, ks_ref, vs_ref, kw_ref, vw_ref, cs_ref,
             *, n_real):
    s = pl.program_id(1)
    gw = N_KV_HEADS * HEAD_DIM
    pw = N_KV_HEADS * LANES
    tiles = ks_ref.shape[2]
    row = lax.broadcasted_iota(jnp.int32, (LANES, QT), 0)
    key_hi = (lax.broadcasted_iota(jnp.int32, (LANES, QT), 1) >= SEL_LEN).astype(jnp.int32)
    farb_rows = ((row >= FARB_LANE0) & (row < FARB_LANE0 + 3)).astype(F32)
    ones = jnp.ones((QT, LANES), BF16)

    @pl.when(s < n_real)
    def _():
        hn = (_rms(h_ref[...], D_MODEL) * g_ref[...]).astype(BF16)
        kv = jnp.dot(hn, w_ref[...], preferred_element_type=F32)
        n_hb = kv.shape[0] // CMP_STRIDE
        for j in range(2 * gw // LANES):
            cs_ref[j] = kv[:, j * LANES:(j + 1) * LANES]
        for c_ref, j0 in ((kc_ref, 0), (vc_ref, gw // LANES)):
            for l in range(CMP_STRIDE):
                for j in range(gw // LANES):
                    col = l * gw + j * LANES
                    c_ref[:, col:col + LANES] = cs_ref[j0 + j, pl.ds(l, n_hb, stride=CMP_STRIDE), :].astype(BF16)
        for br, (k_ref, v_ref) in enumerate(((ks_ref, vs_ref), (kw_ref, vw_ref))):
            for g in range(N_KV_HEADS):
                lo = 2 * gw + br * 2 * pw + g * LANES
                k = _rms(kv[:, lo:lo + LANES], HEAD_DIM) * gk_ref[br]
                v = kv[:, lo + pw:lo + pw + LANES].astype(BF16)
                for t in range(tiles):
                    extra = farb_rows
                    if br == 0:
                        blk = 2 * (tiles * s + t) + key_hi
                        extra = extra + (row - SEL_LANE0 == blk).astype(F32)
                    k_ref[0, g, t] = (k[t * QT:(t + 1) * QT].T + extra).astype(BF16)
                    v_ref[0, g, t] = jnp.concatenate([v[t * QT:(t + 1) * QT], ones], axis=1)

    @pl.when(s == n_real)
    def _():
        pad_k = jnp.broadcast_to((row == PAD_LANE).astype(BF16), (N_KV_HEADS, tiles, LANES, QT))
        pad_v = jnp.zeros((N_KV_HEADS, tiles, QT, 2 * LANES), BF16)
        for k_ref, v_ref in ((ks_ref, vs_ref), (kw_ref, vw_ref)):
            k_ref[0] = pad_k
            v_ref[0] = pad_v


def _pad_heads(w, n_heads):
    k = w.shape[0]
    w = w.reshape(k, n_heads, HEAD_DIM)
    return jnp.pad(w, ((0, 0), (0, 0), (0, LANES - HEAD_DIM))).reshape(k, n_heads * LANES)


def _kv_proj(h, kv_norm, w_kv, k_norm_slc, k_norm_win, bsz, seq):
    gw = N_KV_HEADS * HEAD_DIM
    tiles = PAD_TILES
    rows = tiles * QT
    n_real = seq // rows
    nt = seq // QT
    wk = w_kv.reshape(D_MODEL, 2 * N_BRANCH, gw)
    w = jnp.concatenate([wk[:, 0], wk[:, 1]] + [_pad_heads(wk[:, j], N_KV_HEADS) for j in (2, 3, 4, 5)],
                        axis=1).astype(BF16)
    gk = jnp.pad(jnp.stack([k_norm_slc, k_norm_win]), ((0, 0), (0, LANES - HEAD_DIM)))
    gk = gk.reshape(2, 1, LANES)
    n = bsz * seq
    real = lambda s: jnp.minimum(s, n_real - 1)
    tok = lambda c: pl.BlockSpec((rows, c), lambda b, s: (b * n_real + real(s), 0))
    tile_spec = lambda r, c: pl.BlockSpec((1, N_KV_HEADS, tiles, r, c), lambda b, s: (b, 0, s, 0, 0))
    hb_spec = pl.BlockSpec((rows // CMP_STRIDE, CMP_STRIDE * gw), lambda b, s: (b * n_real + real(s), 0))
    dense = jax.ShapeDtypeStruct((n // CMP_STRIDE, CMP_STRIDE * gw), BF16)
    k_tiles = jax.ShapeDtypeStruct((bsz, N_KV_HEADS, nt + tiles, LANES, QT), BF16)
    v_tiles = jax.ShapeDtypeStruct((bsz, N_KV_HEADS, nt + tiles, QT, 2 * LANES), BF16)
    return pl.pallas_call(
        functools.partial(_kv_body, n_real=n_real),
        grid=(bsz, n_real + 1),
        in_specs=[tok(D_MODEL), _const_spec((1, D_MODEL)), _const_spec(w.shape),
                  _const_spec((2, 1, LANES))],
        out_specs=[hb_spec, hb_spec, tile_spec(LANES, QT), tile_spec(QT, 2 * LANES),
                   tile_spec(LANES, QT), tile_spec(QT, 2 * LANES)],
        out_shape=[dense, dense, k_tiles, v_tiles, k_tiles, v_tiles],
        scratch_shapes=[pltpu.VMEM((2 * gw // LANES, rows, LANES), F32)],
        compiler_params=_cparams(("parallel", "arbitrary")),
        name="kv_proj",
    )(h, kv_norm.reshape(1, D_MODEL), w, gk)


def _cmp_body(xk_ref, xv_ref, w1k_ref, w1v_ref, bk_ref, bv_ref, w2k_ref, w2v_ref, gk_ref,
              kc_ref, vc_ref, *, nh):
    ab_k = jnp.dot(xk_ref[0], w1k_ref[...], preferred_element_type=F32)
    ab_v = jnp.dot(xv_ref[0], w1v_ref[...], preferred_element_type=F32)
    for g in range(N_KV_HEADS):
        outs = []
        for ab_all, b_ref, w2_ref in ((ab_k, bk_ref, w2k_ref), (ab_v, bv_ref, w2v_ref)):
            ab = ab_all[:, g * 2 * CMP_HIDDEN:(g + 1) * 2 * CMP_HIDDEN]
            hid = ab[:, :CMP_HIDDEN] + pltpu.roll(ab[:, CMP_HIDDEN:], nh - 1, 0) + b_ref[...]
            hid = jax.nn.gelu(hid).astype(BF16)
            outs.append(jnp.dot(hid, w2_ref[...], preferred_element_type=F32))
        k = _rms(outs[0], HEAD_DIM) * gk_ref[...]
        kc_ref[0, g] = k.T.astype(BF16)
        vc_ref[0, g] = outs[1].astype(BF16)


def _compress(kc_raw, vc_raw, k_norm_cmp, pos_k, pos_v, k_w1, k_w2, v_w1, v_w2, bsz, seq):
    nh = seq // CMP_STRIDE
    hb = CMP_STRIDE * N_KV_HEADS * HEAD_DIM
    eye = jnp.eye(N_KV_HEADS, dtype=F32)

    def halfblocks(x):
        return x.reshape(bsz, nh, hb)

    def w1cat(w1):
        w = w1.reshape(2, CMP_STRIDE, HEAD_DIM, CMP_HIDDEN)
        w = jnp.concatenate([w[0], w[1]], axis=-1)
        w = jnp.einsum('ldj,gh->lgdhj', w, eye)
        return w.reshape(hb, N_KV_HEADS * 2 * CMP_HIDDEN).astype(BF16)

    def w2pad(w2):
        return jnp.pad(w2, ((0, 0), (0, LANES - HEAD_DIM))).astype(BF16)

    bias = lambda pos, w1: jnp.einsum('ld,ldh->h', pos, w1,
                                      precision=lax.Precision.HIGHEST).reshape(1, CMP_HIDDEN)
    gk = jnp.pad(k_norm_cmp, (0, LANES - HEAD_DIM)).reshape(1, LANES)
    xspec = pl.BlockSpec((1, nh, hb), lambda b: (b, 0, 0))
    ospec = pl.BlockSpec((1, N_KV_HEADS, LANES, nh), lambda b: (b, 0, 0, 0))
    vspec = pl.BlockSpec((1, N_KV_HEADS, nh, LANES), lambda b: (b, 0, 0, 0))
    w1_shape = (hb, N_KV_HEADS * 2 * CMP_HIDDEN)
    return pl.pallas_call(
        functools.partial(_cmp_body, nh=nh),
        grid=(bsz,),
        in_specs=[xspec, xspec, _const_spec(w1_shape), _const_spec(w1_shape),
                  _const_spec((1, CMP_HIDDEN)), _const_spec((1, CMP_HIDDEN)),
                  _const_spec((CMP_HIDDEN, LANES)), _const_spec((CMP_HIDDEN, LANES)),
                  _const_spec((1, LANES))],
        out_specs=[ospec, vspec],
        out_shape=[jax.ShapeDtypeStruct((bsz, N_KV_HEADS, LANES, nh), BF16),
                   jax.ShapeDtypeStruct((bsz, N_KV_HEADS, nh, LANES), BF16)],
        compiler_params=_cparams(("parallel",)),
        name="kv_compress",
    )(halfblocks(kc_raw), halfblocks(vc_raw), w1cat(k_w1), w1cat(v_w1),
      bias(pos_k, k_w1), bias(pos_v, v_w1), w2pad(k_w2), w2pad(v_w2), gk)


def _qproj_body(h_ref, g_ref, w_ref, qsc_ref, qc_ref, q_ref, gate_ref):
    u = (_rms(h_ref[...], D_MODEL) * g_ref[...]).astype(BF16)
    qg = jnp.dot(u, w_ref[...], preferred_element_type=F32)
    for hh in range(N_HEADS):
        sl = slice(hh * LANES, (hh + 1) * LANES)
        q_ref[:, sl] = (_rms(qg[:, sl], HEAD_DIM) * qsc_ref[:, sl] + qc_ref[:, sl]).astype(BF16)
    gate_ref[...] = jax.nn.sigmoid(qg[:, N_HEADS * LANES:])


def _qproj(h, mix_gain, w_qg, q_norm, rel_bias, bsz, seq):
    rows = min(QP_ROWS, seq)
    n = bsz * seq
    nq = N_HEADS * HEAD_DIM
    wq = jnp.concatenate([_pad_heads(w_qg[:, :nq], N_HEADS),
                          jnp.pad(w_qg[:, nq:], ((0, 0), (0, LANES - N_BRANCH * N_HEADS)))],
                         axis=1).astype(BF16)
    qsc = jnp.tile(jnp.pad(q_norm * (ATTN_SCALE * LOG2E), (0, LANES - HEAD_DIM)), N_HEADS)
    qsc = qsc.reshape(1, N_HEADS * LANES)
    far = _split3(rel_bias[NUM_BUCKETS - 1] * LOG2E)
    qc = jnp.zeros((N_HEADS, LANES), F32)
    for t, term in enumerate(far):
        qc = qc.at[:, FARB_LANE0 + t].set(term.astype(F32))
    qc = qc.at[:, PAD_LANE].set(NEG_INF).reshape(1, N_HEADS * LANES)
    tok = lambda c: pl.BlockSpec((rows, c), lambda i: (i, 0))
    return pl.pallas_call(
        _qproj_body,
        grid=(n // rows,),
        in_specs=[tok(D_MODEL), _const_spec((1, D_MODEL)), _const_spec(wq.shape),
                  _const_spec((1, N_HEADS * LANES)), _const_spec((1, N_HEADS * LANES))],
        out_specs=[tok(N_HEADS * LANES), tok(LANES)],
        out_shape=[jax.ShapeDtypeStruct((n, N_HEADS * LANES), BF16),
                   jax.ShapeDtypeStruct((n, LANES), F32)],
        compiler_params=_cparams(("parallel",)),
        name="q_proj",
    )(h, mix_gain.reshape(1, D_MODEL), wq, qsc, qc)


def _attn_body(q_ref, gate_ref, kc_ref, vc_ref, ks_ref, vs_ref, kw_ref, vw_ref,
               ctab_ref, wtab_ref, ovlt_ref, gexp_ref,
               o_ref, s_scr, sn_scr, m_scr, qf_scr, acc_scr, out_scr, *, n_sel, n_top, nt):
    i = pl.program_id(1)
    rq = Q_PER_KV * QT
    n_win = WIN // QT + 1
    lane = lax.broadcasted_iota(jnp.int32, (QT, LANES), 1)
    sel_lane = (lane >= SEL_LANE0) & (lane < SEL_LANE0 + n_sel)
    far_cut = sel_lane & (lane - SEL_LANE0 >= 2 * (i - 1))
    sel_lane4 = jnp.concatenate([sel_lane] * Q_PER_KV, axis=0)
    blk = lax.broadcasted_iota(jnp.int32, (n_sel, QT), 0)
    blkf = blk.astype(F32)
    qpos = i * QT + lax.broadcasted_iota(jnp.int32, (n_sel, QT), 1)
    cur = lax.shift_right_arithmetic(qpos, int(math.log2(SEL_LEN)))
    forced = (blk == 0) | (blk == cur) | (blk == cur - 1)
    causal = blk * SEL_LEN <= qpos
    tile = lambda t: jnp.where(t < 0, nt, t)

    low = lane < HEAD_DIM

    def gate_tile(b, tl):
        ca, cb = 2 * tl * N_BRANCH + b, (2 * tl + 1) * N_BRANCH + b
        bc = lambda c: jnp.broadcast_to(gate_ref[:, c:c + 1], (QT, LANES))
        return jnp.where(low, bc(ca), bc(cb))

    def merged(o, g, pr):
        return o[2 * pr * QT:(2 * pr + 1) * QT] + pltpu.roll(o[(2 * pr + 1) * QT:(2 * pr + 2) * QT], HEAD_DIM, 1)

    def normalise(pv):
        return pv[:, :LANES] * (1.0 / pv[:, LANES:])

    def tab(g, lo, width):
        return wtab_ref[Q_PER_KV * g:Q_PER_KV * (g + 1), :, lo:lo + width].reshape(rq, width)

    def queries(g):
        return jnp.concatenate([q_ref[:, (Q_PER_KV * g + r) * LANES:(Q_PER_KV * g + r + 1) * LANES]
                                for r in range(Q_PER_KV)], axis=0)

    cmp_p, pv_cs, picked, win = {}, {}, {}, {}
    w_tiles = [tile(i - (n_win - 1) + j) for j in range(n_win)]

    def cmp_softmax(g):
        sc = jnp.dot(queries(g), kc_ref[0, g], preferred_element_type=F32)
        sc = sc + ctab_ref[Q_PER_KV * g:Q_PER_KV * (g + 1)].reshape(rq, sc.shape[-1])
        e = jnp.exp2(sc - jnp.max(sc, axis=-1, keepdims=True))
        row_ok = i * QT + lax.broadcasted_iota(jnp.int32, (QT, sc.shape[-1]), 0) >= CMP_LEN - 1
        cmp_p[g] = jnp.where(jnp.concatenate([row_ok] * Q_PER_KV, axis=0),
                             e * (1.0 / jnp.sum(e, axis=-1, keepdims=True)), 0.0)

    def cmp_out(g):
        p = cmp_p[g]
        pv_cs[g] = jnp.dot(p.astype(BF16), vc_ref[0, g], preferred_element_type=F32)
        psum = p[0:QT] + p[QT:2 * QT] + p[2 * QT:3 * QT] + p[3 * QT:4 * QT]
        ps_t = sum(jnp.dot(ovlt_ref[...], t, preferred_element_type=F32) for t in _split3(psum.T))
        cmp_p[g] = jnp.where(forced, BIG, jnp.where(causal, ps_t[SEL_LANE0:SEL_LANE0 + n_sel], -BIG))

    def select(g):
        score = cmp_p[g]
        pick = jnp.zeros((n_sel, QT), F32)
        for _ in range(n_top):
            mx = jnp.max(score, axis=0, keepdims=True)
            first = jnp.min(jnp.where(score == mx, blkf, float(LANES)), axis=0, keepdims=True)
            hit = blkf == first
            pick = jnp.where(hit, 1.0, pick)
            score = jnp.where(hit, -jnp.inf, score)
        picked[g] = pick

    def win_logits(g):
        k_w = jnp.concatenate([kw_ref[0, g, t] for t in w_tiles], axis=1)
        s_w = jnp.dot(queries(g), k_w, preferred_element_type=F32)
        win[g] = [s_w[:, :QT] + tab(g, 0, QT), s_w[:, QT:(n_win - 2) * QT],
                  s_w[:, (n_win - 2) * QT:] + tab(g, QT, 2 * QT)]

    def win_softmax(g):
        mw = jnp.max(jnp.concatenate(win[g], axis=1), axis=-1, keepdims=True)
        win[g] = jnp.concatenate([jnp.exp2(t - mw).astype(BF16) for t in win[g]], axis=1)

    def win_pv(g):
        v_w = jnp.concatenate([vw_ref[0, g, t] for t in w_tiles], axis=0)
        win[g] = normalise(jnp.dot(win[g], v_w, preferred_element_type=F32))

    def win_out(g):
        o_w = win[g]
        for pr in range(Q_PER_KV // 2):
            tl = Q_PER_KV * g // 2 + pr
            out_scr[:, tl * LANES:(tl + 1) * LANES] = (gate_tile(0, tl) * merged(pv_cs[g], g, pr)
                                                       + gate_tile(2, tl) * merged(o_w, g, pr))

    def near_logits(g):
        q = queries(g)
        selb_t = jnp.where(picked[g] == 0.0, NEG_INF, 0.0)
        selb = jnp.concatenate([jnp.zeros((SEL_LANE0, QT), F32), selb_t,
                                jnp.zeros((LANES - SEL_LANE0 - n_sel, QT), F32)], axis=0).T
        selb_far = jnp.where(far_cut, NEG_INF, selb)
        q_near = jnp.where(sel_lane4, jnp.concatenate([selb.astype(BF16)] * Q_PER_KV, axis=0), q)
        qf_scr[g] = jnp.where(sel_lane4, jnp.concatenate([selb_far.astype(BF16)] * Q_PER_KV, axis=0), q)
        k_n = jnp.concatenate([ks_ref[0, g, tile(i - 1)], ks_ref[0, g, i]], axis=1)
        s_n = jnp.dot(q_near, k_n, preferred_element_type=F32) + tab(g, QT, 2 * QT)
        sn_scr[g] = s_n
        m_scr[g] = jnp.maximum(s_n[:, :QT], s_n[:, QT:])

    stages = (cmp_softmax, win_logits, cmp_out, win_softmax, select, win_pv, near_logits, win_out)
    for step in range(N_KV_HEADS + len(stages) - 1):
        for k, stage in enumerate(stages):
            if 0 <= step - k < N_KV_HEADS:
                stage(step - k)

    n_chunks = (jnp.maximum(i - 1, 0) + FAR_TILES - 1) // FAR_TILES

    def chunk_loop(body):
        def pair(p, carry):
            body(2 * p)
            body(2 * p + 1)
            return carry

        lax.fori_loop(0, n_chunks // 2, pair, 0)

        @pl.when(n_chunks % 2 == 1)
        def _():
            body(n_chunks - 1)

    def far_logits(c):
        for g in range(N_KV_HEADS):
            k_f = jnp.concatenate([ks_ref[0, g, FAR_TILES * c + t] for t in range(FAR_TILES)], axis=1)
            s = jnp.dot(qf_scr[g], k_f, preferred_element_type=F32)
            s_scr[g, c] = s
            m = m_scr[g]
            for t in range(FAR_TILES):
                m = jnp.maximum(m, s[:, t * QT:(t + 1) * QT])
            m_scr[g] = m

    chunk_loop(far_logits)

    near_p = {}

    def near_softmax(g):
        m = jnp.max(m_scr[g], axis=-1, keepdims=True)
        m_scr[g] = jnp.broadcast_to(m, (rq, QT))
        near_p[g] = jnp.exp2(sn_scr[g] - m).astype(BF16)

    def near_pv(g):
        v_n = jnp.concatenate([vs_ref[0, g, tile(i - 1)], vs_ref[0, g, i]], axis=0)
        acc_scr[g] = jnp.dot(near_p[g], v_n, preferred_element_type=F32)

    for step in range(N_KV_HEADS + 1):
        if step < N_KV_HEADS:
            near_softmax(step)
        if step > 0:
            near_pv(step - 1)

    def far_pv(c):
        for g in range(N_KV_HEADS):
            m = jnp.concatenate([m_scr[g]] * FAR_TILES, axis=1)
            p = jnp.exp2(s_scr[g, c] - m).astype(BF16)
            v_f = jnp.concatenate([vs_ref[0, g, FAR_TILES * c + t] for t in range(FAR_TILES)], axis=0)
            acc_scr[g] += jnp.dot(p, v_f, preferred_element_type=F32)

    chunk_loop(far_pv)

    for g in range(N_KV_HEADS):
        o_s = normalise(acc_scr[g])
        for pr in range(Q_PER_KV // 2):
            tl = Q_PER_KV * g // 2 + pr
            o = out_scr[:, tl * LANES:(tl + 1) * LANES] + gate_tile(1, tl) * merged(o_s, g, pr)
            o_ref[:, tl * LANES:(tl + 1) * LANES] = o.astype(BF16)


def _rel_bucket(dist):
    n = jnp.maximum(dist, 0)
    max_exact = NUM_BUCKETS // 2
    logv = jnp.log(jnp.maximum(n, 1).astype(F32) / max_exact) / math.log(MAX_DISTANCE / max_exact)
    large = jnp.minimum(max_exact + (logv * (NUM_BUCKETS - max_exact)).astype(jnp.int32), NUM_BUCKETS - 1)
    return jnp.where(n < max_exact, n, large)


def _bias_tables(rel_bias, seq):
    nt = seq // QT
    nc = seq // CMP_STRIDE
    cpt = QT // CMP_STRIDE
    rb = rel_bias * LOG2E

    def f(dist):
        onehot = (_rel_bucket(dist)[..., None] == jnp.arange(NUM_BUCKETS)).astype(F32)
        return jnp.einsum('...k,kh->h...', onehot, rb, precision=lax.Precision.HIGHEST)

    c_rel = jnp.arange(-cpt * (nt - 1), nc)
    dc = jnp.arange(QT)[:, None] - (c_rel * CMP_STRIDE + CMP_LEN - 1)[None, :]
    t0 = jnp.where((dc >= 0)[None], f(dc), NEG_INF)
    ctab = jnp.stack([t0[:, :, cpt * (nt - 1 - i):cpt * (nt - 1 - i) + nc] for i in range(nt)], axis=1)
    ctab = ctab.reshape(N_HEADS, seq, nc)
    far = rb[NUM_BUCKETS - 1].reshape(N_HEADS, 1, 1)
    dq = jnp.arange(QT)[:, None] - jnp.arange(QT)[None, :]
    d0, d3, d4 = WIN + dq, QT + dq, dq
    wtab = jnp.concatenate([jnp.where((d0 < WIN)[None], f(d0) - far, NEG_INF), f(d3) - far,
                            jnp.where((d4 >= 0)[None], f(d4) - far, NEG_INF)], axis=2)
    return ctab, wtab


def _overlap_matrix_t(seq):
    nc = seq // CMP_STRIDE
    n_sel = seq // SEL_LEN
    c_start = jnp.arange(nc) * CMP_STRIDE
    j_start = jnp.arange(n_sel) * SEL_LEN
    ov = jnp.clip(jnp.minimum(c_start[None, :] + CMP_LEN, j_start[:, None] + SEL_LEN)
                  - jnp.maximum(c_start[None, :], j_start[:, None]), 0, None)
    ov = ov.astype(F32) / CMP_LEN
    return jnp.pad(ov, ((SEL_LANE0, LANES - SEL_LANE0 - n_sel), (0, 0))).astype(BF16)


def _gate_expansion():
    col = jnp.arange(LANES).reshape(1, LANES, 1)
    head = (jnp.arange(D_MODEL) // HEAD_DIM).reshape(1, 1, D_MODEL)
    br = jnp.arange(N_BRANCH).reshape(N_BRANCH, 1, 1)
    return (col == head * N_BRANCH + br).astype(BF16)


def _attention(q, gates, kv, rel_bias, bsz, seq):
    kc, vc, ks, vs, kw, vw = kv
    nt = seq // QT
    nc = seq // CMP_STRIDE
    n_sel = seq // SEL_LEN
    n_top = min(SEL_TOPN, n_sel)
    n_chunk = max(-(-(nt - 2) // FAR_TILES), 1)
    assert n_chunk * FAR_TILES <= nt
    nts = nt + PAD_TILES
    ctab, wtab = _bias_tables(rel_bias, seq)
    rq = Q_PER_KV * QT
    per_batch = lambda shape: pl.BlockSpec((1,) + shape, lambda b, i: (b,) + (0,) * len(shape),
                                           pipeline_mode=pl.Buffered(1))
    tok = lambda c: pl.BlockSpec((QT, c), lambda b, i: (b * nt + i, 0))
    return pl.pallas_call(
        functools.partial(_attn_body, n_sel=n_sel, n_top=n_top, nt=nt),
        grid=(bsz, nt),
        in_specs=[tok(N_HEADS * LANES), tok(LANES),
                  per_batch((N_KV_HEADS, LANES, nc)),
                  per_batch((N_KV_HEADS, nc, LANES)),
                  per_batch((N_KV_HEADS, nts, LANES, QT)),
                  per_batch((N_KV_HEADS, nts, QT, 2 * LANES)),
                  per_batch((N_KV_HEADS, nts, LANES, QT)),
                  per_batch((N_KV_HEADS, nts, QT, 2 * LANES)),
                  pl.BlockSpec((N_HEADS, QT, nc), lambda b, i: (0, i, 0)),
                  _const_spec((N_HEADS, QT, 3 * QT)),
                  _const_spec((LANES, nc)),
                  _const_spec((N_BRANCH, LANES, D_MODEL))],
        out_specs=tok(D_MODEL),
        out_shape=jax.ShapeDtypeStruct((bsz * seq, D_MODEL), BF16),
        scratch_shapes=[pltpu.VMEM((N_KV_HEADS, n_chunk, rq, FAR_TILES * QT), F32),
                        pltpu.VMEM((N_KV_HEADS, rq, 2 * QT), F32),
                        pltpu.VMEM((N_KV_HEADS, rq, QT), F32),
                        pltpu.VMEM((N_KV_HEADS, rq, LANES), BF16),
                        pltpu.VMEM((N_KV_HEADS, rq, 2 * LANES), F32),
                        pltpu.VMEM((QT, D_MODEL), F32)],
        compiler_params=_cparams(("parallel", "arbitrary")),
        name="nsa_attn",
    )(q, gates, kc, vc, ks, vs, kw, vw, ctab, wtab, _overlap_matrix_t(seq), _gate_expansion())


def kernel(x, rel_bias, ffn1_norm, ffn1_w_in, ffn1_w_out, mix_norm, ffn2_norm, ffn2_w_in, ffn2_w_out,
           s5_a_re, s5_a_im, s5_log_dt, s5_b_re, s5_b_im, s5_c_re, s5_c_im, s5_d, s5_w_glu,
           kv_norm, w_kv, k_norm_cmp, k_norm_slc, k_norm_win, cmp_pos_k, cmp_pos_v,
           cmp_k_w1, cmp_k_w2, cmp_v_w1, cmp_v_w2, w_qg, q_norm, w_o):
    bsz, seq, _ = x.shape
    depth = ffn1_norm.shape[0]
    n_a = s5_a_re.shape[0]
    h = x.reshape(bsz * seq, D_MODEL)
    kv = None
    ffn1_w_in, ffn1_w_out, ffn2_w_in, ffn2_w_out = (
        w.astype(BF16) for w in (ffn1_w_in, ffn1_w_out, ffn2_w_in, ffn2_w_out))
    for layer in range(depth):
        s5_layer = layer < n_a
        h = _ffn(h, ffn1_norm[layer], ffn1_w_in, ffn1_w_out, layer, bsz, seq,
                 in_tm=False, out_tm=s5_layer)
        pre = None
        if s5_layer:
            a = layer
            h = _s5(h, mix_norm[layer], s5_a_re[a], s5_a_im[a], s5_log_dt[a], s5_b_re[a], s5_b_im[a],
                    s5_c_re[a], s5_c_im[a], s5_d[a], s5_w_glu[a], bsz, seq)
        else:
            b = layer - n_a
            q, gates = _qproj(h, mix_norm[layer], w_qg[b], q_norm[b], rel_bias, bsz, seq)
            pre = (_attention(q, gates, kv, rel_bias, bsz, seq), w_o[b])
        h = _ffn(h, ffn2_norm[layer], ffn2_w_in, ffn2_w_out, layer, bsz, seq,
                 in_tm=s5_layer, out_tm=False, pre=pre)
        if layer == n_a - 1:
            kc_raw, vc_raw, ks, vs, kw, vw = _kv_proj(h, kv_norm, w_kv, k_norm_slc, k_norm_win, bsz, seq)
            kc, vc = _compress(kc_raw, vc_raw, k_norm_cmp, cmp_pos_k, cmp_pos_v,
                               cmp_k_w1, cmp_k_w2, cmp_v_w1, cmp_v_w2, bsz, seq)
            kv = (kc, vc, ks, vs, kw, vw)
    return h.reshape(bsz, seq, D_MODEL)
```

```python
import functools
import math

import jax
import jax.numpy as jnp
from jax import lax
from jax.experimental import pallas as pl
from jax.experimental.pallas import tpu as pltpu

F32 = jnp.float32
BF16 = jnp.bfloat16

D_MODEL = 1024
D_FF = 2816
RMS_EPS = 1e-6
S5_GROUP_CH = 16
S5_GROUPS = D_MODEL // S5_GROUP_CH
S5_STATE = 64
N_HEADS = 16
HEAD_DIM = 64
N_KV_HEADS = 4
Q_PER_KV = N_HEADS // N_KV_HEADS
CMP_LEN = 32
CMP_STRIDE = 16
CMP_HIDDEN = 2 * HEAD_DIM
SEL_LEN = 64
SEL_TOPN = 8
WIN = 512
N_BRANCH = 3
ATTN_SCALE = HEAD_DIM ** -0.5
NUM_BUCKETS = 32
MAX_DISTANCE = 128
NEG_INF = -1e30
BIG = 1e9

LANES = 128
MXU_DIM = 256
VMEM_LIMIT = 56 * 1024 * 1024

FFN_ROWS = 1024
FFN_BATCH = 8
FF_CHUNK = MXU_DIM
S5_TC = 16
S5_SLAB = MXU_DIM
S5_LANES = 256
QP_ROWS = 1024
QT = 128
ATTN_TILES = 2
FAR_TILES = 4
PAD_TILES = WIN // QT
LOG2E = math.log2(math.e)

SEL_LANE0 = HEAD_DIM
FARB_LANE0 = 96
PAD_LANE = 99


def _rms(x, n):
    ss = jnp.sum(x * x, axis=-1, keepdims=True)
    return x * lax.rsqrt(ss * (1.0 / n) + RMS_EPS)


def _cparams(sem):
    return pltpu.CompilerParams(dimension_semantics=sem, vmem_limit_bytes=VMEM_LIMIT)


def _const_spec(shape):
    nd = len(shape)
    return pl.BlockSpec(shape, lambda *_: (0,) * nd, pipeline_mode=pl.Buffered(1))


def _split3(x):
    x1 = x.astype(BF16)
    r1 = x - x1.astype(F32)
    x2 = r1.astype(BF16)
    x3 = (r1 - x2.astype(F32)).astype(BF16)
    return x1, x2, x3


def _ffn_body(*refs, pre, in_tm, out_tm):
    if pre:
        x_ref, po_ref, pw_ref, g_ref, win_ref, wout_ref, o_ref, act_ref = refs
    else:
        x_ref, g_ref, win_ref, wout_ref, o_ref, act_ref = refs
    rows = act_ref.shape[0]
    x = (jnp.swapaxes(x_ref[...], 0, 1) if in_tm else x_ref[...]).reshape(rows, D_MODEL)
    if pre:
        x = x + jnp.dot(po_ref[...].reshape(rows, D_MODEL), pw_ref[...], preferred_element_type=F32)
    xn = (_rms(x, D_MODEL) * g_ref[...]).astype(BF16)
    for c in range(D_FF // FF_CHUNK):
        lo = c * FF_CHUNK
        a = jnp.dot(xn, win_ref[:, lo:lo + FF_CHUNK], preferred_element_type=F32)
        b = jnp.dot(xn, win_ref[:, D_FF + lo:D_FF + lo + FF_CHUNK], preferred_element_type=F32)
        act_ref[:, lo:lo + FF_CHUNK] = (a * jax.nn.sigmoid(a) * b).astype(BF16)
    y = jnp.dot(act_ref[...], wout_ref[...], preferred_element_type=F32)
    res = x + 0.5 * y
    nb = o_ref.shape[1] if out_tm else o_ref.shape[0]
    res = res.reshape(nb, rows // nb, D_MODEL)
    o_ref[...] = jnp.swapaxes(res, 0, 1) if out_tm else res


def _token_spec(tb, tt, time_major):
    if time_major:
        return pl.BlockSpec((tt, tb, D_MODEL), lambda b, t: (t, b, 0))
    return pl.BlockSpec((tb, tt, D_MODEL), lambda b, t: (b, t, 0))


def _layer_spec(shape, layer):
    return pl.BlockSpec((None,) + shape, lambda *_: (layer, 0, 0), pipeline_mode=pl.Buffered(1))


def _ffn(h, gain, w_in, w_out, layer, bsz, seq, in_tm, out_tm, pre=None):
    tb = FFN_BATCH if (in_tm or out_tm) else 1
    tt = min(FFN_ROWS // tb, seq)
    rows = tb * tt
    as_bm = lambda a: a.reshape(bsz, seq, D_MODEL)
    out_shape = (seq, bsz, D_MODEL) if out_tm else (bsz, seq, D_MODEL)
    args, specs = [h if in_tm else as_bm(h)], [_token_spec(tb, tt, in_tm)]
    if pre is not None:
        args += [as_bm(pre[0]), pre[1].astype(BF16)]
        specs += [_token_spec(tb, tt, False), _const_spec((D_MODEL, D_MODEL))]
    args += [gain.reshape(1, D_MODEL), w_in, w_out]
    specs += [_const_spec((1, D_MODEL)), _layer_spec((D_MODEL, 2 * D_FF), layer),
              _layer_spec((D_FF, D_MODEL), layer)]
    out = pl.pallas_call(
        functools.partial(_ffn_body, pre=pre is not None, in_tm=in_tm, out_tm=out_tm),
        grid=(bsz // tb, seq // tt),
        in_specs=specs,
        out_specs=_token_spec(tb, tt, out_tm),
        out_shape=jax.ShapeDtypeStruct(out_shape, F32),
        scratch_shapes=[pltpu.VMEM((rows, D_FF), BF16)],
        compiler_params=_cparams(("parallel", "parallel")),
        name="ffn",
    )(*args)
    return out if out_tm else out.reshape(bsz * seq, D_MODEL)


def _s5_body(h_ref, g_ref, bmat_ref, cmat_ref, are_ref, aim_ref, d_ref, wglu_ref,
             o_ref, bu_ref, xb_ref, st_ref, y_ref, *, tc, nb):
    n_slab = D_MODEL // S5_SLAB
    half = S5_SLAB // S5_GROUP_CH * S5_STATE
    rows = tc * nb

    @pl.when(pl.program_id(0) == 0)
    def _():
        st_ref[...] = jnp.zeros_like(st_ref)

    h = h_ref[...].reshape(rows, D_MODEL)
    u = _rms(h, D_MODEL) * g_ref[...]
    ub = u.astype(BF16)

    def project(sl):
        bu_ref[sl % 2] = jnp.dot(ub[:, sl * S5_SLAB:(sl + 1) * S5_SLAB], bmat_ref[sl],
                                 preferred_element_type=F32)

    project(0)
    for sl in range(n_slab):
        if sl + 1 < n_slab:
            project(sl + 1)
        bu, xb = bu_ref.at[sl % 2], xb_ref.at[sl % 2]
        for wb in range(half // S5_LANES):
            re = slice(wb * S5_LANES, (wb + 1) * S5_LANES)
            im = slice(half + wb * S5_LANES, half + (wb + 1) * S5_LANES)
            ar = jnp.broadcast_to(are_ref[sl, :, re], (nb, S5_LANES))
            ai = jnp.broadcast_to(aim_ref[sl, :, re], (nb, S5_LANES))
            xr, xi = st_ref[sl, :, re], st_ref[sl, :, im]
            for t in range(tc):
                rs = slice(t * nb, (t + 1) * nb)
                xr, xi = ar * xr - ai * xi + bu[rs, re], ar * xi + ai * xr + bu[rs, im]
                xb[rs, re] = xr.astype(BF16)
                xb[rs, im] = xi.astype(BF16)
            st_ref[sl, :, re] = xr
            st_ref[sl, :, im] = xi
        y_ref[:, sl * S5_SLAB:(sl + 1) * S5_SLAB] = jnp.dot(
            xb_ref[sl % 2], cmat_ref[sl], preferred_element_type=F32)
    y = jax.nn.gelu(y_ref[...] + d_ref[...] * u)
    gate = jnp.dot(y.astype(BF16), wglu_ref[...], preferred_element_type=F32)
    o_ref[...] = (h + y * jax.nn.sigmoid(gate)).reshape(tc, nb, D_MODEL)


def _s5_params(a_re, a_im, log_dt, b_re, b_im, c_re, c_im):
    dt = jnp.exp(log_dt)[:, None]
    mag = jnp.exp(a_re * dt)
    ab_re = mag * jnp.cos(a_im * dt)
    ab_im = mag * jnp.sin(a_im * dt)
    den = a_re * a_re + a_im * a_im
    z_re = ((ab_re - 1.0) * a_re + ab_im * a_im) / den
    z_im = (ab_im * a_re - (ab_re - 1.0) * a_im) / den
    bb_re = z_re[..., None] * b_re - z_im[..., None] * b_im
    bb_im = z_re[..., None] * b_im + z_im[..., None] * b_re
    n_slab = D_MODEL // S5_SLAB
    gps = S5_GROUPS // n_slab
    eye = jnp.eye(gps, dtype=F32)

    def in_mat(bb):
        bb = bb.reshape(n_slab, gps, S5_STATE, S5_GROUP_CH)
        m = jnp.einsum('sgph,gk->sghkp', bb, eye)
        return m.reshape(n_slab, gps * S5_GROUP_CH, gps * S5_STATE)

    def out_mat(cc):
        cc = cc.reshape(n_slab, gps, S5_GROUP_CH, S5_STATE)
        m = jnp.einsum('sghp,gk->sgpkh', cc, eye)
        return m.reshape(n_slab, gps * S5_STATE, gps * S5_GROUP_CH)

    bmat = jnp.concatenate([in_mat(bb_re), in_mat(bb_im)], axis=2).astype(BF16)
    cmat = jnp.concatenate([out_mat(c_re), out_mat(-c_im)], axis=1).astype(BF16)
    are = ab_re.reshape(n_slab, 1, gps * S5_STATE)
    aim = ab_im.reshape(n_slab, 1, gps * S5_STATE)
    return bmat, cmat, are, aim


def _s5(h_tm, gain, a_re, a_im, log_dt, b_re, b_im, c_re, c_im, d_skip, w_glu, bsz, seq):
    tc = min(S5_TC, seq)
    n_slab = D_MODEL // S5_SLAB
    half = S5_SLAB // S5_GROUP_CH * S5_STATE
    bmat, cmat, are, aim = _s5_params(a_re, a_im, log_dt, b_re, b_im, c_re, c_im)
    blk = pl.BlockSpec((tc, bsz, D_MODEL), lambda i: (i, 0, 0))
    out = pl.pallas_call(
        functools.partial(_s5_body, tc=tc, nb=bsz),
        grid=(seq // tc,),
        in_specs=[blk,
                  _const_spec((1, D_MODEL)),
                  _const_spec((n_slab, S5_SLAB, 2 * half)),
                  _const_spec((n_slab, 2 * half, S5_SLAB)),
                  _const_spec((n_slab, 1, half)),
                  _const_spec((n_slab, 1, half)),
                  _const_spec((1, D_MODEL)),
                  _const_spec((D_MODEL, D_MODEL))],
        out_specs=blk,
        out_shape=jax.ShapeDtypeStruct((seq, bsz, D_MODEL), F32),
        scratch_shapes=[pltpu.VMEM((2, tc * bsz, 2 * half), F32),
                        pltpu.VMEM((2, tc * bsz, 2 * half), BF16),
                        pltpu.VMEM((n_slab, bsz, 2 * half), F32),
                        pltpu.VMEM((tc * bsz, D_MODEL), F32)],
        compiler_params=_cparams(("arbitrary",)),
        name="s5",
    )(h_tm, gain.reshape(1, D_MODEL), bmat, cmat, are, aim,
      d_skip.reshape(1, D_MODEL), w_glu.astype(BF16))
    return out


def _kv_body(h_ref, g_ref, w_ref, gk_ref, kc_ref, vc_ref, ks_ref, vs_ref, kw_ref, vw_ref, cs_ref,
             *, n_real):
    s = pl.program_id(1)
    gw = N_KV_HEADS * HEAD_DIM
    pw = N_KV_HEADS * LANES
    tiles = ks_ref.shape[2]
    row = lax.broadcasted_iota(jnp.int32, (LANES, QT), 0)
    key_hi = (lax.broadcasted_iota(jnp.int32, (LANES, QT), 1) >= SEL_LEN).astype(jnp.int32)
    farb_rows = ((row >= FARB_LANE0) & (row < FARB_LANE0 + 3)).astype(F32)
    ones = jnp.ones((QT, LANES), BF16)

    @pl.when(s < n_real)
    def _():
        hn = (_rms(h_ref[...], D_MODEL) * g_ref[...]).astype(BF16)
        proj = lambda lo, n: jnp.dot(hn, w_ref[:, lo:lo + n], preferred_element_type=F32)
        kvc = proj(0, 2 * gw)
        kv_all = [(proj(2 * gw + br * 2 * pw, pw), proj(2 * gw + br * 2 * pw + pw, pw)) for br in range(2)]
        n_hb = kvc.shape[0] // CMP_STRIDE
        for j in range(2 * gw // LANES):
            cs_ref[j] = kvc[:, j * LANES:(j + 1) * LANES]
        for c_ref, j0 in ((kc_ref, 0), (vc_ref, gw // LANES)):
            for l in range(CMP_STRIDE):
                for j in range(gw // LANES):
                    col = l * gw + j * LANES
                    c_ref[:, col:col + LANES] = cs_ref[j0 + j, pl.ds(l, n_hb, stride=CMP_STRIDE), :].astype(BF16)
        for br, (k_ref, v_ref) in enumerate(((ks_ref, vs_ref), (kw_ref, vw_ref))):
            k_all, v_all = kv_all[br]
            for g in range(N_KV_HEADS):
                k = _rms(k_all[:, g * LANES:(g + 1) * LANES], HEAD_DIM) * gk_ref[br]
                v = v_all[:, g * LANES:(g + 1) * LANES].astype(BF16)
                for t in range(tiles):
                    extra = farb_rows
                    if br == 0:
                        blk = 2 * (tiles * s + t) + key_hi
                        extra = extra + (row - SEL_LANE0 == blk).astype(F32)
                    k_ref[0, g, t] = (k[t * QT:(t + 1) * QT].T + extra).astype(BF16)
                    v_ref[0, g, t] = jnp.concatenate([v[t * QT:(t + 1) * QT], ones], axis=1)

    @pl.when(s == n_real)
    def _():
        pad_k = jnp.broadcast_to((row == PAD_LANE).astype(BF16), (N_KV_HEADS, tiles, LANES, QT))
        pad_v = jnp.zeros((N_KV_HEADS, tiles, QT, 2 * LANES), BF16)
        for k_ref, v_ref in ((ks_ref, vs_ref), (kw_ref, vw_ref)):
            k_ref[0] = pad_k
            v_ref[0] = pad_v


def _pad_heads(w, n_heads):
    k = w.shape[0]
    w = w.reshape(k, n_heads, HEAD_DIM)
    return jnp.pad(w, ((0, 0), (0, 0), (0, LANES - HEAD_DIM))).reshape(k, n_heads * LANES)


def _kv_proj(h, kv_norm, w_kv, k_norm_slc, k_norm_win, bsz, seq):
    gw = N_KV_HEADS * HEAD_DIM
    tiles = PAD_TILES
    rows = tiles * QT
    n_real = seq // rows
    nt = seq // QT
    wk = w_kv.reshape(D_MODEL, 2 * N_BRANCH, gw)
    w = jnp.concatenate([wk[:, 0], wk[:, 1]] + [_pad_heads(wk[:, j], N_KV_HEADS) for j in (2, 3, 4, 5)],
                        axis=1).astype(BF16)
    gk = jnp.pad(jnp.stack([k_norm_slc, k_norm_win]), ((0, 0), (0, LANES - HEAD_DIM)))
    gk = gk.reshape(2, 1, LANES)
    n = bsz * seq
    real = lambda s: jnp.minimum(s, n_real - 1)
    tok = lambda c: pl.BlockSpec((rows, c), lambda b, s: (b * n_real + real(s), 0))
    tile_spec = lambda r, c: pl.BlockSpec((1, N_KV_HEADS, tiles, r, c), lambda b, s: (b, 0, s, 0, 0))
    hb_spec = pl.BlockSpec((rows // CMP_STRIDE, CMP_STRIDE * gw), lambda b, s: (b * n_real + real(s), 0))
    dense = jax.ShapeDtypeStruct((n // CMP_STRIDE, CMP_STRIDE * gw), BF16)
    k_tiles = jax.ShapeDtypeStruct((bsz, N_KV_HEADS, nt + tiles, LANES, QT), BF16)
    v_tiles = jax.ShapeDtypeStruct((bsz, N_KV_HEADS, nt + tiles, QT, 2 * LANES), BF16)
    return pl.pallas_call(
        functools.partial(_kv_body, n_real=n_real),
        grid=(bsz, n_real + 1),
        in_specs=[tok(D_MODEL), _const_spec((1, D_MODEL)), _const_spec(w.shape),
                  _const_spec((2, 1, LANES))],
        out_specs=[hb_spec, hb_spec, tile_spec(LANES, QT), tile_spec(QT, 2 * LANES),
                   tile_spec(LANES, QT), tile_spec(QT, 2 * LANES)],
        out_shape=[dense, dense, k_tiles, v_tiles, k_tiles, v_tiles],
        scratch_shapes=[pltpu.VMEM((2 * gw // LANES, rows, LANES), F32)],
        compiler_params=_cparams(("parallel", "arbitrary")),
        name="kv_proj",
    )(h, kv_norm.reshape(1, D_MODEL), w, gk)


def _cmp_body(xk_ref, xv_ref, w1k_ref, w1v_ref, bk_ref, bv_ref, w2k_ref, w2v_ref, gk_ref,
              kc_ref, vc_ref, *, nh):
    ab_k = jnp.dot(xk_ref[0], w1k_ref[...], preferred_element_type=F32)
    ab_v = jnp.dot(xv_ref[0], w1v_ref[...], preferred_element_type=F32)
    for g in range(N_KV_HEADS):
        outs = []
        for ab_all, b_ref, w2_ref in ((ab_k, bk_ref, w2k_ref), (ab_v, bv_ref, w2v_ref)):
            ab = ab_all[:, g * 2 * CMP_HIDDEN:(g + 1) * 2 * CMP_HIDDEN]
            hid = ab[:, :CMP_HIDDEN] + pltpu.roll(ab[:, CMP_HIDDEN:], nh - 1, 0) + b_ref[...]
            hid = jax.nn.gelu(hid).astype(BF16)
            outs.append(jnp.dot(hid, w2_ref[...], preferred_element_type=F32))
        k = _rms(outs[0], HEAD_DIM) * gk_ref[...]
        kc_ref[0, g] = k.T.astype(BF16)
        vc_ref[0, g] = outs[1].astype(BF16)


def _compress(kc_raw, vc_raw, k_norm_cmp, pos_k, pos_v, k_w1, k_w2, v_w1, v_w2, bsz, seq):
    nh = seq // CMP_STRIDE
    hb = CMP_STRIDE * N_KV_HEADS * HEAD_DIM
    eye = jnp.eye(N_KV_HEADS, dtype=F32)

    def halfblocks(x):
        return x.reshape(bsz, nh, hb)

    def w1cat(w1):
        w = w1.reshape(2, CMP_STRIDE, HEAD_DIM, CMP_HIDDEN)
        w = jnp.concatenate([w[0], w[1]], axis=-1)
        w = jnp.einsum('ldj,gh->lgdhj', w, eye)
        return w.reshape(hb, N_KV_HEADS * 2 * CMP_HIDDEN).astype(BF16)

    def w2pad(w2):
        return jnp.pad(w2, ((0, 0), (0, LANES - HEAD_DIM))).astype(BF16)

    bias = lambda pos, w1: jnp.einsum('ld,ldh->h', pos, w1,
                                      precision=lax.Precision.HIGHEST).reshape(1, CMP_HIDDEN)
    gk = jnp.pad(k_norm_cmp, (0, LANES - HEAD_DIM)).reshape(1, LANES)
    xspec = pl.BlockSpec((1, nh, hb), lambda b: (b, 0, 0))
    ospec = pl.BlockSpec((1, N_KV_HEADS, LANES, nh), lambda b: (b, 0, 0, 0))
    vspec = pl.BlockSpec((1, N_KV_HEADS, nh, LANES), lambda b: (b, 0, 0, 0))
    w1_shape = (hb, N_KV_HEADS * 2 * CMP_HIDDEN)
    return pl.pallas_call(
        functools.partial(_cmp_body, nh=nh),
        grid=(bsz,),
        in_specs=[xspec, xspec, _const_spec(w1_shape), _const_spec(w1_shape),
                  _const_spec((1, CMP_HIDDEN)), _const_spec((1, CMP_HIDDEN)),
                  _const_spec((CMP_HIDDEN, LANES)), _const_spec((CMP_HIDDEN, LANES)),
                  _const_spec((1, LANES))],
        out_specs=[ospec, vspec],
        out_shape=[jax.ShapeDtypeStruct((bsz, N_KV_HEADS, LANES, nh), BF16),
                   jax.ShapeDtypeStruct((bsz, N_KV_HEADS, nh, LANES), BF16)],
        compiler_params=_cparams(("parallel",)),
        name="kv_compress",
    )(halfblocks(kc_raw), halfblocks(vc_raw), w1cat(k_w1), w1cat(v_w1),
      bias(pos_k, k_w1), bias(pos_v, v_w1), w2pad(k_w2), w2pad(v_w2), gk)


def _qproj_body(h_ref, g_ref, w_ref, qsc_ref, qc_ref, q_ref, gate_ref):
    u = (_rms(h_ref[...], D_MODEL) * g_ref[...]).astype(BF16)
    qg = jnp.dot(u, w_ref[...], preferred_element_type=F32)
    for hh in range(N_HEADS):
        sl = slice(hh * LANES, (hh + 1) * LANES)
        q_ref[:, sl] = (_rms(qg[:, sl], HEAD_DIM) * qsc_ref[:, sl] + qc_ref[:, sl]).astype(BF16)
    gate_ref[...] = jax.nn.sigmoid(qg[:, N_HEADS * LANES:])


def _qproj(h, mix_gain, w_qg, q_norm, rel_bias, bsz, seq):
    rows = min(QP_ROWS, seq)
    n = bsz * seq
    nq = N_HEADS * HEAD_DIM
    wq = jnp.concatenate([_pad_heads(w_qg[:, :nq], N_HEADS),
                          jnp.pad(w_qg[:, nq:], ((0, 0), (0, LANES - N_BRANCH * N_HEADS)))],
                         axis=1).astype(BF16)
    qsc = jnp.tile(jnp.pad(q_norm * (ATTN_SCALE * LOG2E), (0, LANES - HEAD_DIM)), N_HEADS)
    qsc = qsc.reshape(1, N_HEADS * LANES)
    far = _split3(rel_bias[NUM_BUCKETS - 1] * LOG2E)
    qc = jnp.zeros((N_HEADS, LANES), F32)
    for t, term in enumerate(far):
        qc = qc.at[:, FARB_LANE0 + t].set(term.astype(F32))
    qc = qc.at[:, PAD_LANE].set(NEG_INF).reshape(1, N_HEADS * LANES)
    tok = lambda c: pl.BlockSpec((rows, c), lambda i: (i, 0))
    return pl.pallas_call(
        _qproj_body,
        grid=(n // rows,),
        in_specs=[tok(D_MODEL), _const_spec((1, D_MODEL)), _const_spec(wq.shape),
                  _const_spec((1, N_HEADS * LANES)), _const_spec((1, N_HEADS * LANES))],
        out_specs=[tok(N_HEADS * LANES), tok(LANES)],
        out_shape=[jax.ShapeDtypeStruct((n, N_HEADS * LANES), BF16),
                   jax.ShapeDtypeStruct((n, LANES), F32)],
        compiler_params=_cparams(("parallel",)),
        name="q_proj",
    )(h, mix_gain.reshape(1, D_MODEL), wq, qsc, qc)


def _attn_body(q_ref, gate_ref, kc_ref, vc_ref, ks_ref, vs_ref, kw_ref, vw_ref, ctab_ref, wtab_ref, ovlt_ref,
               o_ref, *scratch, **static):
    def one_tile(j, carry):
        rows = pl.ds(pl.multiple_of(j * QT, QT), QT)
        _attn_tile(pl.program_id(1) * ATTN_TILES + j, q_ref.at[rows], gate_ref.at[rows],
                   kc_ref, vc_ref, ks_ref, vs_ref, kw_ref, vw_ref, ctab_ref.at[:, rows], wtab_ref, ovlt_ref,
                   o_ref.at[rows], *scratch, **static)
        return carry

    lax.fori_loop(0, ATTN_TILES, one_tile, 0)


def _attn_tile(i, q_ref, gate_ref, kc_ref, vc_ref, ks_ref, vs_ref, kw_ref, vw_ref,
               ctab_ref, wtab_ref, ovlt_ref,
               o_ref, s_scr, sn_scr, m_scr, qf_scr, acc_scr, out_scr, *, n_sel, n_top, nt):
    rq = Q_PER_KV * QT
    n_win = WIN // QT + 1
    lane = lax.broadcasted_iota(jnp.int32, (QT, LANES), 1)
    sel_lane = (lane >= SEL_LANE0) & (lane < SEL_LANE0 + n_sel)
    far_cut = sel_lane & (lane - SEL_LANE0 >= 2 * (i - 1))
    sel_lane4 = jnp.concatenate([sel_lane] * Q_PER_KV, axis=0)
    blk = lax.broadcasted_iota(jnp.int32, (n_sel, QT), 0)
    blkf = blk.astype(F32)
    qpos = i * QT + lax.broadcasted_iota(jnp.int32, (n_sel, QT), 1)
    cur = lax.shift_right_arithmetic(qpos, int(math.log2(SEL_LEN)))
    forced = (blk == 0) | (blk == cur) | (blk == cur - 1)
    causal = blk * SEL_LEN <= qpos
    tile = lambda t: jnp.where(t < 0, nt, t)

    low = lane < HEAD_DIM

    def gate_tile(b, tl):
        ca, cb = 2 * tl * N_BRANCH + b, (2 * tl + 1) * N_BRANCH + b
        bc = lambda c: jnp.broadcast_to(gate_ref[:, c:c + 1], (QT, LANES))
        return jnp.where(low, bc(ca), bc(cb))

    def merged(o, g, pr):
        return o[2 * pr * QT:(2 * pr + 1) * QT] + pltpu.roll(o[(2 * pr + 1) * QT:(2 * pr + 2) * QT], HEAD_DIM, 1)

    def normalise(pv):
        return pv[:, :LANES] * (1.0 / pv[:, LANES:])

    def tab(g, lo, width):
        return wtab_ref[Q_PER_KV * g:Q_PER_KV * (g + 1), :, lo:lo + width].reshape(rq, width)

    def queries(g):
        return jnp.concatenate([q_ref[:, (Q_PER_KV * g + r) * LANES:(Q_PER_KV * g + r + 1) * LANES]
                                for r in range(Q_PER_KV)], axis=0)

    cmp_p, pv_cs, picked, win = {}, {}, {}, {}
    w_tiles = [tile(i - (n_win - 1) + j) for j in range(n_win)]

    def cmp_softmax(g):
        sc = jnp.dot(queries(g), kc_ref[0, g], preferred_element_type=F32)
        sc = sc + ctab_ref[Q_PER_KV * g:Q_PER_KV * (g + 1)].reshape(rq, sc.shape[-1])
        e = jnp.exp2(sc - jnp.max(sc, axis=-1, keepdims=True))
        row_ok = i * QT + lax.broadcasted_iota(jnp.int32, (QT, sc.shape[-1]), 0) >= CMP_LEN - 1
        cmp_p[g] = jnp.where(jnp.concatenate([row_ok] * Q_PER_KV, axis=0),
                             e * (1.0 / jnp.sum(e, axis=-1, keepdims=True)), 0.0)

    def cmp_out(g):
        p = cmp_p[g]
        pv_cs[g] = jnp.dot(p.astype(BF16), vc_ref[0, g], preferred_element_type=F32)
        psum = p[0:QT] + p[QT:2 * QT] + p[2 * QT:3 * QT] + p[3 * QT:4 * QT]
        ps_t = sum(jnp.dot(ovlt_ref[...], t, preferred_element_type=F32) for t in _split3(psum.T))
        cmp_p[g] = jnp.where(forced, BIG, jnp.where(causal, ps_t[SEL_LANE0:SEL_LANE0 + n_sel], -BIG))

    def select(g):
        score = cmp_p[g]
        pick = jnp.zeros((n_sel, QT), F32)
        for _ in range(n_top):
            mx = jnp.max(score, axis=0, keepdims=True)
            first = jnp.min(jnp.where(score == mx, blkf, float(LANES)), axis=0, keepdims=True)
            hit = blkf == first
            pick = jnp.where(hit, 1.0, pick)
            score = jnp.where(hit, -jnp.inf, score)
        picked[g] = pick

    def win_logits(g):
        k_w = jnp.concatenate([kw_ref[0, g, t] for t in w_tiles], axis=1)
        s_w = jnp.dot(queries(g), k_w, preferred_element_type=F32)
        win[g] = [s_w[:, :QT] + tab(g, 0, QT), s_w[:, QT:(n_win - 2) * QT],
                  s_w[:, (n_win - 2) * QT:] + tab(g, QT, 2 * QT)]

    def win_softmax(g):
        mw = jnp.max(jnp.concatenate(win[g], axis=1), axis=-1, keepdims=True)
        win[g] = jnp.concatenate([jnp.exp2(t - mw).astype(BF16) for t in win[g]], axis=1)

    def win_pv(g):
        v_w = jnp.concatenate([vw_ref[0, g, t] for t in w_tiles], axis=0)
        win[g] = normalise(jnp.dot(win[g], v_w, preferred_element_type=F32))

    def win_out(g):
        o_w = win[g]
        for pr in range(Q_PER_KV // 2):
            tl = Q_PER_KV * g // 2 + pr
            out_scr[:, tl * LANES:(tl + 1) * LANES] = (gate_tile(0, tl) * merged(pv_cs[g], g, pr)
                                                       + gate_tile(2, tl) * merged(o_w, g, pr))

    def near_logits(g):
        q = queries(g)
        selb_t = jnp.where(picked[g] == 0.0, NEG_INF, 0.0)
        selb = jnp.concatenate([jnp.zeros((SEL_LANE0, QT), F32), selb_t,
                                jnp.zeros((LANES - SEL_LANE0 - n_sel, QT), F32)], axis=0).T
        selb_far = jnp.where(far_cut, NEG_INF, selb)
        q_near = jnp.where(sel_lane4, jnp.concatenate([selb.astype(BF16)] * Q_PER_KV, axis=0), q)
        qf_scr[g] = jnp.where(sel_lane4, jnp.concatenate([selb_far.astype(BF16)] * Q_PER_KV, axis=0), q)
        k_n = jnp.concatenate([ks_ref[0, g, tile(i - 1)], ks_ref[0, g, i]], axis=1)
        s_n = jnp.dot(q_near, k_n, preferred_element_type=F32) + tab(g, QT, 2 * QT)
        sn_scr[g] = s_n
        m_scr[g] = jnp.maximum(s_n[:, :QT], s_n[:, QT:])

    stages = (cmp_softmax, win_logits, cmp_out, win_softmax, select, win_pv, near_logits, win_out)
    for step in range(N_KV_HEADS + len(stages) - 1):
        for k, stage in enumerate(stages):
            if 0 <= step - k < N_KV_HEADS:
                stage(step - k)

    n_chunks = (jnp.maximum(i - 1, 0) + FAR_TILES - 1) // FAR_TILES

    def chunk_loop(body):
        def pair(p, carry):
            body(2 * p)
            body(2 * p + 1)
            return carry

        lax.fori_loop(0, n_chunks // 2, pair, 0)

        @pl.when(n_chunks % 2 == 1)
        def _():
            body(n_chunks - 1)

    def far_logits(c):
        for g in range(N_KV_HEADS):
            k_f = jnp.concatenate([ks_ref[0, g, FAR_TILES * c + t] for t in range(FAR_TILES)], axis=1)
            s = jnp.dot(qf_scr[g], k_f, preferred_element_type=F32)
            s_scr[g, c] = s
            m = m_scr[g]
            for t in range(FAR_TILES):
                m = jnp.maximum(m, s[:, t * QT:(t + 1) * QT])
            m_scr[g] = m

    chunk_loop(far_logits)

    near_p = {}

    def near_softmax(g):
        m = jnp.max(m_scr[g], axis=-1, keepdims=True)
        m_scr[g] = jnp.broadcast_to(m, (rq, QT))
        near_p[g] = jnp.exp2(sn_scr[g] - m).astype(BF16)

    def near_pv(g):
        v_n = jnp.concatenate([vs_ref[0, g, tile(i - 1)], vs_ref[0, g, i]], axis=0)
        acc_scr[g] = jnp.dot(near_p[g], v_n, preferred_element_type=F32)

    for step in range(N_KV_HEADS + 1):
        if step < N_KV_HEADS:
            near_softmax(step)
        if step > 0:
            near_pv(step - 1)

    def far_pv(c):
        for g in range(N_KV_HEADS):
            m = jnp.concatenate([m_scr[g]] * FAR_TILES, axis=1)
            p = jnp.exp2(s_scr[g, c] - m).astype(BF16)
            v_f = jnp.concatenate([vs_ref[0, g, FAR_TILES * c + t] for t in range(FAR_TILES)], axis=0)
            acc_scr[g] += jnp.dot(p, v_f, preferred_element_type=F32)

    chunk_loop(far_pv)

    for g in range(N_KV_HEADS):
        o_s = normalise(acc_scr[g])
        for pr in range(Q_PER_KV // 2):
            tl = Q_PER_KV * g // 2 + pr
            o = out_scr[:, tl * LANES:(tl + 1) * LANES] + gate_tile(1, tl) * merged(o_s, g, pr)
            o_ref[:, tl * LANES:(tl + 1) * LANES] = o.astype(BF16)


def _rel_bucket(dist):
    n = jnp.maximum(dist, 0)
    max_exact = NUM_BUCKETS // 2
    logv = jnp.log(jnp.maximum(n, 1).astype(F32) / max_exact) / math.log(MAX_DISTANCE / max_exact)
    large = jnp.minimum(max_exact + (logv * (NUM_BUCKETS - max_exact)).astype(jnp.int32), NUM_BUCKETS - 1)
    return jnp.where(n < max_exact, n, large)


def _bias_tables(rel_bias, seq):
    nt = seq // QT
    nc = seq // CMP_STRIDE
    cpt = QT // CMP_STRIDE
    rb = rel_bias * LOG2E

    def f(dist):
        onehot = (_rel_bucket(dist)[..., None] == jnp.arange(NUM_BUCKETS)).astype(F32)
        return jnp.einsum('...k,kh->h...', onehot, rb, precision=lax.Precision.HIGHEST)

    c_rel = jnp.arange(-cpt * (nt - 1), nc)
    dc = jnp.arange(QT)[:, None] - (c_rel * CMP_STRIDE + CMP_LEN - 1)[None, :]
    t0 = jnp.where((dc >= 0)[None], f(dc), NEG_INF)
    ctab = jnp.stack([t0[:, :, cpt * (nt - 1 - i):cpt * (nt - 1 - i) + nc] for i in range(nt)], axis=1)
    ctab = ctab.reshape(N_HEADS, seq, nc)
    far = rb[NUM_BUCKETS - 1].reshape(N_HEADS, 1, 1)
    dq = jnp.arange(QT)[:, None] - jnp.arange(QT)[None, :]
    d0, d3, d4 = WIN + dq, QT + dq, dq
    wtab = jnp.concatenate([jnp.where((d0 < WIN)[None], f(d0) - far, NEG_INF), f(d3) - far,
                            jnp.where((d4 >= 0)[None], f(d4) - far, NEG_INF)], axis=2)
    return ctab, wtab


def _overlap_matrix_t(seq):
    nc = seq // CMP_STRIDE
    n_sel = seq // SEL_LEN
    c_start = jnp.arange(nc) * CMP_STRIDE
    j_start = jnp.arange(n_sel) * SEL_LEN
    ov = jnp.clip(jnp.minimum(c_start[None, :] + CMP_LEN, j_start[:, None] + SEL_LEN)
                  - jnp.maximum(c_start[None, :], j_start[:, None]), 0, None)
    ov = ov.astype(F32) / CMP_LEN
    return jnp.pad(ov, ((SEL_LANE0, LANES - SEL_LANE0 - n_sel), (0, 0))).astype(BF16)


def _attention(q, gates, kv, rel_bias, bsz, seq):
    kc, vc, ks, vs, kw, vw = kv
    nt = seq // QT
    nc = seq // CMP_STRIDE
    n_sel = seq // SEL_LEN
    n_top = min(SEL_TOPN, n_sel)
    n_chunk = max(-(-(nt - 2) // FAR_TILES), 1)
    assert n_chunk * FAR_TILES <= nt
    nts = nt + PAD_TILES
    ctab, wtab = _bias_tables(rel_bias, seq)
    rq = Q_PER_KV * QT
    per_batch = lambda shape: pl.BlockSpec((1,) + shape, lambda b, i: (b,) + (0,) * len(shape),
                                           pipeline_mode=pl.Buffered(1))
    steps = nt // ATTN_TILES
    tok = lambda c: pl.BlockSpec((ATTN_TILES * QT, c), lambda b, i: (b * steps + i, 0))
    return pl.pallas_call(
        functools.partial(_attn_body, n_sel=n_sel, n_top=n_top, nt=nt),
        grid=(bsz, steps),
        in_specs=[tok(N_HEADS * LANES), tok(LANES),
                  per_batch((N_KV_HEADS, LANES, nc)),
                  per_batch((N_KV_HEADS, nc, LANES)),
                  per_batch((N_KV_HEADS, nts, LANES, QT)),
                  per_batch((N_KV_HEADS, nts, QT, 2 * LANES)),
                  per_batch((N_KV_HEADS, nts, LANES, QT)),
                  per_batch((N_KV_HEADS, nts, QT, 2 * LANES)),
                  pl.BlockSpec((N_HEADS, ATTN_TILES * QT, nc), lambda b, i: (0, i, 0)),
                  _const_spec((N_HEADS, QT, 3 * QT)),
                  _const_spec((LANES, nc))],
        out_specs=tok(D_MODEL),
        out_shape=jax.ShapeDtypeStruct((bsz * seq, D_MODEL), BF16),
        scratch_shapes=[pltpu.VMEM((N_KV_HEADS, n_chunk, rq, FAR_TILES * QT), F32),
                        pltpu.VMEM((N_KV_HEADS, rq, 2 * QT), F32),
                        pltpu.VMEM((N_KV_HEADS, rq, QT), F32),
                        pltpu.VMEM((N_KV_HEADS, rq, LANES), BF16),
                        pltpu.VMEM((N_KV_HEADS, rq, 2 * LANES), F32),
                        pltpu.VMEM((QT, D_MODEL), F32)],
        compiler_params=_cparams(("parallel", "arbitrary")),
        name="nsa_attn",
    )(q, gates, kc, vc, ks, vs, kw, vw, ctab, wtab, _overlap_matrix_t(seq))


def kernel(x, rel_bias, ffn1_norm, ffn1_w_in, ffn1_w_out, mix_norm, ffn2_norm, ffn2_w_in, ffn2_w_out,
           s5_a_re, s5_a_im, s5_log_dt, s5_b_re, s5_b_im, s5_c_re, s5_c_im, s5_d, s5_w_glu,
           kv_norm, w_kv, k_norm_cmp, k_norm_slc, k_norm_win, cmp_pos_k, cmp_pos_v,
           cmp_k_w1, cmp_k_w2, cmp_v_w1, cmp_v_w2, w_qg, q_norm, w_o):
    bsz, seq, _ = x.shape
    depth = ffn1_norm.shape[0]
    n_a = s5_a_re.shape[0]
    h = x.reshape(bsz * seq, D_MODEL)
    kv = None
    ffn1_w_in, ffn1_w_out, ffn2_w_in, ffn2_w_out = (
        w.astype(BF16) for w in (ffn1_w_in, ffn1_w_out, ffn2_w_in, ffn2_w_out))
    for layer in range(depth):
        s5_layer = layer < n_a
        h = _ffn(h, ffn1_norm[layer], ffn1_w_in, ffn1_w_out, layer, bsz, seq,
                 in_tm=False, out_tm=s5_layer)
        pre = None
        if s5_layer:
            a = layer
            h = _s5(h, mix_norm[layer], s5_a_re[a], s5_a_im[a], s5_log_dt[a], s5_b_re[a], s5_b_im[a],
                    s5_c_re[a], s5_c_im[a], s5_d[a], s5_w_glu[a], bsz, seq)
        else:
            b = layer - n_a
            q, gates = _qproj(h, mix_norm[layer], w_qg[b], q_norm[b], rel_bias, bsz, seq)
            pre = (_attention(q, gates, kv, rel_bias, bsz, seq), w_o[b])
        h = _ffn(h, ffn2_norm[layer], ffn2_w_in, ffn2_w_out, layer, bsz, seq,
                 in_tm=s5_layer, out_tm=False, pre=pre)
        if layer == n_a - 1:
            kc_raw, vc_raw, ks, vs, kw, vw = _kv_proj(h, kv_norm, w_kv, k_norm_slc, k_norm_win, bsz, seq)
            kc, vc = _compress(kc_raw, vc_raw, k_norm_cmp, cmp_pos_k, cmp_pos_v,
                               cmp_k_w1, cmp_k_w2, cmp_v_w1, cmp_v_w2, bsz, seq)
            kv = (kc, vc, ks, vs, kw, vw)
    return h.reshape(bsz, seq, D_MODEL)
```

```python
import functools
import math

import jax
import jax.numpy as jnp
from jax import lax
from jax.experimental import pallas as pl
from jax.experimental.pallas import tpu as pltpu

F32 = jnp.float32
BF16 = jnp.bfloat16

D_MODEL = 1024
D_FF = 2816
RMS_EPS = 1e-6
S5_GROUP_CH = 16
S5_GROUPS = D_MODEL // S5_GROUP_CH
S5_STATE = 64
N_HEADS = 16
HEAD_DIM = 64
N_KV_HEADS = 4
Q_PER_KV = N_HEADS // N_KV_HEADS
CMP_LEN = 32
CMP_STRIDE = 16
CMP_HIDDEN = 2 * HEAD_DIM
SEL_LEN = 64
SEL_TOPN = 8
WIN = 512
N_BRANCH = 3
ATTN_SCALE = HEAD_DIM ** -0.5
NUM_BUCKETS = 32
MAX_DISTANCE = 128
NEG_INF = -1e30
BIG = 1e9

LANES = 128
MXU_DIM = 256
VMEM_LIMIT = 56 * 1024 * 1024
ATTN_VMEM_LIMIT = 61 * 1024 * 1024

FFN_ROWS = 1024
FFN_BATCH = 8
FF_CHUNK = MXU_DIM
S5_TC = 16
S5_SLAB = MXU_DIM
S5_LANES = 256
QP_ROWS = 1024
QT = 128
ATTN_TILES = 2
FAR_TILES = 4
PAD_TILES = WIN // QT
LOG2E = math.log2(math.e)

SEL_LANE0 = HEAD_DIM
FARB_LANE0 = 96
PAD_LANE = 99


def _rms(x, n):
    ss = jnp.sum(x * x, axis=-1, keepdims=True)
    return x * lax.rsqrt(ss * (1.0 / n) + RMS_EPS)


def _cparams(sem, vmem_limit=VMEM_LIMIT):
    return pltpu.CompilerParams(dimension_semantics=sem, vmem_limit_bytes=vmem_limit)


def _const_spec(shape):
    nd = len(shape)
    return pl.BlockSpec(shape, lambda *_: (0,) * nd, pipeline_mode=pl.Buffered(1))


def _split3(x):
    x1 = x.astype(BF16)
    r1 = x - x1.astype(F32)
    x2 = r1.astype(BF16)
    x3 = (r1 - x2.astype(F32)).astype(BF16)
    return x1, x2, x3


def _ffn_body(*refs, pre, in_tm, out_tm):
    if pre:
        x_ref, po_ref, pw_ref, g_ref, win_ref, wout_ref, o_ref, act_ref = refs
    else:
        x_ref, g_ref, win_ref, wout_ref, o_ref, act_ref = refs
    rows = act_ref.shape[0]
    x = (jnp.swapaxes(x_ref[...], 0, 1) if in_tm else x_ref[...]).reshape(rows, D_MODEL)
    if pre:
        x = x + jnp.dot(po_ref[...].reshape(rows, D_MODEL), pw_ref[...], preferred_element_type=F32)
    xn = (_rms(x, D_MODEL) * g_ref[...]).astype(BF16)
    for c in range(D_FF // FF_CHUNK):
        lo = c * FF_CHUNK
        a = jnp.dot(xn, win_ref[:, lo:lo + FF_CHUNK], preferred_element_type=F32)
        b = jnp.dot(xn, win_ref[:, D_FF + lo:D_FF + lo + FF_CHUNK], preferred_element_type=F32)
        act_ref[:, lo:lo + FF_CHUNK] = (a * jax.nn.sigmoid(a) * b).astype(BF16)
    y = jnp.dot(act_ref[...], wout_ref[...], preferred_element_type=F32)
    res = x + 0.5 * y
    nb = o_ref.shape[1] if out_tm else o_ref.shape[0]
    res = res.reshape(nb, rows // nb, D_MODEL)
    o_ref[...] = jnp.swapaxes(res, 0, 1) if out_tm else res


def _token_spec(tb, tt, time_major):
    if time_major:
        return pl.BlockSpec((tt, tb, D_MODEL), lambda b, t: (t, b, 0))
    return pl.BlockSpec((tb, tt, D_MODEL), lambda b, t: (b, t, 0))


def _layer_spec(shape, layer):
    return pl.BlockSpec((None,) + shape, lambda *_: (layer, 0, 0), pipeline_mode=pl.Buffered(1))


def _ffn(h, gain, w_in, w_out, layer, bsz, seq, in_tm, out_tm, pre=None):
    tb = FFN_BATCH if (in_tm or out_tm) else 1
    tt = min(FFN_ROWS // tb, seq)
    rows = tb * tt
    as_bm = lambda a: a.reshape(bsz, seq, D_MODEL)
    out_shape = (seq, bsz, D_MODEL) if out_tm else (bsz, seq, D_MODEL)
    args, specs = [h if in_tm else as_bm(h)], [_token_spec(tb, tt, in_tm)]
    if pre is not None:
        args += [as_bm(pre[0]), pre[1].astype(BF16)]
        specs += [_token_spec(tb, tt, False), _const_spec((D_MODEL, D_MODEL))]
    args += [gain.reshape(1, D_MODEL), w_in, w_out]
    specs += [_const_spec((1, D_MODEL)), _layer_spec((D_MODEL, 2 * D_FF), layer),
              _layer_spec((D_FF, D_MODEL), layer)]
    out = pl.pallas_call(
        functools.partial(_ffn_body, pre=pre is not None, in_tm=in_tm, out_tm=out_tm),
        grid=(bsz // tb, seq // tt),
        in_specs=specs,
        out_specs=_token_spec(tb, tt, out_tm),
        out_shape=jax.ShapeDtypeStruct(out_shape, F32),
        scratch_shapes=[pltpu.VMEM((rows, D_FF), BF16)],
        compiler_params=_cparams(("parallel", "parallel")),
        name="ffn",
    )(*args)
    return out if out_tm else out.reshape(bsz * seq, D_MODEL)


def _s5_body(h_ref, g_ref, bmat_ref, cmat_ref, are_ref, aim_ref, d_ref, wglu_ref,
             o_ref, bu_ref, xb_ref, st_ref, y_ref, *, tc, nb):
    n_slab = D_MODEL // S5_SLAB
    half = S5_SLAB // S5_GROUP_CH * S5_STATE
    rows = tc * nb

    @pl.when(pl.program_id(0) == 0)
    def _():
        st_ref[...] = jnp.zeros_like(st_ref)

    h = h_ref[...].reshape(rows, D_MODEL)
    u = _rms(h, D_MODEL) * g_ref[...]
    ub = u.astype(BF16)

    def project(sl):
        bu_ref[sl % 2] = jnp.dot(ub[:, sl * S5_SLAB:(sl + 1) * S5_SLAB], bmat_ref[sl],
                                 preferred_element_type=F32)

    project(0)
    for sl in range(n_slab):
        if sl + 1 < n_slab:
            project(sl + 1)
        bu, xb = bu_ref.at[sl % 2], xb_ref.at[sl % 2]
        for wb in range(half // S5_LANES):
            re = slice(wb * S5_LANES, (wb + 1) * S5_LANES)
            im = slice(half + wb * S5_LANES, half + (wb + 1) * S5_LANES)
            ar = jnp.broadcast_to(are_ref[sl, :, re], (nb, S5_LANES))
            ai = jnp.broadcast_to(aim_ref[sl, :, re], (nb, S5_LANES))
            xr, xi = st_ref[sl, :, re], st_ref[sl, :, im]
            for t in range(tc):
                rs = slice(t * nb, (t + 1) * nb)
                xr, xi = ar * xr - ai * xi + bu[rs, re], ar * xi + ai * xr + bu[rs, im]
                xb[rs, re] = xr.astype(BF16)
                xb[rs, im] = xi.astype(BF16)
            st_ref[sl, :, re] = xr
            st_ref[sl, :, im] = xi
        y_ref[:, sl * S5_SLAB:(sl + 1) * S5_SLAB] = jnp.dot(
            xb_ref[sl % 2], cmat_ref[sl], preferred_element_type=F32)
    y = jax.nn.gelu(y_ref[...] + d_ref[...] * u)
    gate = jnp.dot(y.astype(BF16), wglu_ref[...], preferred_element_type=F32)
    o_ref[...] = (h + y * jax.nn.sigmoid(gate)).reshape(tc, nb, D_MODEL)


def _s5_params(a_re, a_im, log_dt, b_re, b_im, c_re, c_im):
    dt = jnp.exp(log_dt)[:, None]
    mag = jnp.exp(a_re * dt)
    ab_re = mag * jnp.cos(a_im * dt)
    ab_im = mag * jnp.sin(a_im * dt)
    den = a_re * a_re + a_im * a_im
    z_re = ((ab_re - 1.0) * a_re + ab_im * a_im) / den
    z_im = (ab_im * a_re - (ab_re - 1.0) * a_im) / den
    bb_re = z_re[..., None] * b_re - z_im[..., None] * b_im
    bb_im = z_re[..., None] * b_im + z_im[..., None] * b_re
    n_slab = D_MODEL // S5_SLAB
    gps = S5_GROUPS // n_slab
    eye = jnp.eye(gps, dtype=F32)

    def in_mat(bb):
        bb = bb.reshape(n_slab, gps, S5_STATE, S5_GROUP_CH)
        m = jnp.einsum('sgph,gk->sghkp', bb, eye)
        return m.reshape(n_slab, gps * S5_GROUP_CH, gps * S5_STATE)

    def out_mat(cc):
        cc = cc.reshape(n_slab, gps, S5_GROUP_CH, S5_STATE)
        m = jnp.einsum('sghp,gk->sgpkh', cc, eye)
        return m.reshape(n_slab, gps * S5_STATE, gps * S5_GROUP_CH)

    bmat = jnp.concatenate([in_mat(bb_re), in_mat(bb_im)], axis=2).astype(BF16)
    cmat = jnp.concatenate([out_mat(c_re), out_mat(-c_im)], axis=1).astype(BF16)
    are = ab_re.reshape(n_slab, 1, gps * S5_STATE)
    aim = ab_im.reshape(n_slab, 1, gps * S5_STATE)
    return bmat, cmat, are, aim


def _s5(h_tm, gain, a_re, a_im, log_dt, b_re, b_im, c_re, c_im, d_skip, w_glu, bsz, seq):
    tc = min(S5_TC, seq)
    n_slab = D_MODEL // S5_SLAB
    half = S5_SLAB // S5_GROUP_CH * S5_STATE
    bmat, cmat, are, aim = _s5_params(a_re, a_im, log_dt, b_re, b_im, c_re, c_im)
    blk = pl.BlockSpec((tc, bsz, D_MODEL), lambda i: (i, 0, 0))
    out = pl.pallas_call(
        functools.partial(_s5_body, tc=tc, nb=bsz),
        grid=(seq // tc,),
        in_specs=[blk,
                  _const_spec((1, D_MODEL)),
                  _const_spec((n_slab, S5_SLAB, 2 * half)),
                  _const_spec((n_slab, 2 * half, S5_SLAB)),
                  _const_spec((n_slab, 1, half)),
                  _const_spec((n_slab, 1, half)),
                  _const_spec((1, D_MODEL)),
                  _const_spec((D_MODEL, D_MODEL))],
        out_specs=blk,
        out_shape=jax.ShapeDtypeStruct((seq, bsz, D_MODEL), F32),
        scratch_shapes=[pltpu.VMEM((2, tc * bsz, 2 * half), F32),
                        pltpu.VMEM((2, tc * bsz, 2 * half), BF16),
                        pltpu.VMEM((n_slab, bsz, 2 * half), F32),
                        pltpu.VMEM((tc * bsz, D_MODEL), F32)],
        compiler_params=_cparams(("arbitrary",)),
        name="s5",
    )(h_tm, gain.reshape(1, D_MODEL), bmat, cmat, are, aim,
      d_skip.reshape(1, D_MODEL), w_glu.astype(BF16))
    return out


def _kv_body(h_ref, g_ref, w_ref, gk_ref, kc_ref, vc_ref, ks_ref, vs_ref, kw_ref, vw_ref, cs_ref,
             *, n_real):
    s = pl.program_id(1)
    gw = N_KV_HEADS * HEAD_DIM
    pw = N_KV_HEADS * LANES
    tiles = ks_ref.shape[2]
    row = lax.broadcasted_iota(jnp.int32, (LANES, QT), 0)
    key_hi = (lax.broadcasted_iota(jnp.int32, (LANES, QT), 1) >= SEL_LEN).astype(jnp.int32)
    farb_rows = ((row >= FARB_LANE0) & (row < FARB_LANE0 + 3)).astype(F32)
    ones = jnp.ones((QT, LANES), BF16)

    @pl.when(s < n_real)
    def _():
        hn = (_rms(h_ref[...], D_MODEL) * g_ref[...]).astype(BF16)
        proj = lambda lo, n: jnp.dot(hn, w_ref[:, lo:lo + n], preferred_element_type=F32)
        kvc = proj(0, 2 * gw)
        kv_all = [(proj(2 * gw + br * 2 * pw, pw), proj(2 * gw + br * 2 * pw + pw, pw)) for br in range(2)]
        n_hb = kvc.shape[0] // CMP_STRIDE
        for j in range(2 * gw // LANES):
            cs_ref[j] = kvc[:, j * LANES:(j + 1) * LANES]
        for c_ref, j0 in ((kc_ref, 0), (vc_ref, gw // LANES)):
            for l in range(CMP_STRIDE):
                for j in range(gw // LANES):
                    col = l * gw + j * LANES
                    c_ref[:, col:col + LANES] = cs_ref[j0 + j, pl.ds(l, n_hb, stride=CMP_STRIDE), :].astype(BF16)
        for br, (k_ref, v_ref) in enumerate(((ks_ref, vs_ref), (kw_ref, vw_ref))):
            k_all, v_all = kv_all[br]
            for g in range(N_KV_HEADS):
                k = _rms(k_all[:, g * LANES:(g + 1) * LANES], HEAD_DIM) * gk_ref[br]
                v = v_all[:, g * LANES:(g + 1) * LANES].astype(BF16)
                for t in range(tiles):
                    extra = farb_rows
                    if br == 0:
                        blk = 2 * (tiles * s + t) + key_hi
                        extra = extra + (row - SEL_LANE0 == blk).astype(F32)
                    k_ref[0, g, t] = (k[t * QT:(t + 1) * QT].T + extra).astype(BF16)
                    v_ref[0, g, t] = jnp.concatenate([v[t * QT:(t + 1) * QT], ones], axis=1)

    @pl.when(s == n_real)
    def _():
        pad_k = jnp.broadcast_to((row == PAD_LANE).astype(BF16), (N_KV_HEADS, tiles, LANES, QT))
        pad_v = jnp.zeros((N_KV_HEADS, tiles, QT, 2 * LANES), BF16)
        for k_ref, v_ref in ((ks_ref, vs_ref), (kw_ref, vw_ref)):
            k_ref[0] = pad_k
            v_ref[0] = pad_v


def _pad_heads(w, n_heads):
    k = w.shape[0]
    w = w.reshape(k, n_heads, HEAD_DIM)
    return jnp.pad(w, ((0, 0), (0, 0), (0, LANES - HEAD_DIM))).reshape(k, n_heads * LANES)


def _kv_proj(h, kv_norm, w_kv, k_norm_slc, k_norm_win, bsz, seq):
    gw = N_KV_HEADS * HEAD_DIM
    tiles = PAD_TILES
    rows = tiles * QT
    n_real = seq // rows
    nt = seq // QT
    wk = w_kv.reshape(D_MODEL, 2 * N_BRANCH, gw)
    w = jnp.concatenate([wk[:, 0], wk[:, 1]] + [_pad_heads(wk[:, j], N_KV_HEADS) for j in (2, 3, 4, 5)],
                        axis=1).astype(BF16)
    gk = jnp.pad(jnp.stack([k_norm_slc, k_norm_win]), ((0, 0), (0, LANES - HEAD_DIM)))
    gk = gk.reshape(2, 1, LANES)
    n = bsz * seq
    real = lambda s: jnp.minimum(s, n_real - 1)
    tok = lambda c: pl.BlockSpec((rows, c), lambda b, s: (b * n_real + real(s), 0))
    tile_spec = lambda r, c: pl.BlockSpec((1, N_KV_HEADS, tiles, r, c), lambda b, s: (b, 0, s, 0, 0))
    hb_spec = pl.BlockSpec((rows // CMP_STRIDE, CMP_STRIDE * gw), lambda b, s: (b * n_real + real(s), 0))
    dense = jax.ShapeDtypeStruct((n // CMP_STRIDE, CMP_STRIDE * gw), BF16)
    k_tiles = jax.ShapeDtypeStruct((bsz, N_KV_HEADS, nt + tiles, LANES, QT), BF16)
    v_tiles = jax.ShapeDtypeStruct((bsz, N_KV_HEADS, nt + tiles, QT, 2 * LANES), BF16)
    return pl.pallas_call(
        functools.partial(_kv_body, n_real=n_real),
        grid=(bsz, n_real + 1),
        in_specs=[tok(D_MODEL), _const_spec((1, D_MODEL)), _const_spec(w.shape),
                  _const_spec((2, 1, LANES))],
        out_specs=[hb_spec, hb_spec, tile_spec(LANES, QT), tile_spec(QT, 2 * LANES),
                   tile_spec(LANES, QT), tile_spec(QT, 2 * LANES)],
        out_shape=[dense, dense, k_tiles, v_tiles, k_tiles, v_tiles],
        scratch_shapes=[pltpu.VMEM((2 * gw // LANES, rows, LANES), F32)],
        compiler_params=_cparams(("parallel", "arbitrary")),
        name="kv_proj",
    )(h, kv_norm.reshape(1, D_MODEL), w, gk)


def _cmp_body(xk_ref, xv_ref, w1k_ref, w1v_ref, bk_ref, bv_ref, w2k_ref, w2v_ref, gk_ref,
              kc_ref, vc_ref, *, nh):
    ab_k = jnp.dot(xk_ref[0], w1k_ref[...], preferred_element_type=F32)
    ab_v = jnp.dot(xv_ref[0], w1v_ref[...], preferred_element_type=F32)
    for g in range(N_KV_HEADS):
        outs = []
        for ab_all, b_ref, w2_ref in ((ab_k, bk_ref, w2k_ref), (ab_v, bv_ref, w2v_ref)):
            ab = ab_all[:, g * 2 * CMP_HIDDEN:(g + 1) * 2 * CMP_HIDDEN]
            hid = ab[:, :CMP_HIDDEN] + pltpu.roll(ab[:, CMP_HIDDEN:], nh - 1, 0) + b_ref[...]
            hid = jax.nn.gelu(hid).astype(BF16)
            outs.append(jnp.dot(hid, w2_ref[...], preferred_element_type=F32))
        k = _rms(outs[0], HEAD_DIM) * gk_ref[...]
        kc_ref[0, g] = k.T.astype(BF16)
        vc_ref[0, g] = outs[1].astype(BF16)


def _compress(kc_raw, vc_raw, k_norm_cmp, pos_k, pos_v, k_w1, k_w2, v_w1, v_w2, bsz, seq):
    nh = seq // CMP_STRIDE
    hb = CMP_STRIDE * N_KV_HEADS * HEAD_DIM
    eye = jnp.eye(N_KV_HEADS, dtype=F32)

    def halfblocks(x):
        return x.reshape(bsz, nh, hb)

    def w1cat(w1):
        w = w1.reshape(2, CMP_STRIDE, HEAD_DIM, CMP_HIDDEN)
        w = jnp.concatenate([w[0], w[1]], axis=-1)
        w = jnp.einsum('ldj,gh->lgdhj', w, eye)
        return w.reshape(hb, N_KV_HEADS * 2 * CMP_HIDDEN).astype(BF16)

    def w2pad(w2):
        return jnp.pad(w2, ((0, 0), (0, LANES - HEAD_DIM))).astype(BF16)

    bias = lambda pos, w1: jnp.einsum('ld,ldh->h', pos, w1,
                                      precision=lax.Precision.HIGHEST).reshape(1, CMP_HIDDEN)
    gk = jnp.pad(k_norm_cmp, (0, LANES - HEAD_DIM)).reshape(1, LANES)
    xspec = pl.BlockSpec((1, nh, hb), lambda b: (b, 0, 0))
    ospec = pl.BlockSpec((1, N_KV_HEADS, LANES, nh), lambda b: (b, 0, 0, 0))
    vspec = pl.BlockSpec((1, N_KV_HEADS, nh, LANES), lambda b: (b, 0, 0, 0))
    w1_shape = (hb, N_KV_HEADS * 2 * CMP_HIDDEN)
    return pl.pallas_call(
        functools.partial(_cmp_body, nh=nh),
        grid=(bsz,),
        in_specs=[xspec, xspec, _const_spec(w1_shape), _const_spec(w1_shape),
                  _const_spec((1, CMP_HIDDEN)), _const_spec((1, CMP_HIDDEN)),
                  _const_spec((CMP_HIDDEN, LANES)), _const_spec((CMP_HIDDEN, LANES)),
                  _const_spec((1, LANES))],
        out_specs=[ospec, vspec],
        out_shape=[jax.ShapeDtypeStruct((bsz, N_KV_HEADS, LANES, nh), BF16),
                   jax.ShapeDtypeStruct((bsz, N_KV_HEADS, nh, LANES), BF16)],
        compiler_params=_cparams(("parallel",)),
        name="kv_compress",
    )(halfblocks(kc_raw), halfblocks(vc_raw), w1cat(k_w1), w1cat(v_w1),
      bias(pos_k, k_w1), bias(pos_v, v_w1), w2pad(k_w2), w2pad(v_w2), gk)


def _qproj_body(h_ref, g_ref, w_ref, qsc_ref, qc_ref, q_ref, gate_ref):
    u = (_rms(h_ref[...], D_MODEL) * g_ref[...]).astype(BF16)
    qg = jnp.dot(u, w_ref[...], preferred_element_type=F32)
    for hh in range(N_HEADS):
        sl = slice(hh * LANES, (hh + 1) * LANES)
        q_ref[:, sl] = (_rms(qg[:, sl], HEAD_DIM) * qsc_ref[:, sl] + qc_ref[:, sl]).astype(BF16)
    gate_ref[...] = jax.nn.sigmoid(qg[:, N_HEADS * LANES:])


def _qproj(h, mix_gain, w_qg, q_norm, rel_bias, bsz, seq):
    rows = min(QP_ROWS, seq)
    n = bsz * seq
    nq = N_HEADS * HEAD_DIM
    wq = jnp.concatenate([_pad_heads(w_qg[:, :nq], N_HEADS),
                          jnp.pad(w_qg[:, nq:], ((0, 0), (0, LANES - N_BRANCH * N_HEADS)))],
                         axis=1).astype(BF16)
    qsc = jnp.tile(jnp.pad(q_norm * (ATTN_SCALE * LOG2E), (0, LANES - HEAD_DIM)), N_HEADS)
    qsc = qsc.reshape(1, N_HEADS * LANES)
    far = _split3(rel_bias[NUM_BUCKETS - 1] * LOG2E)
    qc = jnp.zeros((N_HEADS, LANES), F32)
    for t, term in enumerate(far):
        qc = qc.at[:, FARB_LANE0 + t].set(term.astype(F32))
    qc = qc.at[:, PAD_LANE].set(NEG_INF).reshape(1, N_HEADS * LANES)
    tok = lambda c: pl.BlockSpec((rows, c), lambda i: (i, 0))
    return pl.pallas_call(
        _qproj_body,
        grid=(n // rows,),
        in_specs=[tok(D_MODEL), _const_spec((1, D_MODEL)), _const_spec(wq.shape),
                  _const_spec((1, N_HEADS * LANES)), _const_spec((1, N_HEADS * LANES))],
        out_specs=[tok(N_HEADS * LANES), tok(LANES)],
        out_shape=[jax.ShapeDtypeStruct((n, N_HEADS * LANES), BF16),
                   jax.ShapeDtypeStruct((n, LANES), F32)],
        compiler_params=_cparams(("parallel",)),
        name="q_proj",
    )(h, mix_gain.reshape(1, D_MODEL), wq, qsc, qc)


def _attn_body(q_ref, gate_ref, kc_ref, vc_ref, ks_ref, vs_ref, kw_ref, vw_ref, ctab_ref, wtab_ref, ovlt_ref,
               o_ref, *scratch, **static):
    def one_tile(j, carry):
        rows = pl.ds(pl.multiple_of(j * QT, QT), QT)
        _attn_tile(pl.program_id(1) * ATTN_TILES + j, q_ref.at[rows], gate_ref.at[rows],
                   kc_ref, vc_ref, ks_ref, vs_ref, kw_ref, vw_ref, ctab_ref.at[:, rows], wtab_ref, ovlt_ref,
                   o_ref.at[rows], *scratch, **static)
        return carry

    lax.fori_loop(0, ATTN_TILES, one_tile, 0)


def _attn_tile(i, q_ref, gate_ref, kc_ref, vc_ref, ks_ref, vs_ref, kw_ref, vw_ref,
               ctab_ref, wtab_ref, ovlt_ref,
               o_ref, s_scr, sn_scr, m_scr, qf_scr, acc_scr, out_scr, *, n_sel, n_top, nt):
    rq = Q_PER_KV * QT
    n_win = WIN // QT + 1
    lane = lax.broadcasted_iota(jnp.int32, (QT, LANES), 1)
    sel_lane = (lane >= SEL_LANE0) & (lane < SEL_LANE0 + n_sel)
    far_cut = sel_lane & (lane - SEL_LANE0 >= 2 * (i - 1))
    sel_lane4 = jnp.concatenate([sel_lane] * Q_PER_KV, axis=0)
    blk = lax.broadcasted_iota(jnp.int32, (n_sel, QT), 0)
    blkf = blk.astype(F32)
    qpos = i * QT + lax.broadcasted_iota(jnp.int32, (n_sel, QT), 1)
    cur = lax.shift_right_arithmetic(qpos, int(math.log2(SEL_LEN)))
    forced = (blk == 0) | (blk == cur) | (blk == cur - 1)
    causal = blk * SEL_LEN <= qpos
    tile = lambda t: jnp.where(t < 0, nt, t)

    low = lane < HEAD_DIM

    def gate_tile(b, tl):
        ca, cb = 2 * tl * N_BRANCH + b, (2 * tl + 1) * N_BRANCH + b
        bc = lambda c: jnp.broadcast_to(gate_ref[:, c:c + 1], (QT, LANES))
        return jnp.where(low, bc(ca), bc(cb))

    def merged(o, g, pr):
        return o[2 * pr * QT:(2 * pr + 1) * QT] + pltpu.roll(o[(2 * pr + 1) * QT:(2 * pr + 2) * QT], HEAD_DIM, 1)

    def normalise(pv):
        return pv[:, :LANES] * (1.0 / pv[:, LANES:])

    def tab(g, lo, width):
        return wtab_ref[Q_PER_KV * g:Q_PER_KV * (g + 1), :, lo:lo + width].reshape(rq, width)

    def queries(g):
        return jnp.concatenate([q_ref[:, (Q_PER_KV * g + r) * LANES:(Q_PER_KV * g + r + 1) * LANES]
                                for r in range(Q_PER_KV)], axis=0)

    cmp_p, pv_cs, picked, win = {}, {}, {}, {}
    w_tiles = [tile(i - (n_win - 1) + j) for j in range(n_win)]

    def cmp_softmax(g):
        sc = jnp.dot(queries(g), kc_ref[0, g], preferred_element_type=F32)
        sc = sc + ctab_ref[Q_PER_KV * g:Q_PER_KV * (g + 1)].reshape(rq, sc.shape[-1])
        e = jnp.exp2(sc - jnp.max(sc, axis=-1, keepdims=True))
        row_ok = i * QT + lax.broadcasted_iota(jnp.int32, (QT, sc.shape[-1]), 0) >= CMP_LEN - 1
        cmp_p[g] = jnp.where(jnp.concatenate([row_ok] * Q_PER_KV, axis=0),
                             e * (1.0 / jnp.sum(e, axis=-1, keepdims=True)), 0.0)

    def cmp_out(g):
        p = cmp_p[g]
        pv_cs[g] = jnp.dot(p.astype(BF16), vc_ref[0, g], preferred_element_type=F32)
        psum = p[0:QT] + p[QT:2 * QT] + p[2 * QT:3 * QT] + p[3 * QT:4 * QT]
        ps_t = sum(jnp.dot(ovlt_ref[...], t, preferred_element_type=F32) for t in _split3(psum.T))
        cmp_p[g] = jnp.where(forced, BIG, jnp.where(causal, ps_t[SEL_LANE0:SEL_LANE0 + n_sel], -BIG))

    def select(g):
        score = cmp_p[g]
        pick = jnp.zeros((n_sel, QT), F32)
        for _ in range(n_top):
            mx = jnp.max(score, axis=0, keepdims=True)
            first = jnp.min(jnp.where(score == mx, blkf, float(LANES)), axis=0, keepdims=True)
            hit = blkf == first
            pick = jnp.where(hit, 1.0, pick)
            score = jnp.where(hit, -jnp.inf, score)
        picked[g] = pick

    def win_logits(g):
        k_w = jnp.concatenate([kw_ref[0, g, t] for t in w_tiles], axis=1)
        s_w = jnp.dot(queries(g), k_w, preferred_element_type=F32)
        win[g] = [s_w[:, :QT] + tab(g, 0, QT), s_w[:, QT:(n_win - 2) * QT],
                  s_w[:, (n_win - 2) * QT:] + tab(g, QT, 2 * QT)]

    def win_softmax(g):
        mw = jnp.max(jnp.concatenate(win[g], axis=1), axis=-1, keepdims=True)
        win[g] = jnp.concatenate([jnp.exp2(t - mw).astype(BF16) for t in win[g]], axis=1)

    def win_pv(g):
        v_w = jnp.concatenate([vw_ref[0, g, t] for t in w_tiles], axis=0)
        win[g] = normalise(jnp.dot(win[g], v_w, preferred_element_type=F32))

    def win_out(g):
        o_w = win[g]
        for pr in range(Q_PER_KV // 2):
            tl = Q_PER_KV * g // 2 + pr
            out_scr[:, tl * LANES:(tl + 1) * LANES] = (gate_tile(0, tl) * merged(pv_cs[g], g, pr)
                                                       + gate_tile(2, tl) * merged(o_w, g, pr))

    def near_logits(g):
        q = queries(g)
        selb_t = jnp.where(picked[g] == 0.0, NEG_INF, 0.0)
        selb = jnp.concatenate([jnp.zeros((SEL_LANE0, QT), F32), selb_t,
                                jnp.zeros((LANES - SEL_LANE0 - n_sel, QT), F32)], axis=0).T
        selb_far = jnp.where(far_cut, NEG_INF, selb)
        q_near = jnp.where(sel_lane4, jnp.concatenate([selb.astype(BF16)] * Q_PER_KV, axis=0), q)
        qf_scr[g] = jnp.where(sel_lane4, jnp.concatenate([selb_far.astype(BF16)] * Q_PER_KV, axis=0), q)
        k_n = jnp.concatenate([ks_ref[0, g, tile(i - 1)], ks_ref[0, g, i]], axis=1)
        s_n = jnp.dot(q_near, k_n, preferred_element_type=F32) + tab(g, QT, 2 * QT)
        sn_scr[g] = s_n
        m_scr[g] = jnp.maximum(s_n[:, :QT], s_n[:, QT:])

    stages = (cmp_softmax, win_logits, cmp_out, win_softmax, select, win_pv, near_logits, win_out)
    for step in range(N_KV_HEADS + len(stages) - 1):
        for k, stage in enumerate(stages):
            if 0 <= step - k < N_KV_HEADS:
                stage(step - k)

    n_chunks = (jnp.maximum(i - 1, 0) + FAR_TILES - 1) // FAR_TILES

    def chunk_loop(body):
        def pair(p, carry):
            body(2 * p)
            body(2 * p + 1)
            return carry

        lax.fori_loop(0, n_chunks // 2, pair, 0)

        @pl.when(n_chunks % 2 == 1)
        def _():
            body(n_chunks - 1)

    def far_logits(c):
        for g in range(N_KV_HEADS):
            k_f = jnp.concatenate([ks_ref[0, g, FAR_TILES * c + t] for t in range(FAR_TILES)], axis=1)
            s = jnp.dot(qf_scr[g], k_f, preferred_element_type=F32)
            s_scr[g, c] = s
            m = m_scr[g]
            for t in range(FAR_TILES):
                m = jnp.maximum(m, s[:, t * QT:(t + 1) * QT])
            m_scr[g] = m

    chunk_loop(far_logits)

    near_p = {}

    def near_softmax(g):
        m = jnp.max(m_scr[g], axis=-1, keepdims=True)
        m_scr[g] = jnp.broadcast_to(m, (rq, QT))
        near_p[g] = jnp.exp2(sn_scr[g] - m).astype(BF16)

    def near_pv(g):
        v_n = jnp.concatenate([vs_ref[0, g, tile(i - 1)], vs_ref[0, g, i]], axis=0)
        acc_scr[g] = jnp.dot(near_p[g], v_n, preferred_element_type=F32)

    for step in range(N_KV_HEADS + 1):
        if step < N_KV_HEADS:
            near_softmax(step)
        if step > 0:
            near_pv(step - 1)

    def far_pv(c):
        for g in range(N_KV_HEADS):
            m = jnp.concatenate([m_scr[g]] * FAR_TILES, axis=1)
            p = jnp.exp2(s_scr[g, c] - m).astype(BF16)
            v_f = jnp.concatenate([vs_ref[0, g, FAR_TILES * c + t] for t in range(FAR_TILES)], axis=0)
            acc_scr[g] += jnp.dot(p, v_f, preferred_element_type=F32)

    chunk_loop(far_pv)

    for g in range(N_KV_HEADS):
        o_s = normalise(acc_scr[g])
        for pr in range(Q_PER_KV // 2):
            tl = Q_PER_KV * g // 2 + pr
            o = out_scr[:, tl * LANES:(tl + 1) * LANES] + gate_tile(1, tl) * merged(o_s, g, pr)
            o_ref[:, tl * LANES:(tl + 1) * LANES] = o.astype(BF16)


def _rel_bucket(dist):
    n = jnp.maximum(dist, 0)
    max_exact = NUM_BUCKETS // 2
    logv = jnp.log(jnp.maximum(n, 1).astype(F32) / max_exact) / math.log(MAX_DISTANCE / max_exact)
    large = jnp.minimum(max_exact + (logv * (NUM_BUCKETS - max_exact)).astype(jnp.int32), NUM_BUCKETS - 1)
    return jnp.where(n < max_exact, n, large)


def _bias_tables(rel_bias, seq):
    nt = seq // QT
    nc = seq // CMP_STRIDE
    cpt = QT // CMP_STRIDE
    rb = rel_bias * LOG2E

    def f(dist):
        onehot = (_rel_bucket(dist)[..., None] == jnp.arange(NUM_BUCKETS)).astype(F32)
        return jnp.einsum('...k,kh->h...', onehot, rb, precision=lax.Precision.HIGHEST)

    c_rel = jnp.arange(-cpt * (nt - 1), nc)
    dc = jnp.arange(QT)[:, None] - (c_rel * CMP_STRIDE + CMP_LEN - 1)[None, :]
    t0 = jnp.where((dc >= 0)[None], f(dc), NEG_INF)
    ctab = jnp.stack([t0[:, :, cpt * (nt - 1 - i):cpt * (nt - 1 - i) + nc] for i in range(nt)], axis=1)
    ctab = ctab.reshape(N_HEADS, seq, nc)
    far = rb[NUM_BUCKETS - 1].reshape(N_HEADS, 1, 1)
    dq = jnp.arange(QT)[:, None] - jnp.arange(QT)[None, :]
    d0, d3, d4 = WIN + dq, QT + dq, dq
    wtab = jnp.concatenate([jnp.where((d0 < WIN)[None], f(d0) - far, NEG_INF), f(d3) - far,
                            jnp.where((d4 >= 0)[None], f(d4) - far, NEG_INF)], axis=2)
    return ctab, wtab


def _overlap_matrix_t(seq):
    nc = seq // CMP_STRIDE
    n_sel = seq // SEL_LEN
    c_start = jnp.arange(nc) * CMP_STRIDE
    j_start = jnp.arange(n_sel) * SEL_LEN
    ov = jnp.clip(jnp.minimum(c_start[None, :] + CMP_LEN, j_start[:, None] + SEL_LEN)
                  - jnp.maximum(c_start[None, :], j_start[:, None]), 0, None)
    ov = ov.astype(F32) / CMP_LEN
    return jnp.pad(ov, ((SEL_LANE0, LANES - SEL_LANE0 - n_sel), (0, 0))).astype(BF16)


def _attention(q, gates, kv, rel_bias, bsz, seq):
    kc, vc, ks, vs, kw, vw = kv
    nt = seq // QT
    nc = seq // CMP_STRIDE
    n_sel = seq // SEL_LEN
    n_top = min(SEL_TOPN, n_sel)
    n_chunk = max(-(-(nt - 2) // FAR_TILES), 1)
    assert n_chunk * FAR_TILES <= nt
    nts = nt + PAD_TILES
    ctab, wtab = _bias_tables(rel_bias, seq)
    rq = Q_PER_KV * QT
    per_batch = lambda shape, bufs=1: pl.BlockSpec((1,) + shape, lambda b, i: (b,) + (0,) * len(shape),
                                                   pipeline_mode=pl.Buffered(bufs))
    steps = nt // ATTN_TILES
    tok = lambda c: pl.BlockSpec((ATTN_TILES * QT, c), lambda b, i: (b * steps + i, 0))
    return pl.pallas_call(
        functools.partial(_attn_body, n_sel=n_sel, n_top=n_top, nt=nt),
        grid=(bsz, steps),
        in_specs=[tok(N_HEADS * LANES), tok(LANES),
                  per_batch((N_KV_HEADS, LANES, nc)),
                  per_batch((N_KV_HEADS, nc, LANES)),
                  per_batch((N_KV_HEADS, nts, LANES, QT), 2),
                  per_batch((N_KV_HEADS, nts, QT, 2 * LANES)),
                  per_batch((N_KV_HEADS, nts, LANES, QT), 2),
                  per_batch((N_KV_HEADS, nts, QT, 2 * LANES)),
                  pl.BlockSpec((N_HEADS, ATTN_TILES * QT, nc), lambda b, i: (0, i, 0)),
                  _const_spec((N_HEADS, QT, 3 * QT)),
                  _const_spec((LANES, nc))],
        out_specs=tok(D_MODEL),
        out_shape=jax.ShapeDtypeStruct((bsz * seq, D_MODEL), BF16),
        scratch_shapes=[pltpu.VMEM((N_KV_HEADS, n_chunk, rq, FAR_TILES * QT), F32),
                        pltpu.VMEM((N_KV_HEADS, rq, 2 * QT), F32),
                        pltpu.VMEM((N_KV_HEADS, rq, QT), F32),
                        pltpu.VMEM((N_KV_HEADS, rq, LANES), BF16),
                        pltpu.VMEM((N_KV_HEADS, rq, 2 * LANES), F32),
                        pltpu.VMEM((QT, D_MODEL), F32)],
        compiler_params=_cparams(("parallel", "arbitrary"), ATTN_VMEM_LIMIT),
        name="nsa_attn",
    )(q, gates, kc, vc, ks, vs, kw, vw, ctab, wtab, _overlap_matrix_t(seq))


def kernel(x, rel_bias, ffn1_norm, ffn1_w_in, ffn1_w_out, mix_norm, ffn2_norm, ffn2_w_in, ffn2_w_out,
           s5_a_re, s5_a_im, s5_log_dt, s5_b_re, s5_b_im, s5_c_re, s5_c_im, s5_d, s5_w_glu,
           kv_norm, w_kv, k_norm_cmp, k_norm_slc, k_norm_win, cmp_pos_k, cmp_pos_v,
           cmp_k_w1, cmp_k_w2, cmp_v_w1, cmp_v_w2, w_qg, q_norm, w_o):
    bsz, seq, _ = x.shape
    depth = ffn1_norm.shape[0]
    n_a = s5_a_re.shape[0]
    h = x.reshape(bsz * seq, D_MODEL)
    kv = None
    ffn1_w_in, ffn1_w_out, ffn2_w_in, ffn2_w_out = (
        w.astype(BF16) for w in (ffn1_w_in, ffn1_w_out, ffn2_w_in, ffn2_w_out))
    for layer in range(depth):
        s5_layer = layer < n_a
        h = _ffn(h, ffn1_norm[layer], ffn1_w_in, ffn1_w_out, layer, bsz, seq,
                 in_tm=False, out_tm=s5_layer)
        pre = None
        if s5_layer:
            a = layer
            h = _s5(h, mix_norm[layer], s5_a_re[a], s5_a_im[a], s5_log_dt[a], s5_b_re[a], s5_b_im[a],
                    s5_c_re[a], s5_c_im[a], s5_d[a], s5_w_glu[a], bsz, seq)
        else:
            b = layer - n_a
            q, gates = _qproj(h, mix_norm[layer], w_qg[b], q_norm[b], rel_bias, bsz, seq)
            pre = (_attention(q, gates, kv, rel_bias, bsz, seq), w_o[b])
        h = _ffn(h, ffn2_norm[layer], ffn2_w_in, ffn2_w_out, layer, bsz, seq,
                 in_tm=s5_layer, out_tm=False, pre=pre)
        if layer == n_a - 1:
            kc_raw, vc_raw, ks, vs, kw, vw = _kv_proj(h, kv_norm, w_kv, k_norm_slc, k_norm_win, bsz, seq)
            kc, vc = _compress(kc_raw, vc_raw, k_norm_cmp, cmp_pos_k, cmp_pos_v,
                               cmp_k_w1, cmp_k_w2, cmp_v_w1, cmp_v_w2, bsz, seq)
            kv = (kc, vc, ks, vs, kw, vw)
    return h.reshape(bsz, seq, D_MODEL)
```

```python
import functools
import math

import jax
import jax.numpy as jnp
from jax import lax
from jax.experimental import pallas as pl
from jax.experimental.pallas import tpu as pltpu

F32 = jnp.float32
BF16 = jnp.bfloat16

D_MODEL = 1024
D_FF = 2816
RMS_EPS = 1e-6
S5_GROUP_CH = 16
S5_GROUPS = D_MODEL // S5_GROUP_CH
S5_STATE = 64
N_HEADS = 16
HEAD_DIM = 64
N_KV_HEADS = 4
Q_PER_KV = N_HEADS // N_KV_HEADS
CMP_LEN = 32
CMP_STRIDE = 16
CMP_HIDDEN = 2 * HEAD_DIM
SEL_LEN = 64
SEL_TOPN = 8
WIN = 512
N_BRANCH = 3
ATTN_SCALE = HEAD_DIM ** -0.5
NUM_BUCKETS = 32
MAX_DISTANCE = 128
NEG_INF = -1e30
BIG = 1e9

LANES = 128
MXU_DIM = 256
VMEM_LIMIT = 56 * 1024 * 1024
ATTN_VMEM_LIMIT = 61 * 1024 * 1024

FFN_ROWS = 1024
FFN_BATCH = 8
FF_CHUNK = MXU_DIM
S5_TC = 16
S5_CHUNKS = 2
S5_SLAB = MXU_DIM
S5_LANES = 256
QP_ROWS = 1024
QT = 128
ATTN_TILES = 2
FAR_TILES = 4
PAD_TILES = WIN // QT
LOG2E = math.log2(math.e)

SEL_LANE0 = HEAD_DIM
FARB_LANE0 = 96
PAD_LANE = 99


def _rms(x, n):
    ss = jnp.sum(x * x, axis=-1, keepdims=True)
    return x * lax.rsqrt(ss * (1.0 / n) + RMS_EPS)


def _cparams(sem, vmem_limit=VMEM_LIMIT):
    return pltpu.CompilerParams(dimension_semantics=sem, vmem_limit_bytes=vmem_limit)


def _const_spec(shape):
    nd = len(shape)
    return pl.BlockSpec(shape, lambda *_: (0,) * nd, pipeline_mode=pl.Buffered(1))


def _split3(x):
    x1 = x.astype(BF16)
    r1 = x - x1.astype(F32)
    x2 = r1.astype(BF16)
    x3 = (r1 - x2.astype(F32)).astype(BF16)
    return x1, x2, x3


def _ffn_body(*refs, pre, in_tm, out_tm):
    if pre:
        x_ref, po_ref, pw_ref, g_ref, win_ref, wout_ref, o_ref, act_ref = refs
    else:
        x_ref, g_ref, win_ref, wout_ref, o_ref, act_ref = refs
    rows = act_ref.shape[0]
    x = (jnp.swapaxes(x_ref[...], 0, 1) if in_tm else x_ref[...]).reshape(rows, D_MODEL)
    if pre:
        x = x + jnp.dot(po_ref[...].reshape(rows, D_MODEL), pw_ref[...], preferred_element_type=F32)
    xn = (_rms(x, D_MODEL) * g_ref[...]).astype(BF16)
    for c in range(D_FF // FF_CHUNK):
        lo = c * FF_CHUNK
        a = jnp.dot(xn, win_ref[:, lo:lo + FF_CHUNK], preferred_element_type=F32)
        b = jnp.dot(xn, win_ref[:, D_FF + lo:D_FF + lo + FF_CHUNK], preferred_element_type=F32)
        act_ref[:, lo:lo + FF_CHUNK] = (a * jax.nn.sigmoid(a) * b).astype(BF16)
    y = jnp.dot(act_ref[...], wout_ref[...], preferred_element_type=F32)
    res = x + 0.5 * y
    nb = o_ref.shape[1] if out_tm else o_ref.shape[0]
    res = res.reshape(nb, rows // nb, D_MODEL)
    o_ref[...] = jnp.swapaxes(res, 0, 1) if out_tm else res


def _token_spec(tb, tt, time_major):
    if time_major:
        return pl.BlockSpec((tt, tb, D_MODEL), lambda b, t: (t, b, 0))
    return pl.BlockSpec((tb, tt, D_MODEL), lambda b, t: (b, t, 0))


def _layer_spec(shape, layer):
    return pl.BlockSpec((None,) + shape, lambda *_: (layer, 0, 0), pipeline_mode=pl.Buffered(1))


def _ffn(h, gain, w_in, w_out, layer, bsz, seq, in_tm, out_tm, pre=None):
    tb = FFN_BATCH if (in_tm or out_tm) else 1
    tt = min(FFN_ROWS // tb, seq)
    rows = tb * tt
    as_bm = lambda a: a.reshape(bsz, seq, D_MODEL)
    out_shape = (seq, bsz, D_MODEL) if out_tm else (bsz, seq, D_MODEL)
    args, specs = [h if in_tm else as_bm(h)], [_token_spec(tb, tt, in_tm)]
    if pre is not None:
        args += [as_bm(pre[0]), pre[1].astype(BF16)]
        specs += [_token_spec(tb, tt, False), _const_spec((D_MODEL, D_MODEL))]
    args += [gain.reshape(1, D_MODEL), w_in, w_out]
    specs += [_const_spec((1, D_MODEL)), _layer_spec((D_MODEL, 2 * D_FF), layer),
              _layer_spec((D_FF, D_MODEL), layer)]
    out = pl.pallas_call(
        functools.partial(_ffn_body, pre=pre is not None, in_tm=in_tm, out_tm=out_tm),
        grid=(bsz // tb, seq // tt),
        in_specs=specs,
        out_specs=_token_spec(tb, tt, out_tm),
        out_shape=jax.ShapeDtypeStruct(out_shape, F32),
        scratch_shapes=[pltpu.VMEM((rows, D_FF), BF16)],
        compiler_params=_cparams(("parallel", "parallel")),
        name="ffn",
    )(*args)
    return out if out_tm else out.reshape(bsz * seq, D_MODEL)


def _s5_body(h_ref, g_ref, bmat_ref, cmat_ref, are_ref, aim_ref, d_ref, wglu_ref,
             o_ref, bu_ref, xb_ref, st_ref, y_ref, *, tc, nb):
    @pl.when(pl.program_id(0) == 0)
    def _():
        st_ref[...] = jnp.zeros_like(st_ref)

    def one_chunk(j, carry):
        steps = pl.ds(pl.multiple_of(j * tc, tc), tc)
        _s5_chunk(h_ref.at[steps], g_ref, bmat_ref, cmat_ref, are_ref, aim_ref, d_ref, wglu_ref,
                  o_ref.at[steps], bu_ref, xb_ref, st_ref, y_ref, tc=tc, nb=nb)
        return carry

    lax.fori_loop(0, h_ref.shape[0] // tc, one_chunk, 0)


def _s5_chunk(h_ref, g_ref, bmat_ref, cmat_ref, are_ref, aim_ref, d_ref, wglu_ref,
              o_ref, bu_ref, xb_ref, st_ref, y_ref, *, tc, nb):
    n_slab = D_MODEL // S5_SLAB
    half = S5_SLAB // S5_GROUP_CH * S5_STATE
    rows = tc * nb
    h = h_ref[...].reshape(rows, D_MODEL)
    u = _rms(h, D_MODEL) * g_ref[...]
    ub = u.astype(BF16)

    def project(sl):
        bu_ref[sl % 2] = jnp.dot(ub[:, sl * S5_SLAB:(sl + 1) * S5_SLAB], bmat_ref[sl],
                                 preferred_element_type=F32)

    project(0)
    for sl in range(n_slab):
        if sl + 1 < n_slab:
            project(sl + 1)
        bu, xb = bu_ref.at[sl % 2], xb_ref.at[sl % 2]
        for wb in range(half // S5_LANES):
            re = slice(wb * S5_LANES, (wb + 1) * S5_LANES)
            im = slice(half + wb * S5_LANES, half + (wb + 1) * S5_LANES)
            ar = jnp.broadcast_to(are_ref[sl, :, re], (nb, S5_LANES))
            ai = jnp.broadcast_to(aim_ref[sl, :, re], (nb, S5_LANES))
            xr, xi = st_ref[sl, :, re], st_ref[sl, :, im]
            for t in range(tc):
                rs = slice(t * nb, (t + 1) * nb)
                xr, xi = ar * xr - ai * xi + bu[rs, re], ar * xi + ai * xr + bu[rs, im]
                xb[rs, re] = xr.astype(BF16)
                xb[rs, im] = xi.astype(BF16)
            st_ref[sl, :, re] = xr
            st_ref[sl, :, im] = xi
        y_ref[:, sl * S5_SLAB:(sl + 1) * S5_SLAB] = jnp.dot(
            xb_ref[sl % 2], cmat_ref[sl], preferred_element_type=F32)
    y = jax.nn.gelu(y_ref[...] + d_ref[...] * u)
    gate = jnp.dot(y.astype(BF16), wglu_ref[...], preferred_element_type=F32)
    o_ref[...] = (h + y * jax.nn.sigmoid(gate)).reshape(tc, nb, D_MODEL)


def _s5_params(a_re, a_im, log_dt, b_re, b_im, c_re, c_im):
    dt = jnp.exp(log_dt)[:, None]
    mag = jnp.exp(a_re * dt)
    ab_re = mag * jnp.cos(a_im * dt)
    ab_im = mag * jnp.sin(a_im * dt)
    den = a_re * a_re + a_im * a_im
    z_re = ((ab_re - 1.0) * a_re + ab_im * a_im) / den
    z_im = (ab_im * a_re - (ab_re - 1.0) * a_im) / den
    bb_re = z_re[..., None] * b_re - z_im[..., None] * b_im
    bb_im = z_re[..., None] * b_im + z_im[..., None] * b_re
    n_slab = D_MODEL // S5_SLAB
    gps = S5_GROUPS // n_slab
    eye = jnp.eye(gps, dtype=F32)

    def in_mat(bb):
        bb = bb.reshape(n_slab, gps, S5_STATE, S5_GROUP_CH)
        m = jnp.einsum('sgph,gk->sghkp', bb, eye)
        return m.reshape(n_slab, gps * S5_GROUP_CH, gps * S5_STATE)

    def out_mat(cc):
        cc = cc.reshape(n_slab, gps, S5_GROUP_CH, S5_STATE)
        m = jnp.einsum('sghp,gk->sgpkh', cc, eye)
        return m.reshape(n_slab, gps * S5_STATE, gps * S5_GROUP_CH)

    bmat = jnp.concatenate([in_mat(bb_re), in_mat(bb_im)], axis=2).astype(BF16)
    cmat = jnp.concatenate([out_mat(c_re), out_mat(-c_im)], axis=1).astype(BF16)
    are = ab_re.reshape(n_slab, 1, gps * S5_STATE)
    aim = ab_im.reshape(n_slab, 1, gps * S5_STATE)
    return bmat, cmat, are, aim


def _s5(h_tm, gain, a_re, a_im, log_dt, b_re, b_im, c_re, c_im, d_skip, w_glu, bsz, seq):
    tc = min(S5_TC, seq)
    n_slab = D_MODEL // S5_SLAB
    half = S5_SLAB // S5_GROUP_CH * S5_STATE
    bmat, cmat, are, aim = _s5_params(a_re, a_im, log_dt, b_re, b_im, c_re, c_im)
    step_t = S5_CHUNKS * tc if seq % (S5_CHUNKS * tc) == 0 else tc
    blk = pl.BlockSpec((step_t, bsz, D_MODEL), lambda i: (i, 0, 0))
    out = pl.pallas_call(
        functools.partial(_s5_body, tc=tc, nb=bsz),
        grid=(seq // step_t,),
        in_specs=[blk,
                  _const_spec((1, D_MODEL)),
                  _const_spec((n_slab, S5_SLAB, 2 * half)),
                  _const_spec((n_slab, 2 * half, S5_SLAB)),
                  _const_spec((n_slab, 1, half)),
                  _const_spec((n_slab, 1, half)),
                  _const_spec((1, D_MODEL)),
                  _const_spec((D_MODEL, D_MODEL))],
        out_specs=blk,
        out_shape=jax.ShapeDtypeStruct((seq, bsz, D_MODEL), F32),
        scratch_shapes=[pltpu.VMEM((2, tc * bsz, 2 * half), F32),
                        pltpu.VMEM((2, tc * bsz, 2 * half), BF16),
                        pltpu.VMEM((n_slab, bsz, 2 * half), F32),
                        pltpu.VMEM((tc * bsz, D_MODEL), F32)],
        compiler_params=_cparams(("arbitrary",)),
        name="s5",
    )(h_tm, gain.reshape(1, D_MODEL), bmat, cmat, are, aim,
      d_skip.reshape(1, D_MODEL), w_glu.astype(BF16))
    return out


def _kv_body(h_ref, g_ref, w_ref, gk_ref, kc_ref, vc_ref, ks_ref, vs_ref, kw_ref, vw_ref, cs_ref,
             *, n_real):
    s = pl.program_id(1)
    gw = N_KV_HEADS * HEAD_DIM
    pw = N_KV_HEADS * LANES
    tiles = ks_ref.shape[2]
    row = lax.broadcasted_iota(jnp.int32, (LANES, QT), 0)
    key_hi = (lax.broadcasted_iota(jnp.int32, (LANES, QT), 1) >= SEL_LEN).astype(jnp.int32)
    farb_rows = ((row >= FARB_LANE0) & (row < FARB_LANE0 + 3)).astype(F32)
    ones = jnp.ones((QT, LANES), BF16)

    @pl.when(s < n_real)
    def _():
        hn = (_rms(h_ref[...], D_MODEL) * g_ref[...]).astype(BF16)
        proj = lambda lo, n: jnp.dot(hn, w_ref[:, lo:lo + n], preferred_element_type=F32)
        kvc = proj(0, 2 * gw)
        kv_all = [(proj(2 * gw + br * 2 * pw, pw), proj(2 * gw + br * 2 * pw + pw, pw)) for br in range(2)]
        n_hb = kvc.shape[0] // CMP_STRIDE
        for j in range(2 * gw // LANES):
            cs_ref[j] = kvc[:, j * LANES:(j + 1) * LANES]
        for c_ref, j0 in ((kc_ref, 0), (vc_ref, gw // LANES)):
            for l in range(CMP_STRIDE):
                for j in range(gw // LANES):
                    col = l * gw + j * LANES
                    c_ref[:, col:col + LANES] = cs_ref[j0 + j, pl.ds(l, n_hb, stride=CMP_STRIDE), :].astype(BF16)
        for br, (k_ref, v_ref) in enumerate(((ks_ref, vs_ref), (kw_ref, vw_ref))):
            k_all, v_all = kv_all[br]
            for g in range(N_KV_HEADS):
                k = _rms(k_all[:, g * LANES:(g + 1) * LANES], HEAD_DIM) * gk_ref[br]
                v = v_all[:, g * LANES:(g + 1) * LANES].astype(BF16)
                for t in range(tiles):
                    extra = farb_rows
                    if br == 0:
                        blk = 2 * (tiles * s + t) + key_hi
                        extra = extra + (row - SEL_LANE0 == blk).astype(F32)
                    k_ref[0, g, t] = (k[t * QT:(t + 1) * QT].T + extra).astype(BF16)
                    v_ref[0, g, t] = jnp.concatenate([v[t * QT:(t + 1) * QT], ones], axis=1)

    @pl.when(s == n_real)
    def _():
        pad_k = jnp.broadcast_to((row == PAD_LANE).astype(BF16), (N_KV_HEADS, tiles, LANES, QT))
        pad_v = jnp.zeros((N_KV_HEADS, tiles, QT, 2 * LANES), BF16)
        for k_ref, v_ref in ((ks_ref, vs_ref), (kw_ref, vw_ref)):
            k_ref[0] = pad_k
            v_ref[0] = pad_v


def _pad_heads(w, n_heads):
    k = w.shape[0]
    w = w.reshape(k, n_heads, HEAD_DIM)
    return jnp.pad(w, ((0, 0), (0, 0), (0, LANES - HEAD_DIM))).reshape(k, n_heads * LANES)


def _kv_proj(h, kv_norm, w_kv, k_norm_slc, k_norm_win, bsz, seq):
    gw = N_KV_HEADS * HEAD_DIM
    tiles = PAD_TILES
    rows = tiles * QT
    n_real = seq // rows
    nt = seq // QT
    wk = w_kv.reshape(D_MODEL, 2 * N_BRANCH, gw)
    w = jnp.concatenate([wk[:, 0], wk[:, 1]] + [_pad_heads(wk[:, j], N_KV_HEADS) for j in (2, 3, 4, 5)],
                        axis=1).astype(BF16)
    gk = jnp.pad(jnp.stack([k_norm_slc, k_norm_win]), ((0, 0), (0, LANES - HEAD_DIM)))
    gk = gk.reshape(2, 1, LANES)
    n = bsz * seq
    real = lambda s: jnp.minimum(s, n_real - 1)
    tok = lambda c: pl.BlockSpec((rows, c), lambda b, s: (b * n_real + real(s), 0))
    tile_spec = lambda r, c: pl.BlockSpec((1, N_KV_HEADS, tiles, r, c), lambda b, s: (b, 0, s, 0, 0))
    hb_spec = pl.BlockSpec((rows // CMP_STRIDE, CMP_STRIDE * gw), lambda b, s: (b * n_real + real(s), 0))
    dense = jax.ShapeDtypeStruct((n // CMP_STRIDE, CMP_STRIDE * gw), BF16)
    k_tiles = jax.ShapeDtypeStruct((bsz, N_KV_HEADS, nt + tiles, LANES, QT), BF16)
    v_tiles = jax.ShapeDtypeStruct((bsz, N_KV_HEADS, nt + tiles, QT, 2 * LANES), BF16)
    return pl.pallas_call(
        functools.partial(_kv_body, n_real=n_real),
        grid=(bsz, n_real + 1),
        in_specs=[tok(D_MODEL), _const_spec((1, D_MODEL)), _const_spec(w.shape),
                  _const_spec((2, 1, LANES))],
        out_specs=[hb_spec, hb_spec, tile_spec(LANES, QT), tile_spec(QT, 2 * LANES),
                   tile_spec(LANES, QT), tile_spec(QT, 2 * LANES)],
        out_shape=[dense, dense, k_tiles, v_tiles, k_tiles, v_tiles],
        scratch_shapes=[pltpu.VMEM((2 * gw // LANES, rows, LANES), F32)],
        compiler_params=_cparams(("parallel", "arbitrary")),
        name="kv_proj",
    )(h, kv_norm.reshape(1, D_MODEL), w, gk)


def _cmp_body(xk_ref, xv_ref, w1k_ref, w1v_ref, bk_ref, bv_ref, w2k_ref, w2v_ref, gk_ref,
              kc_ref, vc_ref, *, nh):
    ab_k = jnp.dot(xk_ref[0], w1k_ref[...], preferred_element_type=F32)
    ab_v = jnp.dot(xv_ref[0], w1v_ref[...], preferred_element_type=F32)
    for g in range(N_KV_HEADS):
        outs = []
        for ab_all, b_ref, w2_ref in ((ab_k, bk_ref, w2k_ref), (ab_v, bv_ref, w2v_ref)):
            ab = ab_all[:, g * 2 * CMP_HIDDEN:(g + 1) * 2 * CMP_HIDDEN]
            hid = ab[:, :CMP_HIDDEN] + pltpu.roll(ab[:, CMP_HIDDEN:], nh - 1, 0) + b_ref[...]
            hid = jax.nn.gelu(hid).astype(BF16)
            outs.append(jnp.dot(hid, w2_ref[...], preferred_element_type=F32))
        k = _rms(outs[0], HEAD_DIM) * gk_ref[...]
        kc_ref[0, g] = k.T.astype(BF16)
        vc_ref[0, g] = outs[1].astype(BF16)


def _compress(kc_raw, vc_raw, k_norm_cmp, pos_k, pos_v, k_w1, k_w2, v_w1, v_w2, bsz, seq):
    nh = seq // CMP_STRIDE
    hb = CMP_STRIDE * N_KV_HEADS * HEAD_DIM
    eye = jnp.eye(N_KV_HEADS, dtype=F32)

    def halfblocks(x):
        return x.reshape(bsz, nh, hb)

    def w1cat(w1):
        w = w1.reshape(2, CMP_STRIDE, HEAD_DIM, CMP_HIDDEN)
        w = jnp.concatenate([w[0], w[1]], axis=-1)
        w = jnp.einsum('ldj,gh->lgdhj', w, eye)
        return w.reshape(hb, N_KV_HEADS * 2 * CMP_HIDDEN).astype(BF16)

    def w2pad(w2):
        return jnp.pad(w2, ((0, 0), (0, LANES - HEAD_DIM))).astype(BF16)

    bias = lambda pos, w1: jnp.einsum('ld,ldh->h', pos, w1,
                                      precision=lax.Precision.HIGHEST).reshape(1, CMP_HIDDEN)
    gk = jnp.pad(k_norm_cmp, (0, LANES - HEAD_DIM)).reshape(1, LANES)
    xspec = pl.BlockSpec((1, nh, hb), lambda b: (b, 0, 0))
    ospec = pl.BlockSpec((1, N_KV_HEADS, LANES, nh), lambda b: (b, 0, 0, 0))
    vspec = pl.BlockSpec((1, N_KV_HEADS, nh, LANES), lambda b: (b, 0, 0, 0))
    w1_shape = (hb, N_KV_HEADS * 2 * CMP_HIDDEN)
    return pl.pallas_call(
        functools.partial(_cmp_body, nh=nh),
        grid=(bsz,),
        in_specs=[xspec, xspec, _const_spec(w1_shape), _const_spec(w1_shape),
                  _const_spec((1, CMP_HIDDEN)), _const_spec((1, CMP_HIDDEN)),
                  _const_spec((CMP_HIDDEN, LANES)), _const_spec((CMP_HIDDEN, LANES)),
                  _const_spec((1, LANES))],
        out_specs=[ospec, vspec],
        out_shape=[jax.ShapeDtypeStruct((bsz, N_KV_HEADS, LANES, nh), BF16),
                   jax.ShapeDtypeStruct((bsz, N_KV_HEADS, nh, LANES), BF16)],
        compiler_params=_cparams(("parallel",)),
        name="kv_compress",
    )(halfblocks(kc_raw), halfblocks(vc_raw), w1cat(k_w1), w1cat(v_w1),
      bias(pos_k, k_w1), bias(pos_v, v_w1), w2pad(k_w2), w2pad(v_w2), gk)


def _qproj_body(h_ref, g_ref, w_ref, qsc_ref, qc_ref, q_ref, gate_ref):
    u = (_rms(h_ref[...], D_MODEL) * g_ref[...]).astype(BF16)
    qg = jnp.dot(u, w_ref[...], preferred_element_type=F32)
    for hh in range(N_HEADS):
        sl = slice(hh * LANES, (hh + 1) * LANES)
        q_ref[:, sl] = (_rms(qg[:, sl], HEAD_DIM) * qsc_ref[:, sl] + qc_ref[:, sl]).astype(BF16)
    gate_ref[...] = jax.nn.sigmoid(qg[:, N_HEADS * LANES:])


def _qproj(h, mix_gain, w_qg, q_norm, rel_bias, bsz, seq):
    rows = min(QP_ROWS, seq)
    n = bsz * seq
    nq = N_HEADS * HEAD_DIM
    wq = jnp.concatenate([_pad_heads(w_qg[:, :nq], N_HEADS),
                          jnp.pad(w_qg[:, nq:], ((0, 0), (0, LANES - N_BRANCH * N_HEADS)))],
                         axis=1).astype(BF16)
    qsc = jnp.tile(jnp.pad(q_norm * (ATTN_SCALE * LOG2E), (0, LANES - HEAD_DIM)), N_HEADS)
    qsc = qsc.reshape(1, N_HEADS * LANES)
    far = _split3(rel_bias[NUM_BUCKETS - 1] * LOG2E)
    qc = jnp.zeros((N_HEADS, LANES), F32)
    for t, term in enumerate(far):
        qc = qc.at[:, FARB_LANE0 + t].set(term.astype(F32))
    qc = qc.at[:, PAD_LANE].set(NEG_INF).reshape(1, N_HEADS * LANES)
    tok = lambda c: pl.BlockSpec((rows, c), lambda i: (i, 0))
    return pl.pallas_call(
        _qproj_body,
        grid=(n // rows,),
        in_specs=[tok(D_MODEL), _const_spec((1, D_MODEL)), _const_spec(wq.shape),
                  _const_spec((1, N_HEADS * LANES)), _const_spec((1, N_HEADS * LANES))],
        out_specs=[tok(N_HEADS * LANES), tok(LANES)],
        out_shape=[jax.ShapeDtypeStruct((n, N_HEADS * LANES), BF16),
                   jax.ShapeDtypeStruct((n, LANES), F32)],
        compiler_params=_cparams(("parallel",)),
        name="q_proj",
    )(h, mix_gain.reshape(1, D_MODEL), wq, qsc, qc)


def _attn_body(q_ref, gate_ref, kc_ref, vc_ref, ks_ref, vs_ref, kw_ref, vw_ref, ctab_ref, wtab_ref, ovlt_ref,
               o_ref, *scratch, **static):
    def one_tile(j, carry):
        rows = pl.ds(pl.multiple_of(j * QT, QT), QT)
        _attn_tile(pl.program_id(1) * ATTN_TILES + j, q_ref.at[rows], gate_ref.at[rows],
                   kc_ref, vc_ref, ks_ref, vs_ref, kw_ref, vw_ref, ctab_ref.at[:, rows], wtab_ref, ovlt_ref,
                   o_ref.at[rows], *scratch, **static)
        return carry

    lax.fori_loop(0, ATTN_TILES, one_tile, 0)


def _attn_tile(i, q_ref, gate_ref, kc_ref, vc_ref, ks_ref, vs_ref, kw_ref, vw_ref,
               ctab_ref, wtab_ref, ovlt_ref,
               o_ref, s_scr, sn_scr, m_scr, qf_scr, acc_scr, out_scr, *, n_sel, n_top, nt):
    rq = Q_PER_KV * QT
    n_win = WIN // QT + 1
    lane = lax.broadcasted_iota(jnp.int32, (QT, LANES), 1)
    sel_lane = (lane >= SEL_LANE0) & (lane < SEL_LANE0 + n_sel)
    far_cut = sel_lane & (lane - SEL_LANE0 >= 2 * (i - 1))
    sel_lane4 = jnp.concatenate([sel_lane] * Q_PER_KV, axis=0)
    blk = lax.broadcasted_iota(jnp.int32, (n_sel, QT), 0)
    blkf = blk.astype(F32)
    qpos = i * QT + lax.broadcasted_iota(jnp.int32, (n_sel, QT), 1)
    cur = lax.shift_right_arithmetic(qpos, int(math.log2(SEL_LEN)))
    forced = (blk == 0) | (blk == cur) | (blk == cur - 1)
    causal = blk * SEL_LEN <= qpos
    tile = lambda t: jnp.where(t < 0, nt, t)

    low = lane < HEAD_DIM

    def gate_tile(b, tl):
        ca, cb = 2 * tl * N_BRANCH + b, (2 * tl + 1) * N_BRANCH + b
        bc = lambda c: jnp.broadcast_to(gate_ref[:, c:c + 1], (QT, LANES))
        return jnp.where(low, bc(ca), bc(cb))

    def merged(o, g, pr):
        return o[2 * pr * QT:(2 * pr + 1) * QT] + pltpu.roll(o[(2 * pr + 1) * QT:(2 * pr + 2) * QT], HEAD_DIM, 1)

    def normalise(pv):
        return pv[:, :LANES] * (1.0 / pv[:, LANES:])

    def tab(g, lo, width):
        return wtab_ref[Q_PER_KV * g:Q_PER_KV * (g + 1), :, lo:lo + width].reshape(rq, width)

    def queries(g):
        return jnp.concatenate([q_ref[:, (Q_PER_KV * g + r) * LANES:(Q_PER_KV * g + r + 1) * LANES]
                                for r in range(Q_PER_KV)], axis=0)

    cmp_p, pv_cs, picked, win = {}, {}, {}, {}
    w_tiles = [tile(i - (n_win - 1) + j) for j in range(n_win)]

    def cmp_softmax(g):
        sc = jnp.dot(queries(g), kc_ref[0, g], preferred_element_type=F32)
        sc = sc + ctab_ref[Q_PER_KV * g:Q_PER_KV * (g + 1)].reshape(rq, sc.shape[-1])
        e = jnp.exp2(sc - jnp.max(sc, axis=-1, keepdims=True))
        row_ok = i * QT + lax.broadcasted_iota(jnp.int32, (QT, sc.shape[-1]), 0) >= CMP_LEN - 1
        cmp_p[g] = jnp.where(jnp.concatenate([row_ok] * Q_PER_KV, axis=0),
                             e * (1.0 / jnp.sum(e, axis=-1, keepdims=True)), 0.0)

    def cmp_out(g):
        p = cmp_p[g]
        pv_cs[g] = jnp.dot(p.astype(BF16), vc_ref[0, g], preferred_element_type=F32)
        psum = p[0:QT] + p[QT:2 * QT] + p[2 * QT:3 * QT] + p[3 * QT:4 * QT]
        ps_t = sum(jnp.dot(ovlt_ref[...], t, preferred_element_type=F32) for t in _split3(psum.T))
        cmp_p[g] = jnp.where(forced, BIG, jnp.where(causal, ps_t[SEL_LANE0:SEL_LANE0 + n_sel], -BIG))

    def select(g):
        score = cmp_p[g]
        pick = jnp.zeros((n_sel, QT), F32)
        for _ in range(n_top):
            mx = jnp.max(score, axis=0, keepdims=True)
            first = jnp.min(jnp.where(score == mx, blkf, float(LANES)), axis=0, keepdims=True)
            hit = blkf == first
            pick = jnp.where(hit, 1.0, pick)
            score = jnp.where(hit, -jnp.inf, score)
        picked[g] = pick

    def win_logits(g):
        k_w = jnp.concatenate([kw_ref[0, g, t] for t in w_tiles], axis=1)
        s_w = jnp.dot(queries(g), k_w, preferred_element_type=F32)
        win[g] = [s_w[:, :QT] + tab(g, 0, QT), s_w[:, QT:(n_win - 2) * QT],
                  s_w[:, (n_win - 2) * QT:] + tab(g, QT, 2 * QT)]

    def win_softmax(g):
        mw = jnp.max(jnp.concatenate(win[g], axis=1), axis=-1, keepdims=True)
        win[g] = jnp.concatenate([jnp.exp2(t - mw).astype(BF16) for t in win[g]], axis=1)

    def win_pv(g):
        v_w = jnp.concatenate([vw_ref[0, g, t] for t in w_tiles], axis=0)
        win[g] = normalise(jnp.dot(win[g], v_w, preferred_element_type=F32))

    def win_out(g):
        o_w = win[g]
        for pr in range(Q_PER_KV // 2):
            tl = Q_PER_KV * g // 2 + pr
            out_scr[:, tl * LANES:(tl + 1) * LANES] = (gate_tile(0, tl) * merged(pv_cs[g], g, pr)
                                                       + gate_tile(2, tl) * merged(o_w, g, pr))

    def near_logits(g):
        q = queries(g)
        selb_t = jnp.where(picked[g] == 0.0, NEG_INF, 0.0)
        selb = jnp.concatenate([jnp.zeros((SEL_LANE0, QT), F32), selb_t,
                                jnp.zeros((LANES - SEL_LANE0 - n_sel, QT), F32)], axis=0).T
        selb_far = jnp.where(far_cut, NEG_INF, selb)
        q_near = jnp.where(sel_lane4, jnp.concatenate([selb.astype(BF16)] * Q_PER_KV, axis=0), q)
        qf_scr[g] = jnp.where(sel_lane4, jnp.concatenate([selb_far.astype(BF16)] * Q_PER_KV, axis=0), q)
        k_n = jnp.concatenate([ks_ref[0, g, tile(i - 1)], ks_ref[0, g, i]], axis=1)
        s_n = jnp.dot(q_near, k_n, preferred_element_type=F32) + tab(g, QT, 2 * QT)
        sn_scr[g] = s_n
        m_scr[g] = jnp.maximum(s_n[:, :QT], s_n[:, QT:])

    stages = (cmp_softmax, win_logits, cmp_out, win_softmax, select, win_pv, near_logits, win_out)
    for step in range(N_KV_HEADS + len(stages) - 1):
        for k, stage in enumerate(stages):
            if 0 <= step - k < N_KV_HEADS:
                stage(step - k)

    n_chunks = (jnp.maximum(i - 1, 0) + FAR_TILES - 1) // FAR_TILES

    def chunk_loop(body):
        def pair(p, carry):
            body(2 * p)
            body(2 * p + 1)
            return carry

        lax.fori_loop(0, n_chunks // 2, pair, 0)

        @pl.when(n_chunks % 2 == 1)
        def _():
            body(n_chunks - 1)

    def far_logits(c):
        for g in range(N_KV_HEADS):
            k_f = jnp.concatenate([ks_ref[0, g, FAR_TILES * c + t] for t in range(FAR_TILES)], axis=1)
            s = jnp.dot(qf_scr[g], k_f, preferred_element_type=F32)
            s_scr[g, c] = s
            m = m_scr[g]
            for t in range(FAR_TILES):
                m = jnp.maximum(m, s[:, t * QT:(t + 1) * QT])
            m_scr[g] = m

    chunk_loop(far_logits)

    near_p = {}

    def near_softmax(g):
        m = jnp.max(m_scr[g], axis=-1, keepdims=True)
        m_scr[g] = jnp.broadcast_to(m, (rq, QT))
        near_p[g] = jnp.exp2(sn_scr[g] - m).astype(BF16)

    def near_pv(g):
        v_n = jnp.concatenate([vs_ref[0, g, tile(i - 1)], vs_ref[0, g, i]], axis=0)
        acc_scr[g] = jnp.dot(near_p[g], v_n, preferred_element_type=F32)

    for step in range(N_KV_HEADS + 1):
        if step < N_KV_HEADS:
            near_softmax(step)
        if step > 0:
            near_pv(step - 1)

    def far_pv(c):
        for g in range(N_KV_HEADS):
            m = jnp.concatenate([m_scr[g]] * FAR_TILES, axis=1)
            p = jnp.exp2(s_scr[g, c] - m).astype(BF16)
            v_f = jnp.concatenate([vs_ref[0, g, FAR_TILES * c + t] for t in range(FAR_TILES)], axis=0)
            acc_scr[g] += jnp.dot(p, v_f, preferred_element_type=F32)

    chunk_loop(far_pv)

    for g in range(N_KV_HEADS):
        o_s = normalise(acc_scr[g])
        for pr in range(Q_PER_KV // 2):
            tl = Q_PER_KV * g // 2 + pr
            o = out_scr[:, tl * LANES:(tl + 1) * LANES] + gate_tile(1, tl) * merged(o_s, g, pr)
            o_ref[:, tl * LANES:(tl + 1) * LANES] = o.astype(BF16)


def _rel_bucket(dist):
    n = jnp.maximum(dist, 0)
    max_exact = NUM_BUCKETS // 2
    logv = jnp.log(jnp.maximum(n, 1).astype(F32) / max_exact) / math.log(MAX_DISTANCE / max_exact)
    large = jnp.minimum(max_exact + (logv * (NUM_BUCKETS - max_exact)).astype(jnp.int32), NUM_BUCKETS - 1)
    return jnp.where(n < max_exact, n, large)


def _bias_tables(rel_bias, seq):
    nt = seq // QT
    nc = seq // CMP_STRIDE
    cpt = QT // CMP_STRIDE
    rb = rel_bias * LOG2E

    def f(dist):
        onehot = (_rel_bucket(dist)[..., None] == jnp.arange(NUM_BUCKETS)).astype(F32)
        return jnp.einsum('...k,kh->h...', onehot, rb, precision=lax.Precision.HIGHEST)

    c_rel = jnp.arange(-cpt * (nt - 1), nc)
    dc = jnp.arange(QT)[:, None] - (c_rel * CMP_STRIDE + CMP_LEN - 1)[None, :]
    t0 = jnp.where((dc >= 0)[None], f(dc), NEG_INF)
    ctab = jnp.stack([t0[:, :, cpt * (nt - 1 - i):cpt * (nt - 1 - i) + nc] for i in range(nt)], axis=1)
    ctab = ctab.reshape(N_HEADS, seq, nc)
    far = rb[NUM_BUCKETS - 1].reshape(N_HEADS, 1, 1)
    dq = jnp.arange(QT)[:, None] - jnp.arange(QT)[None, :]
    d0, d3, d4 = WIN + dq, QT + dq, dq
    wtab = jnp.concatenate([jnp.where((d0 < WIN)[None], f(d0) - far, NEG_INF), f(d3) - far,
                            jnp.where((d4 >= 0)[None], f(d4) - far, NEG_INF)], axis=2)
    return ctab, wtab


def _overlap_matrix_t(seq):
    nc = seq // CMP_STRIDE
    n_sel = seq // SEL_LEN
    c_start = jnp.arange(nc) * CMP_STRIDE
    j_start = jnp.arange(n_sel) * SEL_LEN
    ov = jnp.clip(jnp.minimum(c_start[None, :] + CMP_LEN, j_start[:, None] + SEL_LEN)
                  - jnp.maximum(c_start[None, :], j_start[:, None]), 0, None)
    ov = ov.astype(F32) / CMP_LEN
    return jnp.pad(ov, ((SEL_LANE0, LANES - SEL_LANE0 - n_sel), (0, 0))).astype(BF16)


def _attention(q, gates, kv, rel_bias, bsz, seq):
    kc, vc, ks, vs, kw, vw = kv
    nt = seq // QT
    nc = seq // CMP_STRIDE
    n_sel = seq // SEL_LEN
    n_top = min(SEL_TOPN, n_sel)
    n_chunk = max(-(-(nt - 2) // FAR_TILES), 1)
    assert n_chunk * FAR_TILES <= nt
    nts = nt + PAD_TILES
    ctab, wtab = _bias_tables(rel_bias, seq)
    rq = Q_PER_KV * QT
    per_batch = lambda shape, bufs=1: pl.BlockSpec((1,) + shape, lambda b, i: (b,) + (0,) * len(shape),
                                                   pipeline_mode=pl.Buffered(bufs))
    steps = nt // ATTN_TILES
    tok = lambda c: pl.BlockSpec((ATTN_TILES * QT, c), lambda b, i: (b * steps + i, 0))
    return pl.pallas_call(
        functools.partial(_attn_body, n_sel=n_sel, n_top=n_top, nt=nt),
        grid=(bsz, steps),
        in_specs=[tok(N_HEADS * LANES), tok(LANES),
                  per_batch((N_KV_HEADS, LANES, nc)),
                  per_batch((N_KV_HEADS, nc, LANES)),
                  per_batch((N_KV_HEADS, nts, LANES, QT), 2),
                  per_batch((N_KV_HEADS, nts, QT, 2 * LANES)),
                  per_batch((N_KV_HEADS, nts, LANES, QT), 2),
                  per_batch((N_KV_HEADS, nts, QT, 2 * LANES)),
                  pl.BlockSpec((N_HEADS, ATTN_TILES * QT, nc), lambda b, i: (0, i, 0)),
                  _const_spec((N_HEADS, QT, 3 * QT)),
                  _const_spec((LANES, nc))],
        out_specs=tok(D_MODEL),
        out_shape=jax.ShapeDtypeStruct((bsz * seq, D_MODEL), BF16),
        scratch_shapes=[pltpu.VMEM((N_KV_HEADS, n_chunk, rq, FAR_TILES * QT), F32),
                        pltpu.VMEM((N_KV_HEADS, rq, 2 * QT), F32),
                        pltpu.VMEM((N_KV_HEADS, rq, QT), F32),
                        pltpu.VMEM((N_KV_HEADS, rq, LANES), BF16),
                        pltpu.VMEM((N_KV_HEADS, rq, 2 * LANES), F32),
                        pltpu.VMEM((QT, D_MODEL), F32)],
        compiler_params=_cparams(("parallel", "arbitrary"), ATTN_VMEM_LIMIT),
        name="nsa_attn",
    )(q, gates, kc, vc, ks, vs, kw, vw, ctab, wtab, _overlap_matrix_t(seq))


def kernel(x, rel_bias, ffn1_norm, ffn1_w_in, ffn1_w_out, mix_norm, ffn2_norm, ffn2_w_in, ffn2_w_out,
           s5_a_re, s5_a_im, s5_log_dt, s5_b_re, s5_b_im, s5_c_re, s5_c_im, s5_d, s5_w_glu,
           kv_norm, w_kv, k_norm_cmp, k_norm_slc, k_norm_win, cmp_pos_k, cmp_pos_v,
           cmp_k_w1, cmp_k_w2, cmp_v_w1, cmp_v_w2, w_qg, q_norm, w_o):
    bsz, seq, _ = x.shape
    depth = ffn1_norm.shape[0]
    n_a = s5_a_re.shape[0]
    h = x.reshape(bsz * seq, D_MODEL)
    kv = None
    ffn1_w_in, ffn1_w_out, ffn2_w_in, ffn2_w_out = (
        w.astype(BF16) for w in (ffn1_w_in, ffn1_w_out, ffn2_w_in, ffn2_w_out))
    for layer in range(depth):
        s5_layer = layer < n_a
        h = _ffn(h, ffn1_norm[layer], ffn1_w_in, ffn1_w_out, layer, bsz, seq,
                 in_tm=False, out_tm=s5_layer)
        pre = None
        if s5_layer:
            a = layer
            h = _s5(h, mix_norm[layer], s5_a_re[a], s5_a_im[a], s5_log_dt[a], s5_b_re[a], s5_b_im[a],
                    s5_c_re[a], s5_c_im[a], s5_d[a], s5_w_glu[a], bsz, seq)
        else:
            b = layer - n_a
            q, gates = _qproj(h, mix_norm[layer], w_qg[b], q_norm[b], rel_bias, bsz, seq)
            pre = (_attention(q, gates, kv, rel_bias, bsz, seq), w_o[b])
        h = _ffn(h, ffn2_norm[layer], ffn2_w_in, ffn2_w_out, layer, bsz, seq,
                 in_tm=s5_layer, out_tm=False, pre=pre)
        if layer == n_a - 1:
            kc_raw, vc_raw, ks, vs, kw, vw = _kv_proj(h, kv_norm, w_kv, k_norm_slc, k_norm_win, bsz, seq)
            kc, vc = _compress(kc_raw, vc_raw, k_norm_cmp, cmp_pos_k, cmp_pos_v,
                               cmp_k_w1, cmp_k_w2, cmp_v_w1, cmp_v_w2, bsz, seq)
            kv = (kc, vc, ks, vs, kw, vw)
    return h.reshape(bsz, seq, D_MODEL)
```
